```python
import math
import jax, jax.numpy as jnp
from jax import lax
import numpy as np

D_MODEL = 1024
BATCH = 4
SEQ = 4096
DEPTH = 2
DEC_BATCH = 128
DEC_SEQ = 8
PAST_LEN = 2048
PAGE_SIZE = 128

N_MIXERS = 2
N_SSD_LAYERS = (DEPTH + 1) // 2
N_ATTN_LAYERS = DEPTH // 2
EPS = 1e-6
D_FF = 2816
D_INNER = 2 * D_MODEL
SSD_HEADDIM = 64
SSD_HEADS = D_INNER // SSD_HEADDIM
SSD_GROUPS = 8
SSD_HPG = SSD_HEADS // SSD_GROUPS
D_STATE = 128
CONV_W = 4
CONV_DIM = D_INNER + 2 * SSD_GROUPS * D_STATE
D_IN_PROJ = D_INNER + CONV_DIM + SSD_HEADS
SSD_CHUNK = 128
ATTN_PATTERNS = ((128, 1), (512, 4), (2048, 16))
N_ATTN_GROUPS = 3
ATTN_HPG = 8
ATTN_HEAD_DIM = 64
N_ATTN_HEADS = N_ATTN_GROUPS * ATTN_HPG
ATTN_WIDTH = ATTN_HPG * ATTN_HEAD_DIM
QKV_WIDTH = N_ATTN_GROUPS * 3 * ATTN_WIDTH
ATTN_QBLOCK = 128
NEG_INF = -1e30

kernel_name = 'hybrid_ssd_dilated_swa_macaron_step'


def rms_norm(x, g):
    xf = x.astype(jnp.float32)
    y = xf * lax.rsqrt(jnp.mean(xf * xf, axis=-1, keepdims=True) + EPS)
    return (y * g.astype(jnp.float32)).astype(x.dtype)


def half_swiglu(x, g, w_in, w_out):
    a, b = jnp.split(rms_norm(x, g) @ w_in, 2, axis=-1)
    return x + 0.5 * ((jax.nn.silu(a) * b) @ w_out)


def alibi_slopes():
    h = jnp.arange(1, N_ATTN_HEADS + 1, dtype=jnp.float32)
    return (2.0 ** (-8.0 * h / N_ATTN_HEADS)).reshape(N_ATTN_GROUPS, ATTN_HPG)


def ssd_chunked(x, dt, a, bm, cm, h0):
    b, l = x.shape[:2]
    q = SSD_CHUNK if l % SSD_CHUNK == 0 else l
    nc = l // q
    f32 = jnp.float32
    x = x.reshape(b, nc, q, SSD_GROUPS, SSD_HPG, SSD_HEADDIM).astype(f32)
    dt = dt.reshape(b, nc, q, SSD_GROUPS, SSD_HPG)
    bm = bm.reshape(b, nc, q, SSD_GROUPS, D_STATE).astype(f32)
    cm = cm.reshape(b, nc, q, SSD_GROUPS, D_STATE).astype(f32)
    a_cum = jnp.cumsum(dt * a, axis=2)
    causal = jnp.tril(jnp.ones((q, q), dtype=bool))[:, :, None, None]
    seg = a_cum[:, :, :, None] - a_cum[:, :, None, :]
    decay = jnp.exp(jnp.where(causal, seg, -jnp.inf))
    cb = jnp.einsum('bcqgn,bcsgn->bcqsg', cm, bm)
    wts = cb[..., None] * decay * dt[:, :, None]
    y_diag = jnp.einsum('bcqsgj,bcsgjp->bcqgjp', wts, x)
    xw = (jnp.exp(a_cum[:, :, -1:] - a_cum) * dt)[..., None] * x
    states = jnp.einsum('bcsgn,bcsgjp->bcgjpn', bm, xw)
    chunk_decay = jnp.exp(a_cum[:, :, -1])

    def step(h, inp):
        st, dec = inp
        return dec[..., None, None] * h + st, h

    h_last, h_prev = lax.scan(step, h0.astype(f32), (jnp.moveaxis(states, 1, 0), jnp.moveaxis(chunk_decay, 1, 0)))
    h_prev = jnp.moveaxis(h_prev, 0, 1)
    y_off = jnp.einsum('bcqgn,bcgjpn->bcqgjp', cm, h_prev) * jnp.exp(a_cum)[..., None]
    y = (y_diag + y_off).reshape(b, l, SSD_GROUPS, SSD_HPG, SSD_HEADDIM)
    return y, h_last


def ssd_mixer(u, conv_state, ssm_state, w_in, conv_w, conv_b, dt_bias, a_log, d_skip, norm_g, w_out):
    b, l, _ = u.shape
    f32 = jnp.float32
    proj = u @ w_in
    z = proj[..., :D_INNER]
    xbc = proj[..., D_INNER:D_INNER + CONV_DIM]
    dt_raw = proj[..., D_INNER + CONV_DIM:]
    xpad = jnp.concatenate([conv_state.astype(xbc.dtype), xbc], axis=1)
    xc = conv_b + sum(xpad[:, k:k + l] * conv_w[k] for k in range(CONV_W))
    xc = jax.nn.silu(xc)
    new_conv = xpad[:, l:]
    xs = xc[..., :D_INNER].reshape(b, l, SSD_GROUPS, SSD_HPG, SSD_HEADDIM)
    bm = xc[..., D_INNER:D_INNER + SSD_GROUPS * D_STATE].reshape(b, l, SSD_GROUPS, D_STATE)
    cm = xc[..., D_INNER + SSD_GROUPS * D_STATE:].reshape(b, l, SSD_GROUPS, D_STATE)
    dt = jax.nn.softplus(dt_raw.astype(f32) + dt_bias.astype(f32)).reshape(b, l, SSD_GROUPS, SSD_HPG)
    a = -jnp.exp(a_log.astype(f32)).reshape(SSD_GROUPS, SSD_HPG)
    h0 = ssm_state.reshape(b, SSD_GROUPS, SSD_HPG, SSD_HEADDIM, D_STATE)
    y, h_last = ssd_chunked(xs, dt, a, bm, cm, h0)
    y = y + d_skip.astype(f32).reshape(SSD_GROUPS, SSD_HPG)[:, :, None] * xs.astype(f32)
    gz = (y.reshape(b, l, D_INNER) * jax.nn.silu(z.astype(f32))).reshape(b, l, SSD_GROUPS, D_INNER // SSD_GROUPS)
    gz = gz * lax.rsqrt(jnp.mean(gz * gz, axis=-1, keepdims=True) + EPS)
    gz = (gz.reshape(b, l, D_INNER) * norm_g.astype(f32)).astype(u.dtype)
    new_ssm = h_last.reshape(b, SSD_HEADS, SSD_HEADDIM, D_STATE).astype(ssm_state.dtype)
    return gz @ w_out, new_conv, new_ssm


def dilated_group_attention(q, kvbuf, offset, win, dil, slopes):
    b, l, h, hd = q.shape
    nk = win // dil + 1
    qb = ATTN_QBLOCK if l % ATTN_QBLOCK == 0 else l
    starts = jnp.arange(l // qb) * qb
    j = jnp.arange(nk)
    bias = -(slopes.astype(jnp.float32) * dil)[:, None, None] * j.astype(jnp.float32)

    def block(start):
        qi = start + jnp.arange(qb)
        q_blk = lax.dynamic_slice_in_dim(q, start, qb, axis=1)
        idx = offset + qi[:, None] - dil * j[None, :]
        kv = jnp.take(kvbuf, jnp.maximum(idx, 0), axis=1)
        s = jnp.einsum('bqhd,bqkhd->bhqk', q_blk, kv[:, :, :, 0]).astype(jnp.float32) + bias
        s = jnp.where((idx >= 0)[None, None], s, NEG_INF)
        lse = jax.nn.logsumexp(s, axis=-1)
        p = jnp.exp(s - lse[..., None]).astype(kv.dtype)
        o = jnp.einsum('bhqk,bqkhd->bqhd', p, kv[:, :, :, 1])
        return o, lse

    o, lse = lax.map(block, starts)
    o = jnp.moveaxis(o, 0, 1).reshape(b, l, h, hd)
    lse = jnp.transpose(lse, (1, 0, 3, 2)).reshape(b, l, h)
    return o, lse


def dilated_mixer(u, past, w_qkv, w_o):
    b, l, _ = u.shape
    qkv = (u @ w_qkv).reshape(b, l, N_ATTN_GROUPS, 3, ATTN_HPG, ATTN_HEAD_DIM)
    slopes = alibi_slopes()
    outs, lses, new_rows = [], [], []
    for g, (win, dil) in enumerate(ATTN_PATTERNS):
        q = qkv[:, :, g, 0] * (ATTN_HEAD_DIM ** -0.5)
        kv_new = qkv[:, :, g, 1:]
        kvbuf = jnp.concatenate([past[g].astype(kv_new.dtype), kv_new], axis=1)
        o, lse = dilated_group_attention(q, kvbuf, past[g].shape[1], win, dil, slopes[g])
        outs.append(o)
        lses.append(lse)
        new_rows.append(kv_new)
    w = jax.nn.softmax(jnp.stack(lses), axis=0)
    o = jnp.sum(w[..., None] * jnp.stack(outs).astype(jnp.float32), axis=0).astype(u.dtype)
    return o.reshape(b, l, ATTN_WIDTH) @ w_o, new_rows


def setup_inputs(seed: int = 0) -> dict:
    key = jax.random.key(seed)
    ks = jax.random.split(key, 24)

    def nrm(k, shape, scale):
        return jax.random.normal(k, shape, jnp.float32) * scale

    dt0 = jnp.exp(jax.random.uniform(ks[13], (N_SSD_LAYERS, SSD_HEADS), jnp.float32, math.log(1e-3), math.log(1e-1)))
    return {
        'x_prompt': nrm(ks[0], (BATCH, SEQ, D_MODEL), 1.0),
        'x_sample': nrm(ks[1], (DEC_BATCH, DEC_SEQ, D_MODEL), 1.0),
        'state_conv': nrm(ks[2], (N_SSD_LAYERS, DEC_BATCH, CONV_W - 1, CONV_DIM), 1.0),
        'state_ssm': nrm(ks[3], (N_SSD_LAYERS, DEC_BATCH, SSD_HEADS, SSD_HEADDIM, D_STATE), 0.1),
        'cache_kv_g0': nrm(ks[4], (N_ATTN_LAYERS, DEC_BATCH, min(ATTN_PATTERNS[0][0], PAST_LEN), 2, ATTN_HPG, ATTN_HEAD_DIM), 1.0),
        'cache_kv_g1': nrm(ks[5], (N_ATTN_LAYERS, DEC_BATCH, min(ATTN_PATTERNS[1][0], PAST_LEN), 2, ATTN_HPG, ATTN_HEAD_DIM), 1.0),
        'cache_kv_g2': nrm(ks[6], (N_ATTN_LAYERS, DEC_BATCH, min(ATTN_PATTERNS[2][0], PAST_LEN), 2, ATTN_HPG, ATTN_HEAD_DIM), 1.0),
        'norm_w': 1.0 + nrm(ks[7], (DEPTH, 3, D_MODEL), 0.02),
        'w_ffn_in': nrm(ks[8], (DEPTH, 2, D_MODEL, 2 * D_FF), D_MODEL ** -0.5),
        'w_ffn_out': nrm(ks[9], (DEPTH, 2, D_FF, D_MODEL), D_FF ** -0.5),
        'ssm_w_in': nrm(ks[10], (N_SSD_LAYERS, D_MODEL, D_IN_PROJ), D_MODEL ** -0.5),
        'ssm_conv_w': nrm(ks[11], (N_SSD_LAYERS, CONV_W, CONV_DIM), CONV_W ** -0.5),
        'ssm_conv_b': nrm(ks[12], (N_SSD_LAYERS, CONV_DIM), 0.02),
        'ssm_dt_bias': dt0 + jnp.log(-jnp.expm1(-dt0)),
        'ssm_a_log': jnp.log(jax.random.uniform(ks[14], (N_SSD_LAYERS, SSD_HEADS), jnp.float32, 1.0, 16.0)),
        'ssm_d': 1.0 + nrm(ks[15], (N_SSD_LAYERS, SSD_HEADS), 0.1),
        'ssm_norm_w': 1.0 + nrm(ks[16], (N_SSD_LAYERS, D_INNER), 0.02),
        'ssm_w_out': nrm(ks[17], (N_SSD_LAYERS, D_INNER, D_MODEL), D_INNER ** -0.5),
        'attn_w_qkv': nrm(ks[18], (N_ATTN_LAYERS, D_MODEL, QKV_WIDTH), D_MODEL ** -0.5),
        'attn_w_o': nrm(ks[19], (N_ATTN_LAYERS, ATTN_WIDTH, D_MODEL), ATTN_WIDTH ** -0.5),
        'norm_f': 1.0 + nrm(ks[20], (D_MODEL,), 0.02),
    }


def reference(x_prompt, x_sample, state_conv, state_ssm, cache_kv_g0, cache_kv_g1, cache_kv_g2,
              norm_w, w_ffn_in, w_ffn_out, ssm_w_in, ssm_conv_w, ssm_conv_b, ssm_dt_bias, ssm_a_log,
              ssm_d, ssm_norm_w, ssm_w_out, attn_w_qkv, attn_w_o, norm_f):
    xp, xs = x_prompt, x_sample
    bp = xp.shape[0]
    caches = (cache_kv_g0, cache_kv_g1, cache_kv_g2)
    conv_p, conv_s, ssm_p, ssm_s = [], [], [], []
    kv_p = [[] for _ in range(N_ATTN_GROUPS)]
    kv_s = [[] for _ in range(N_ATTN_GROUPS)]
    for i in range(DEPTH):
        xp = half_swiglu(xp, norm_w[i, 0], w_ffn_in[i, 0], w_ffn_out[i, 0])
        xs = half_swiglu(xs, norm_w[i, 0], w_ffn_in[i, 0], w_ffn_out[i, 0])
        li = i // N_MIXERS
        up, us = rms_norm(xp, norm_w[i, 1]), rms_norm(xs, norm_w[i, 1])
        if i % N_MIXERS == 0:
            prm = (ssm_w_in[li], ssm_conv_w[li], ssm_conv_b[li], ssm_dt_bias[li], ssm_a_log[li],
                   ssm_d[li], ssm_norm_w[li], ssm_w_out[li])
            zc = jnp.zeros((bp, CONV_W - 1, CONV_DIM), xp.dtype)
            zh = jnp.zeros((bp, SSD_HEADS, SSD_HEADDIM, D_STATE), jnp.float32)
            yp, cp, hp = ssd_mixer(up, zc, zh, *prm)
            ys, cs, hs = ssd_mixer(us, state_conv[li], state_ssm[li], *prm)
            conv_p.append(cp)
            conv_s.append(cs)
            ssm_p.append(hp)
            ssm_s.append(hs)
        else:
            empty = [jnp.zeros((bp, 0, 2, ATTN_HPG, ATTN_HEAD_DIM), xp.dtype) for _ in range(N_ATTN_GROUPS)]
            yp, rows_p = dilated_mixer(up, empty, attn_w_qkv[li], attn_w_o[li])
            ys, rows_s = dilated_mixer(us, [c[li] for c in caches], attn_w_qkv[li], attn_w_o[li])
            for g, (win, _) in enumerate(ATTN_PATTERNS):
                lp = rows_p[g].shape[1]
                kv_p[g].append(rows_p[g][:, lp - min(win, lp):])
                kv_s[g].append(rows_s[g])
        xp = xp + yp
        xs = xs + ys
        xp = half_swiglu(xp, norm_w[i, 2], w_ffn_in[i, 1], w_ffn_out[i, 1])
        xs = half_swiglu(xs, norm_w[i, 2], w_ffn_in[i, 1], w_ffn_out[i, 1])
    y_prompt = rms_norm(xp, norm_f)
    y_sample = rms_norm(xs, norm_f)
    return (y_prompt, y_sample,
            jnp.stack(conv_p), jnp.stack(conv_s), jnp.stack(ssm_p), jnp.stack(ssm_s),
            jnp.stack(kv_p[0]), jnp.stack(kv_s[0]), jnp.stack(kv_p[1]), jnp.stack(kv_s[1]),
            jnp.stack(kv_p[2]), jnp.stack(kv_s[2]))
```

```python
import functools

import jax
import jax.numpy as jnp
from jax import lax
from jax.experimental import pallas as pl
from jax.experimental.pallas import tpu as pltpu

F32 = jnp.float32
BF16 = jnp.bfloat16

EPS = 1e-6
NEG_INF = -1e30

D_MODEL = 1024
D_FF = 2816
D_INNER = 2048
SSD_HEADS = 32
SSD_HEADDIM = 64
SSD_GROUPS = 8
SSD_HPG = SSD_HEADS // SSD_GROUPS
GROUP_W = SSD_HPG * SSD_HEADDIM
D_STATE = 128
CONV_W = 4
CONV_DIM = D_INNER + 2 * SSD_GROUPS * D_STATE
SSD_CHUNK = 128
ATTN_PATTERNS = ((128, 1), (512, 4), (2048, 16))
N_ATTN_GROUPS = 3
ATTN_HPG = 8
ATTN_HEAD_DIM = 64
ATTN_WIDTH = ATTN_HPG * ATTN_HEAD_DIM
QKV_WIDTH = N_ATTN_GROUPS * 3 * ATTN_WIDTH
ATTN_QBLOCK = 128
ATTN_NKEYS = 128

LANES = 128
VMEM_LIMIT = 56 * 1024 * 1024

NT_DIMS = (((1,), (1,)), ((), ()))
TN_DIMS = (((0,), (0,)), ((), ()))


def _const_spec(shape):
    zeros = (0,) * len(shape)
    return pl.BlockSpec(shape, lambda *_: zeros, pipeline_mode=pl.Buffered(1))


def _params(*semantics):
    return pltpu.CompilerParams(dimension_semantics=semantics, vmem_limit_bytes=VMEM_LIMIT)


def _rms(x, g):
    return x * lax.rsqrt(jnp.mean(x * x, axis=-1, keepdims=True) + EPS) * g


def _silu(x):
    return x * jax.nn.sigmoid(x)


def _softplus(x):
    return jnp.maximum(x, 0.0) + jnp.log1p(jnp.exp(-jnp.abs(x)))


def _ffn_body(x_ref, g_ref, win_ref, wout_ref, *rest, final_norm):
    if final_norm:
        gf_ref, o_ref = rest
    else:
        (o_ref,) = rest
    x = x_ref[...]
    h = _rms(x, g_ref[...]).astype(BF16)
    a = jnp.dot(h, win_ref[:, :D_FF], preferred_element_type=F32)
    b = jnp.dot(h, win_ref[:, D_FF:], preferred_element_type=F32)
    t = (_silu(a) * b).astype(BF16)
    y = x + 0.5 * jnp.dot(t, wout_ref[...], preferred_element_type=F32)
    if final_norm:
        y = _rms(y, gf_ref[...])
    o_ref[...] = y


def _ffn(x, g, w_in, w_out, g_final=None, tm=512):
    m, d = x.shape
    tm = min(tm, m)
    row = pl.BlockSpec((tm, d), lambda i: (i, 0))
    in_specs = [row, _const_spec((1, d)), _const_spec(w_in.shape), _const_spec(w_out.shape)]
    args = [x, g.reshape(1, d), w_in, w_out]
    if g_final is not None:
        in_specs.append(_const_spec((1, d)))
        args.append(g_final.reshape(1, d))
    return pl.pallas_call(
        functools.partial(_ffn_body, final_norm=g_final is not None),
        out_shape=jax.ShapeDtypeStruct((m, d), F32),
        grid=(m // tm,),
        in_specs=in_specs,
        out_specs=row,
        compiler_params=_params("parallel"),
        name="ffn",
    )(*args)


def _norm_mm_body(x_ref, g_ref, w_ref, *o_refs, splits):
    h = _rms(x_ref[...], g_ref[...]).astype(BF16)
    off = 0
    for o_ref, n in zip(o_refs, splits):
        o_ref[...] = jnp.dot(h, w_ref[:, off:off + n], preferred_element_type=F32)
        off += n


def _norm_mm(x, g, w, splits, tm=256):
    m, d = x.shape
    tm = min(tm, m)
    return pl.pallas_call(
        functools.partial(_norm_mm_body, splits=splits),
        out_shape=[jax.ShapeDtypeStruct((m, n), F32) for n in splits],
        grid=(m // tm,),
        in_specs=[pl.BlockSpec((tm, d), lambda i: (i, 0)), _const_spec((1, d)), _const_spec(w.shape)],
        out_specs=[pl.BlockSpec((tm, n), lambda i: (i, 0)) for n in splits],
        compiler_params=_params("parallel"),
        name="norm_mm",
    )(x, g.reshape(1, d), w)


def _mm_res_body(y_ref, w_ref, x_ref, o_ref):
    o_ref[...] = x_ref[...] + jnp.dot(y_ref[...].astype(BF16), w_ref[...], preferred_element_type=F32)


def _mm_res(y, w, x, tm=512):
    m, k = y.shape
    d = x.shape[1]
    tm = min(tm, m)
    return pl.pallas_call(
        _mm_res_body,
        out_shape=jax.ShapeDtypeStruct((m, d), F32),
        grid=(m // tm,),
        in_specs=[pl.BlockSpec((tm, k), lambda i: (i, 0)), _const_spec(w.shape),
                  pl.BlockSpec((tm, d), lambda i: (i, 0))],
        out_specs=pl.BlockSpec((tm, d), lambda i: (i, 0)),
        compiler_params=_params("parallel"),
        name="mm_res",
    )(y, w, x)


def _expand_heads(v, first, width):
    rows = v.shape[0]
    return jnp.concatenate(
        [jnp.broadcast_to(v[:, first + j:first + j + 1], (rows, width)) for j in range(SSD_HPG)], axis=1)


def _ssd_chunk(q, xcur, xm1, xm2, xm3, z, dt_raw, cw_ref, cb_ref, dtb_ref, alog_ref, drep_ref, ng_ref,
               h_in_ref, h_out_ref, gz_ref):
    cw = cw_ref[...]
    xc = _silu(cb_ref[...] + cw[0:1] * xm3 + cw[1:2] * xm2 + cw[2:3] * xm1 + cw[3:4] * xcur)
    xs = xc[:, :D_INNER]
    bm = xc[:, D_INNER:D_INNER + SSD_GROUPS * D_STATE]
    cm = xc[:, D_INNER + SSD_GROUPS * D_STATE:]

    dt = _softplus(dt_raw + dtb_ref[...])
    a = -jnp.exp(alog_ref[...])
    row = lax.broadcasted_iota(jnp.int32, (q, q), 0)
    col = lax.broadcasted_iota(jnp.int32, (q, q), 1)
    causal = row >= col
    a_cum = jnp.dot(causal.astype(F32), dt * a, precision=lax.Precision.HIGHEST, preferred_element_type=F32)
    a_last = a_cum[q - 1:q, :]
    w_state = jnp.exp(a_last - a_cum) * dt
    e_cum = jnp.exp(a_cum)
    e_last = jnp.exp(a_last)
    a_cum_t = a_cum.T
    dt_t = dt.T

    for g in range(SSD_GROUPS):
        h0 = g * SSD_HPG
        c0 = g * GROUP_W
        bmb = bm[:, g * D_STATE:(g + 1) * D_STATE].astype(BF16)
        cmb = cm[:, g * D_STATE:(g + 1) * D_STATE].astype(BF16)
        cb = lax.dot_general(cmb, bmb, NT_DIMS, preferred_element_type=F32)
        x_g = xs[:, c0:c0 + GROUP_W]
        y_heads = []
        for j in range(SSD_HPG):
            h = h0 + j
            seg = a_cum[:, h:h + 1] - a_cum_t[h:h + 1, :]
            decay = jnp.exp(jnp.where(causal, seg, -jnp.inf))
            wts = (cb * decay * dt_t[h:h + 1, :]).astype(BF16)
            x_h = x_g[:, j * SSD_HEADDIM:(j + 1) * SSD_HEADDIM].astype(BF16)
            y_heads.append(jnp.dot(wts, x_h, preferred_element_type=F32))
        y_diag = jnp.concatenate(y_heads, axis=1)

        h_prev = h_in_ref[c0:c0 + GROUP_W, :]
        y_off = lax.dot_general(cmb, h_prev.astype(BF16), NT_DIMS, preferred_element_type=F32)
        y_off = y_off * _expand_heads(e_cum, h0, SSD_HEADDIM)
        xw = (x_g * _expand_heads(w_state, h0, SSD_HEADDIM)).astype(BF16)
        states = lax.dot_general(xw, bmb, TN_DIMS, preferred_element_type=F32)
        carry = jnp.concatenate(
            [jnp.broadcast_to(e_last[:, h0 + j:h0 + j + 1], (SSD_HEADDIM, D_STATE)) for j in range(SSD_HPG)], axis=0)
        h_out_ref[c0:c0 + GROUP_W, :] = carry * h_prev + states

        y = y_diag + y_off + drep_ref[:, c0:c0 + GROUP_W] * x_g
        gz = y * _silu(z[:, c0:c0 + GROUP_W])
        gz = gz * lax.rsqrt(jnp.mean(gz * gz, axis=-1, keepdims=True) + EPS) * ng_ref[:, c0:c0 + GROUP_W]
        gz_ref[:, c0:c0 + GROUP_W] = gz.astype(gz_ref.dtype)


PAD_ROWS = 8


def _ssd_body(z_ref, xbc_ref, dt_ref, cs_ref, h0_ref, cw_ref, cb_ref, dtb_ref, alog_ref, drep_ref, ng_ref,
              gz_ref, nc_ref, hout_ref, xpad_ref, *, q, single_chunk):
    hist = CONV_W - 1
    if single_chunk:
        xpad_ref[PAD_ROWS - hist:PAD_ROWS, :] = cs_ref[0]
        h_in = h0_ref.at[0]
    else:
        @pl.when(pl.program_id(1) == 0)
        def _():
            xpad_ref[PAD_ROWS - hist:PAD_ROWS, :] = cs_ref[0]
            hout_ref[0] = h0_ref[0]
        h_in = hout_ref.at[0]
    xcur = xbc_ref[...]
    xpad_ref[PAD_ROWS:PAD_ROWS + q, :] = xcur
    xm1 = xpad_ref[PAD_ROWS - 1:PAD_ROWS - 1 + q, :]
    xm2 = xpad_ref[PAD_ROWS - 2:PAD_ROWS - 2 + q, :]
    xm3 = xpad_ref[PAD_ROWS - 3:PAD_ROWS - 3 + q, :]
    _ssd_chunk(q, xcur, xm1, xm2, xm3, z_ref[...], dt_ref[...], cw_ref, cb_ref, dtb_ref, alog_ref, drep_ref,
               ng_ref, h_in, hout_ref.at[0], gz_ref)
    tail = xpad_ref[PAD_ROWS + q - hist:PAD_ROWS + q, :]
    nc_ref[0] = tail
    if not single_chunk:
        xpad_ref[PAD_ROWS - hist:PAD_ROWS, :] = tail


def _ssd(z, xbc, dt_raw, conv_state, ssm_state, prm, batch, seq, gz_dtype):
    q = SSD_CHUNK if seq % SSD_CHUNK == 0 else seq
    nc = seq // q
    hist = CONV_W - 1
    hp = SSD_HEADS * SSD_HEADDIM
    row = lambda b, c: (b * nc + c, 0)
    per_b = lambda b, c: (b, 0, 0)
    cw, cb, dtb, alog, drep, ng = prm
    return pl.pallas_call(
        functools.partial(_ssd_body, q=q, single_chunk=nc == 1),
        out_shape=[jax.ShapeDtypeStruct((batch * seq, D_INNER), gz_dtype),
                   jax.ShapeDtypeStruct((batch, hist, CONV_DIM), F32),
                   jax.ShapeDtypeStruct((batch, hp, D_STATE), F32)],
        grid=(batch, nc),
        in_specs=[pl.BlockSpec((q, D_INNER), row), pl.BlockSpec((q, CONV_DIM), row), pl.BlockSpec((q, LANES), row),
                  pl.BlockSpec((1, hist, CONV_DIM), per_b), pl.BlockSpec((1, hp, D_STATE), per_b),
                  _const_spec(cw.shape), _const_spec(cb.shape), _const_spec(dtb.shape), _const_spec(alog.shape),
                  _const_spec(drep.shape), _const_spec(ng.shape)],
        out_specs=[pl.BlockSpec((q, D_INNER), row), pl.BlockSpec((1, hist, CONV_DIM), per_b),
                   pl.BlockSpec((1, hp, D_STATE), per_b)],
        scratch_shapes=[pltpu.VMEM((PAD_ROWS + q, CONV_DIM), F32)],
        compiler_params=_params("parallel", "arbitrary"),
        name="ssd",
    )(z, xbc, dt_raw, conv_state, ssm_state, cw, cb, dtb, alog, drep, ng)


def _attn_prompt_body(q_ref, k_ref, v_ref, o_ref, lse_ref, k_buf, v_buf, *, slope_dil):
    i = pl.program_id(2)
    qb = ATTN_QBLOCK

    @pl.when(i == 0)
    def _():
        k_buf[0:qb, :] = jnp.zeros((qb, ATTN_WIDTH), BF16)
        v_buf[0:qb, :] = jnp.zeros((qb, ATTN_WIDTH), BF16)

    k_buf[qb:2 * qb, :] = k_ref[0].astype(BF16)
    v_buf[qb:2 * qb, :] = v_ref[0].astype(BF16)
    q = q_ref[0] * (ATTN_HEAD_DIM ** -0.5)

    row = lax.broadcasted_iota(jnp.int32, (qb, 2 * qb), 0)
    col = lax.broadcasted_iota(jnp.int32, (qb, 2 * qb), 1)
    dist = qb + row - col
    first_col = jnp.where(i > 0, 0, qb)
    valid = (dist >= 0) & (dist <= ATTN_NKEYS) & (col >= first_col)
    dist_f = dist.astype(F32)
    lane = lax.broadcasted_iota(jnp.int32, (qb, LANES), 1)
    low_half = lane < ATTN_HEAD_DIM
    lse_tile = jnp.zeros((qb, LANES), F32)

    for t in range(ATTN_HPG // 2):
        sl = slice(t * LANES, (t + 1) * LANES)
        q2 = q[:, sl]
        k2 = k_buf[:, sl]
        v2 = v_buf[:, sl]
        outs = []
        for e in range(2):
            h = 2 * t + e
            qm = jnp.where(low_half if e == 0 else ~low_half, q2, 0.0).astype(BF16)
            s = lax.dot_general(qm, k2, NT_DIMS, preferred_element_type=F32) - slope_dil[h] * dist_f
            s = jnp.where(valid, s, NEG_INF)
            m = jnp.max(s, axis=-1, keepdims=True)
            p = jnp.exp(s - m)
            l = jnp.sum(p, axis=-1, keepdims=True)
            outs.append(jnp.dot(p.astype(BF16), v2, preferred_element_type=F32) / l)
            lse_tile = jnp.where(lane == h, m + jnp.log(l), lse_tile)
        o_ref[0, :, sl] = jnp.where(low_half, outs[0], outs[1])
    lse_ref[0] = lse_tile
    k_buf[0:qb, :] = k_buf[qb:2 * qb, :]
    v_buf[0:qb, :] = v_buf[qb:2 * qb, :]


def _attn_prompt(qkv, g, batch, seq, slope_dil):
    dil = ATTN_PATTERNS[g][1]
    sub = seq // dil
    qkv_v = qkv.reshape(batch, sub, dil * QKV_WIDTH)
    blocks_per_tok = QKV_WIDTH // ATTN_WIDTH
    base = 3 * g

    def spec(which):
        return pl.BlockSpec((1, ATTN_QBLOCK, ATTN_WIDTH),
                            lambda b, r, i: (b, i, r * blocks_per_tok + base + which))

    o, lse = pl.pallas_call(
        functools.partial(_attn_prompt_body, slope_dil=slope_dil),
        out_shape=[jax.ShapeDtypeStruct((batch, sub, dil * ATTN_WIDTH), F32),
                   jax.ShapeDtypeStruct((batch, sub, dil * LANES), F32)],
        grid=(batch, dil, sub // ATTN_QBLOCK),
        in_specs=[spec(0), spec(1), spec(2)],
        out_specs=[pl.BlockSpec((1, ATTN_QBLOCK, ATTN_WIDTH), lambda b, r, i: (b, i, r)),
                   pl.BlockSpec((1, ATTN_QBLOCK, LANES), lambda b, r, i: (b, i, r))],
        scratch_shapes=[pltpu.VMEM((2 * ATTN_QBLOCK, ATTN_WIDTH), BF16),
                        pltpu.VMEM((2 * ATTN_QBLOCK, ATTN_WIDTH), BF16)],
        compiler_params=_params("parallel", "parallel", "arbitrary"),
        name=f"attn_prompt_g{g}",
    )(qkv_v, qkv_v, qkv_v)
    return o.reshape(batch * seq, ATTN_WIDTH), lse.reshape(batch * seq, LANES)


def _merge_mm_res_body(o0_ref, o1_ref, o2_ref, l0_ref, l1_ref, l2_ref, w_ref, x_ref, out_ref):
    lses = [l0_ref[...], l1_ref[...], l2_ref[...]]
    m = jnp.maximum(jnp.maximum(lses[0], lses[1]), lses[2])
    es = [jnp.exp(l - m) for l in lses]
    den = es[0] + es[1] + es[2]
    rows = x_ref.shape[0]
    acc = jnp.zeros((rows, ATTN_WIDTH), F32)
    for e, o_ref in zip(es, (o0_ref, o1_ref, o2_ref)):
        w = e / den
        w_full = jnp.concatenate(
            [jnp.broadcast_to(w[:, h:h + 1], (rows, ATTN_HEAD_DIM)) for h in range(ATTN_HPG)], axis=1)
        acc = acc + w_full * o_ref[...]
    out_ref[...] = x_ref[...] + jnp.dot(acc.astype(BF16), w_ref[...], preferred_element_type=F32)


def _merge_mm_res(os_, lses, w, x, tm=512):
    m, d = x.shape
    tm = min(tm, m)
    o_spec = pl.BlockSpec((tm, ATTN_WIDTH), lambda i: (i, 0))
    l_spec = pl.BlockSpec((tm, LANES), lambda i: (i, 0))
    x_spec = pl.BlockSpec((tm, d), lambda i: (i, 0))
    return pl.pallas_call(
        _merge_mm_res_body,
        out_shape=jax.ShapeDtypeStruct((m, d), F32),
        grid=(m // tm,),
        in_specs=[o_spec] * 3 + [l_spec] * 3 + [_const_spec(w.shape), x_spec],
        out_specs=x_spec,
        compiler_params=_params("parallel"),
        name="merge_mm_res",
    )(*os_, *lses, w, x)


def _attn_sample_body(qkv_ref, c0_ref, c1_ref, c2_ref, slope_ref, o_ref, *, n_new):
    qkv = qkv_ref[...]
    caches = (c0_ref, c1_ref, c2_ref)
    head = lax.broadcasted_iota(jnp.int32, (ATTN_HPG, ATTN_WIDTH), 0)
    lane_head = lax.broadcasted_iota(jnp.int32, (ATTN_HPG, ATTN_WIDTH), 1) // ATTN_HEAD_DIM
    diag = head == lane_head
    m_past = lax.broadcasted_iota(jnp.int32, (ATTN_HPG, ATTN_NKEYS), 1)
    i_new = lax.broadcasted_iota(jnp.int32, (ATTN_HPG, n_new), 1)
    rows = []
    for i in range(n_new):
        outs, lses = [], []
        for g, (_, dil) in enumerate(ATTN_PATTERNS):
            base = g * 3 * ATTN_WIDTH
            r = i % dil
            lead = (i - r) // dil
            qrow = qkv[i:i + 1, base:base + ATTN_WIDTH] * (ATTN_HEAD_DIM ** -0.5)
            qbd = jnp.where(diag, jnp.broadcast_to(qrow, (ATTN_HPG, ATTN_WIDTH)), 0.0).astype(BF16)
            k_new = qkv[:, base + ATTN_WIDTH:base + 2 * ATTN_WIDTH].astype(BF16)
            v_new = qkv[:, base + 2 * ATTN_WIDTH:base + 3 * ATTN_WIDTH].astype(BF16)
            past = caches[g]
            k_past = past[0, :, r * 2 * ATTN_WIDTH:r * 2 * ATTN_WIDTH + ATTN_WIDTH].astype(BF16)
            v_past = past[0, :, r * 2 * ATTN_WIDTH + ATTN_WIDTH:(r + 1) * 2 * ATTN_WIDTH].astype(BF16)
            slope = slope_ref[g]
            dist_p = ATTN_NKEYS + lead - m_past
            s_p = lax.dot_general(qbd, k_past, NT_DIMS, preferred_element_type=F32)
            s_p = s_p - slope * dist_p.astype(F32)
            s_p = jnp.where(dist_p <= ATTN_NKEYS, s_p, NEG_INF)
            gap = i - i_new
            dist_n = gap // dil if dil > 1 else gap
            ok_n = (gap >= 0) & ((gap % dil == 0) if dil > 1 else True)
            s_n = lax.dot_general(qbd, k_new, NT_DIMS, preferred_element_type=F32)
            s_n = s_n - slope[:, :n_new] * dist_n.astype(F32)
            s_n = jnp.where(ok_n, s_n, NEG_INF)
            mx = jnp.maximum(jnp.max(s_p, axis=-1, keepdims=True), jnp.max(s_n, axis=-1, keepdims=True))
            e_p = jnp.exp(s_p - mx)
            e_n = jnp.exp(s_n - mx)
            l = jnp.sum(e_p, axis=-1, keepdims=True) + jnp.sum(e_n, axis=-1, keepdims=True)
            o = (jnp.dot(e_p.astype(BF16), v_past, preferred_element_type=F32)
                 + jnp.dot(e_n.astype(BF16), v_new, preferred_element_type=F32)) / l
            outs.append(o)
            lses.append(mx + jnp.log(l))
        top = jnp.maximum(jnp.maximum(lses[0], lses[1]), lses[2])
        es = [jnp.exp(l - top) for l in lses]
        den = es[0] + es[1] + es[2]
        merged = (es[0] * outs[0] + es[1] * outs[1] + es[2] * outs[2]) / den
        rows.append(jnp.sum(jnp.where(diag, merged, 0.0), axis=0, keepdims=True))
    o_ref[...] = jnp.concatenate(rows, axis=0)


def _attn_sample(qkv, caches, slope_tab, batch, n_new):
    kv_w = 2 * ATTN_WIDTH
    views, specs = [], []
    for c, (win, dil) in zip(caches, ATTN_PATTERNS):
        rows = win // dil
        views.append(c.reshape(batch, rows, dil * kv_w))
        n_res = min(dil, n_new)
        specs.append(pl.BlockSpec((1, rows, n_res * kv_w), lambda b: (b, 0, 0)))
    return pl.pallas_call(
        functools.partial(_attn_sample_body, n_new=n_new),
        out_shape=jax.ShapeDtypeStruct((batch * n_new, ATTN_WIDTH), F32),
        grid=(batch,),
        in_specs=[pl.BlockSpec((n_new, QKV_WIDTH), lambda b: (b, 0))] + specs + [_const_spec(slope_tab.shape)],
        out_specs=pl.BlockSpec((n_new, ATTN_WIDTH), lambda b: (b, 0)),
        compiler_params=_params("parallel"),
        name="attn_sample",
    )(qkv, *views, slope_tab)


def _alibi_slopes():
    n_heads = N_ATTN_GROUPS * ATTN_HPG
    return [[2.0 ** (-8.0 * (g * ATTN_HPG + h + 1) / n_heads) for h in range(ATTN_HPG)]
            for g in range(N_ATTN_GROUPS)]


def kernel(x_prompt, x_sample, state_conv, state_ssm, cache_kv_g0, cache_kv_g1, cache_kv_g2, norm_w, w_ffn_in,
           w_ffn_out, ssm_w_in, ssm_conv_w, ssm_conv_b, ssm_dt_bias, ssm_a_log, ssm_d, ssm_norm_w, ssm_w_out,
           attn_w_qkv, attn_w_o, norm_f):
    bp, lp, d = x_prompt.shape
    bs, ls, _ = x_sample.shape
    xs_all = [x_prompt.reshape(bp * lp, d), x_sample.reshape(bs * ls, d)]
    dims = [(bp, lp), (bs, ls)]
    hist = CONV_W - 1
    hp = SSD_HEADS * SSD_HEADDIM

    w_in = w_ffn_in.astype(BF16)
    w_out = w_ffn_out.astype(BF16)
    pad = LANES - SSD_HEADS
    w_proj = jnp.pad(ssm_w_in[0], ((0, 0), (0, pad))).astype(BF16)
    ssd_prm = (ssm_conv_w[0], ssm_conv_b[0].reshape(1, CONV_DIM),
               jnp.pad(ssm_dt_bias[0], (0, pad)).reshape(1, LANES),
               jnp.pad(ssm_a_log[0], (0, pad)).reshape(1, LANES),
               jnp.repeat(ssm_d[0], SSD_HEADDIM).reshape(1, D_INNER),
               ssm_norm_w[0].reshape(1, D_INNER))
    w_ssm_out = ssm_w_out[0].astype(BF16)
    w_qkv = attn_w_qkv[0].astype(BF16)
    w_o = attn_w_o[0].astype(BF16)
    slopes = _alibi_slopes()
    slope_dil = [[s * dil for s in slopes[g]] for g, (_, dil) in enumerate(ATTN_PATTERNS)]
    slope_tab = jnp.broadcast_to(jnp.asarray(slope_dil, F32)[:, :, None], (N_ATTN_GROUPS, ATTN_HPG, LANES))

    conv_states = [jnp.zeros((bp, hist, CONV_DIM), F32), state_conv[0]]
    ssm_states = [jnp.zeros((bp, hp, D_STATE), F32), state_ssm[0].reshape(bs, hp, D_STATE)]
    caches = [c[0] for c in (cache_kv_g0, cache_kv_g1, cache_kv_g2)]

    conv_out, ssm_out = [], []
    for n, (x, (b, l)) in enumerate(zip(xs_all, dims)):
        x = _ffn(x, norm_w[0, 0], w_in[0, 0], w_out[0, 0])
        z, xbc, dt_raw = _norm_mm(x, norm_w[0, 1], w_proj, (D_INNER, CONV_DIM, LANES))
        gz, new_conv, new_ssm = _ssd(z, xbc, dt_raw, conv_states[n], ssm_states[n], ssd_prm, b, l,
                                     BF16 if l % SSD_CHUNK == 0 else F32)
        conv_out.append(new_conv[None])
        ssm_out.append(new_ssm.reshape(1, b, SSD_HEADS, SSD_HEADDIM, D_STATE))
        x = _mm_res(gz, w_ssm_out, x)
        xs_all[n] = _ffn(x, norm_w[0, 2], w_in[0, 1], w_out[0, 1])

    kv_out = []
    for n, (x, (b, l)) in enumerate(zip(xs_all, dims)):
        x = _ffn(x, norm_w[1, 0], w_in[1, 0], w_out[1, 0])
        (qkv,) = _norm_mm(x, norm_w[1, 1], w_qkv, (QKV_WIDTH,))
        if n == 0:
            res = [_attn_prompt(qkv, g, b, l, slope_dil[g]) for g in range(N_ATTN_GROUPS)]
            x = _merge_mm_res([o for o, _ in res], [s for _, s in res], w_o, x)
        else:
            o = _attn_sample(qkv, caches, slope_tab, b, l)
            x = _mm_res(o, w_o, x)
        xs_all[n] = _ffn(x, norm_w[1, 2], w_in[1, 1], w_out[1, 1], g_final=norm_f)
        qkv5 = qkv.reshape(b, l, N_ATTN_GROUPS, 3, ATTN_HPG, ATTN_HEAD_DIM)
        kv_out.append([qkv5[:, l - min(win, l):, g, 1:][None] for g, (win, _) in enumerate(ATTN_PATTERNS)])

    return (xs_all[0].reshape(bp, lp, d), xs_all[1].reshape(bs, ls, d),
            conv_out[0], conv_out[1], ssm_out[0], ssm_out[1],
            kv_out[0][0], kv_out[1][0], kv_out[0][1], kv_out[1][1], kv_out[0][2], kv_out[1][2])
```

```python
import functools

import jax
import jax.numpy as jnp
from jax import lax
from jax.experimental import pallas as pl
from jax.experimental.pallas import tpu as pltpu

F32 = jnp.float32
BF16 = jnp.bfloat16

EPS = 1e-6
NEG_INF = -1e30

D_MODEL = 1024
D_FF = 2816
D_INNER = 2048
SSD_HEADS = 32
SSD_HEADDIM = 64
SSD_GROUPS = 8
SSD_HPG = SSD_HEADS // SSD_GROUPS
GROUP_W = SSD_HPG * SSD_HEADDIM
D_STATE = 128
CONV_W = 4
CONV_DIM = D_INNER + 2 * SSD_GROUPS * D_STATE
SSD_CHUNK = 128
ATTN_PATTERNS = ((128, 1), (512, 4), (2048, 16))
N_ATTN_GROUPS = 3
ATTN_HPG = 8
ATTN_HEAD_DIM = 64
ATTN_WIDTH = ATTN_HPG * ATTN_HEAD_DIM
QKV_WIDTH = N_ATTN_GROUPS * 3 * ATTN_WIDTH
ATTN_QBLOCK = 128
ATTN_NKEYS = 128

LANES = 128
VMEM_LIMIT = 56 * 1024 * 1024

NT_DIMS = (((1,), (1,)), ((), ()))
TN_DIMS = (((0,), (0,)), ((), ()))


def _const_spec(shape):
    zeros = (0,) * len(shape)
    return pl.BlockSpec(shape, lambda *_: zeros, pipeline_mode=pl.Buffered(1))


def _params(*semantics):
    return pltpu.CompilerParams(dimension_semantics=semantics, vmem_limit_bytes=VMEM_LIMIT)


def _rms(x, g):
    return x * lax.rsqrt(jnp.mean(x * x, axis=-1, keepdims=True) + EPS) * g


def _silu(x):
    return x * jax.nn.sigmoid(x)


def _softplus(x):
    return jnp.maximum(x, 0.0) + jnp.log1p(jnp.exp(-jnp.abs(x)))


def _ffn_body(x_ref, g_ref, win_ref, wout_ref, *rest, final_norm):
    if final_norm:
        gf_ref, o_ref = rest
    else:
        (o_ref,) = rest
    x = x_ref[...]
    h = _rms(x, g_ref[...]).astype(BF16)
    a = jnp.dot(h, win_ref[:, :D_FF], preferred_element_type=F32)
    b = jnp.dot(h, win_ref[:, D_FF:], preferred_element_type=F32)
    t = (_silu(a) * b).astype(BF16)
    y = x + 0.5 * jnp.dot(t, wout_ref[...], preferred_element_type=F32)
    if final_norm:
        y = _rms(y, gf_ref[...])
    o_ref[...] = y


def _ffn(x, g, w_in, w_out, g_final=None, tm=512):
    m, d = x.shape
    tm = min(tm, m)
    row = pl.BlockSpec((tm, d), lambda i: (i, 0))
    in_specs = [row, _const_spec((1, d)), _const_spec(w_in.shape), _const_spec(w_out.shape)]
    args = [x, g.reshape(1, d), w_in, w_out]
    if g_final is not None:
        in_specs.append(_const_spec((1, d)))
        args.append(g_final.reshape(1, d))
    return pl.pallas_call(
        functools.partial(_ffn_body, final_norm=g_final is not None),
        out_shape=jax.ShapeDtypeStruct((m, d), F32),
        grid=(m // tm,),
        in_specs=in_specs,
        out_specs=row,
        compiler_params=_params("parallel"),
        name="ffn",
    )(*args)


def _norm_mm_body(x_ref, g_ref, w_ref, *o_refs, splits):
    h = _rms(x_ref[...], g_ref[...]).astype(BF16)
    off = 0
    for o_ref, n in zip(o_refs, splits):
        o_ref[...] = jnp.dot(h, w_ref[:, off:off + n], preferred_element_type=F32)
        off += n


def _norm_mm(x, g, w, splits, tm=256):
    m, d = x.shape
    tm = min(tm, m)
    return pl.pallas_call(
        functools.partial(_norm_mm_body, splits=splits),
        out_shape=[jax.ShapeDtypeStruct((m, n), F32) for n in splits],
        grid=(m // tm,),
        in_specs=[pl.BlockSpec((tm, d), lambda i: (i, 0)), _const_spec((1, d)), _const_spec(w.shape)],
        out_specs=[pl.BlockSpec((tm, n), lambda i: (i, 0)) for n in splits],
        compiler_params=_params("parallel"),
        name="norm_mm",
    )(x, g.reshape(1, d), w)


def _mm_res_body(y_ref, w_ref, x_ref, o_ref):
    o_ref[...] = x_ref[...] + jnp.dot(y_ref[...].astype(BF16), w_ref[...], preferred_element_type=F32)


def _mm_res(y, w, x, tm=512):
    m, k = y.shape
    d = x.shape[1]
    tm = min(tm, m)
    return pl.pallas_call(
        _mm_res_body,
        out_shape=jax.ShapeDtypeStruct((m, d), F32),
        grid=(m // tm,),
        in_specs=[pl.BlockSpec((tm, k), lambda i: (i, 0)), _const_spec(w.shape),
                  pl.BlockSpec((tm, d), lambda i: (i, 0))],
        out_specs=pl.BlockSpec((tm, d), lambda i: (i, 0)),
        compiler_params=_params("parallel"),
        name="mm_res",
    )(y, w, x)


def _expand_heads(v, first, width):
    rows = v.shape[0]
    return jnp.concatenate(
        [jnp.broadcast_to(v[:, first + j:first + j + 1], (rows, width)) for j in range(SSD_HPG)], axis=1)


def _ssd_chunk(q, xcur, xm1, xm2, xm3, z, dt_raw, cw_ref, cb_ref, dtb_ref, alog_ref, drep_ref, ng_ref,
               h_in_ref, h_out_ref, gz_ref):
    cw = cw_ref[...]
    xc = _silu(cb_ref[...] + cw[0:1] * xm3 + cw[1:2] * xm2 + cw[2:3] * xm1 + cw[3:4] * xcur)
    xs = xc[:, :D_INNER]
    bm = xc[:, D_INNER:D_INNER + SSD_GROUPS * D_STATE]
    cm = xc[:, D_INNER + SSD_GROUPS * D_STATE:]

    dt = _softplus(dt_raw + dtb_ref[...])
    a = -jnp.exp(alog_ref[...])
    row = lax.broadcasted_iota(jnp.int32, (q, q), 0)
    col = lax.broadcasted_iota(jnp.int32, (q, q), 1)
    causal = row >= col
    a_cum = jnp.dot(causal.astype(F32), dt * a, precision=lax.Precision.HIGHEST, preferred_element_type=F32)
    a_last = a_cum[q - 1:q, :]
    w_state = jnp.exp(a_last - a_cum) * dt
    e_cum = jnp.exp(a_cum)
    e_last = jnp.exp(a_last)
    a_cum_t = a_cum.T
    dt_t = dt.T

    for g in range(SSD_GROUPS):
        h0 = g * SSD_HPG
        c0 = g * GROUP_W
        bmb = bm[:, g * D_STATE:(g + 1) * D_STATE].astype(BF16)
        cmb = cm[:, g * D_STATE:(g + 1) * D_STATE].astype(BF16)
        cb = lax.dot_general(cmb, bmb, NT_DIMS, preferred_element_type=F32)
        x_g = xs[:, c0:c0 + GROUP_W]
        y_heads = []
        for j in range(SSD_HPG):
            h = h0 + j
            seg = a_cum[:, h:h + 1] - a_cum_t[h:h + 1, :]
            decay = jnp.exp(jnp.where(causal, seg, -jnp.inf))
            wts = (cb * decay * dt_t[h:h + 1, :]).astype(BF16)
            x_h = x_g[:, j * SSD_HEADDIM:(j + 1) * SSD_HEADDIM].astype(BF16)
            y_heads.append(jnp.dot(wts, x_h, preferred_element_type=F32))
        y_diag = jnp.concatenate(y_heads, axis=1)

        h_prev = h_in_ref[c0:c0 + GROUP_W, :]
        y_off = lax.dot_general(cmb, h_prev.astype(BF16), NT_DIMS, preferred_element_type=F32)
        y_off = y_off * _expand_heads(e_cum, h0, SSD_HEADDIM)
        xw = (x_g * _expand_heads(w_state, h0, SSD_HEADDIM)).astype(BF16)
        states = lax.dot_general(xw, bmb, TN_DIMS, preferred_element_type=F32)
        carry = jnp.concatenate(
            [jnp.broadcast_to(e_last[:, h0 + j:h0 + j + 1], (SSD_HEADDIM, D_STATE)) for j in range(SSD_HPG)], axis=0)
        h_out_ref[c0:c0 + GROUP_W, :] = carry * h_prev + states

        y = y_diag + y_off + drep_ref[:, c0:c0 + GROUP_W] * x_g
        gz = y * _silu(z[:, c0:c0 + GROUP_W])
        gz = gz * lax.rsqrt(jnp.mean(gz * gz, axis=-1, keepdims=True) + EPS) * ng_ref[:, c0:c0 + GROUP_W]
        gz_ref[:, c0:c0 + GROUP_W] = gz.astype(gz_ref.dtype)


PAD_ROWS = 8


def _ssd_body(z_ref, xbc_ref, dt_ref, cs_ref, h0_ref, cw_ref, cb_ref, dtb_ref, alog_ref, drep_ref, ng_ref,
              gz_ref, nc_ref, hout_ref, xpad_ref, *, q, single_chunk):
    hist = CONV_W - 1
    if single_chunk:
        xpad_ref[PAD_ROWS - hist:PAD_ROWS, :] = cs_ref[0]
        h_in = h0_ref.at[0]
    else:
        @pl.when(pl.program_id(1) == 0)
        def _():
            xpad_ref[PAD_ROWS - hist:PAD_ROWS, :] = cs_ref[0]
            hout_ref[0] = h0_ref[0]
        h_in = hout_ref.at[0]
    xcur = xbc_ref[...]
    xpad_ref[PAD_ROWS:PAD_ROWS + q, :] = xcur
    xm1 = xpad_ref[PAD_ROWS - 1:PAD_ROWS - 1 + q, :]
    xm2 = xpad_ref[PAD_ROWS - 2:PAD_ROWS - 2 + q, :]
    xm3 = xpad_ref[PAD_ROWS - 3:PAD_ROWS - 3 + q, :]
    _ssd_chunk(q, xcur, xm1, xm2, xm3, z_ref[...], dt_ref[...], cw_ref, cb_ref, dtb_ref, alog_ref, drep_ref,
               ng_ref, h_in, hout_ref.at[0], gz_ref)
    tail = xpad_ref[PAD_ROWS + q - hist:PAD_ROWS + q, :]
    nc_ref[0] = tail
    if not single_chunk:
        xpad_ref[PAD_ROWS - hist:PAD_ROWS, :] = tail


def _ssd(z, xbc, dt_raw, conv_state, ssm_state, prm, batch, seq, gz_dtype):
    q = SSD_CHUNK if seq % SSD_CHUNK == 0 else seq
    nc = seq // q
    hist = CONV_W - 1
    hp = SSD_HEADS * SSD_HEADDIM
    row = lambda b, c: (b * nc + c, 0)
    per_b = lambda b, c: (b, 0, 0)
    cw, cb, dtb, alog, drep, ng = prm
    return pl.pallas_call(
        functools.partial(_ssd_body, q=q, single_chunk=nc == 1),
        out_shape=[jax.ShapeDtypeStruct((batch * seq, D_INNER), gz_dtype),
                   jax.ShapeDtypeStruct((batch, hist, CONV_DIM), F32),
                   jax.ShapeDtypeStruct((batch, hp, D_STATE), F32)],
        grid=(batch, nc),
        in_specs=[pl.BlockSpec((q, D_INNER), row), pl.BlockSpec((q, CONV_DIM), row), pl.BlockSpec((q, LANES), row),
                  pl.BlockSpec((1, hist, CONV_DIM), per_b), pl.BlockSpec((1, hp, D_STATE), per_b),
                  _const_spec(cw.shape), _const_spec(cb.shape), _const_spec(dtb.shape), _const_spec(alog.shape),
                  _const_spec(drep.shape), _const_spec(ng.shape)],
        out_specs=[pl.BlockSpec((q, D_INNER), row), pl.BlockSpec((1, hist, CONV_DIM), per_b),
                   pl.BlockSpec((1, hp, D_STATE), per_b)],
        scratch_shapes=[pltpu.VMEM((PAD_ROWS + q, CONV_DIM), F32)],
        compiler_params=_params("parallel", "arbitrary"),
        name="ssd",
    )(z, xbc, dt_raw, conv_state, ssm_state, cw, cb, dtb, alog, drep, ng)


ATTN_SLAB = ATTN_QBLOCK * 16
HEAD_PAIRS = ATTN_HPG // 2
MAX_DIL = 16


def _attn_prompt_body(slope_ref, *refs):
    qkv_refs = refs[:9]
    o_ref = refs[9]
    m_sc, l_sc, acc_sc, carry_k, carry_v, k_buf, v_buf = refs[10:]
    pair = pl.program_id(1)
    s_idx = pl.program_id(2)
    qb = ATTN_QBLOCK

    @pl.when(s_idx == 0)
    def _():
        carry_k[...] = jnp.zeros(carry_k.shape, BF16)
        carry_v[...] = jnp.zeros(carry_v.shape, BF16)

    row = lax.broadcasted_iota(jnp.int32, (qb, 2 * qb), 0)
    col = lax.broadcasted_iota(jnp.int32, (qb, 2 * qb), 1)
    dist = qb + row - col
    in_window = (dist >= 0) & (dist <= ATTN_NKEYS)
    dist_f = dist.astype(F32)
    lane = lax.broadcasted_iota(jnp.int32, (qb, LANES), 1)
    low_half = lane < ATTN_HEAD_DIM

    for g, (_, dil) in enumerate(ATTN_PATTERNS):
        q_ref, k_ref, v_ref = qkv_refs[3 * g:3 * g + 3]
        n_blk = ATTN_SLAB // (qb * dil)
        slopes = [slope_ref[g * ATTN_HPG + 2 * pair + e] * float(dil) for e in range(2)]

        def block(idx, carry, g=g, dil=dil, n_blk=n_blk, q_ref=q_ref, k_ref=k_ref, v_ref=v_ref, slopes=slopes):
            r = idx // n_blk
            blk = idx % n_blk
            if dil == 1:
                rows = pl.ds(pl.multiple_of(qb * blk, qb), qb)
            else:
                rows = pl.ds(r + (dil * qb) * blk, qb, stride=dil)

            @pl.when(blk == 0)
            def _():
                k_buf[0:qb, :] = carry_k[g, r]
                v_buf[0:qb, :] = carry_v[g, r]

            k_buf[qb:2 * qb, :] = k_ref[rows, :].astype(BF16)
            v_buf[qb:2 * qb, :] = v_ref[rows, :].astype(BF16)
            q2 = q_ref[rows, :] * (ATTN_HEAD_DIM ** -0.5)
            first_col = jnp.where((blk > 0) | (s_idx > 0), 0, qb)
            valid = in_window & (col >= first_col)
            k2 = k_buf[...]
            v2 = v_buf[...]
            ms, ls, pvs = [], [], []
            for e in range(2):
                qm = jnp.where(low_half if e == 0 else ~low_half, q2, 0.0).astype(BF16)
                s = lax.dot_general(qm, k2, NT_DIMS, preferred_element_type=F32) - slopes[e] * dist_f
                s = jnp.where(valid, s, NEG_INF)
                m = jnp.max(s, axis=-1, keepdims=True)
                p = jnp.exp(s - m)
                ms.append(m)
                ls.append(jnp.sum(p, axis=-1, keepdims=True))
                pvs.append(jnp.dot(p.astype(BF16), v2, preferred_element_type=F32))
            m_new = jnp.where(low_half, ms[0], ms[1])
            l_new = jnp.where(low_half, ls[0], ls[1])
            pv_new = jnp.where(low_half, pvs[0], pvs[1])
            if g > 0:
                m_old = m_sc[rows, :]
                m_tot = jnp.maximum(m_old, m_new)
                a_old = jnp.exp(m_old - m_tot)
                a_new = jnp.exp(m_new - m_tot)
                l_new = a_old * l_sc[rows, :] + a_new * l_new
                pv_new = a_old * acc_sc[rows, :] + a_new * pv_new
                m_new = m_tot
            if g == N_ATTN_GROUPS - 1:
                o_ref[rows, :] = pv_new / l_new
            else:
                m_sc[rows, :] = m_new
                l_sc[rows, :] = l_new
                acc_sc[rows, :] = pv_new

            @pl.when(blk == n_blk - 1)
            def _():
                carry_k[g, r] = k_buf[qb:2 * qb, :]
                carry_v[g, r] = v_buf[qb:2 * qb, :]

            @pl.when(blk < n_blk - 1)
            def _():
                k_buf[0:qb, :] = k_buf[qb:2 * qb, :]
                v_buf[0:qb, :] = v_buf[qb:2 * qb, :]

            return carry

        lax.fori_loop(0, ATTN_SLAB // qb, block, 0)


def _attn_prompt(qkv, slope_tab, batch, seq):
    n_slab = seq // ATTN_SLAB
    col_blocks = ATTN_WIDTH // LANES

    def spec(g, which):
        base = (3 * g + which) * col_blocks
        return pl.BlockSpec((ATTN_SLAB, LANES), lambda b, p, s: (b * n_slab + s, base + p))

    in_specs = [pl.BlockSpec(memory_space=pltpu.SMEM)]
    in_specs += [spec(g, which) for g in range(N_ATTN_GROUPS) for which in range(3)]
    return pl.pallas_call(
        _attn_prompt_body,
        out_shape=jax.ShapeDtypeStruct((batch * seq, ATTN_WIDTH), F32),
        grid=(batch, HEAD_PAIRS, n_slab),
        in_specs=in_specs,
        out_specs=pl.BlockSpec((ATTN_SLAB, LANES), lambda b, p, s: (b * n_slab + s, p)),
        scratch_shapes=[pltpu.VMEM((ATTN_SLAB, LANES), F32), pltpu.VMEM((ATTN_SLAB, LANES), F32),
                        pltpu.VMEM((ATTN_SLAB, LANES), F32),
                        pltpu.VMEM((N_ATTN_GROUPS, MAX_DIL, ATTN_QBLOCK, LANES), BF16),
                        pltpu.VMEM((N_ATTN_GROUPS, MAX_DIL, ATTN_QBLOCK, LANES), BF16),
                        pltpu.VMEM((2 * ATTN_QBLOCK, LANES), BF16), pltpu.VMEM((2 * ATTN_QBLOCK, LANES), BF16)],
        compiler_params=_params("parallel", "parallel", "arbitrary"),
        name="attn_prompt",
    )(slope_tab, *([qkv] * 9))


def _kv_tail_body(k_ref, v_ref, o_ref):
    o_ref[0, :ATTN_WIDTH, :] = k_ref[...].T
    o_ref[0, ATTN_WIDTH:, :] = v_ref[...].T


def _kv_tail_t(qkv, g, batch, seq, tm=128):
    win = min(ATTN_PATTERNS[g][0], seq)
    first = (seq - win) // tm
    per_b = seq // tm

    def spec(which):
        return pl.BlockSpec((tm, ATTN_WIDTH), lambda b, j: (b * per_b + first + j, 3 * g + which))

    return pl.pallas_call(
        _kv_tail_body,
        out_shape=jax.ShapeDtypeStruct((batch, 2 * ATTN_WIDTH, win), F32),
        grid=(batch, win // tm),
        in_specs=[spec(1), spec(2)],
        out_specs=pl.BlockSpec((1, 2 * ATTN_WIDTH, tm), lambda b, j: (b, 0, j)),
        compiler_params=_params("parallel", "parallel"),
        name=f"kv_tail_g{g}",
    )(qkv, qkv)


def _attn_sample_body(slope_ref, qkv_ref, c0_ref, c1_ref, c2_ref, o_ref, *, n_new):
    qkv = qkv_ref[...]
    caches = (c0_ref, c1_ref, c2_ref)
    gap_n = (lax.broadcasted_iota(jnp.int32, (n_new, n_new), 0)
             - lax.broadcasted_iota(jnp.int32, (n_new, n_new), 1))
    masks = []
    for c_ref, (win, dil) in zip(caches, ATTN_PATTERNS):
        past_len = c_ref.shape[2]
        gap_p = (past_len + lax.broadcasted_iota(jnp.int32, (n_new, past_len), 0)
                 - lax.broadcasted_iota(jnp.int32, (n_new, past_len), 1))
        ok_p = (gap_p <= win) & ((gap_p & (dil - 1)) == 0)
        ok_n = (gap_n >= 0) & ((gap_n & (dil - 1)) == 0)
        masks.append((ok_p, gap_p.astype(F32), ok_n, gap_n.astype(F32)))
    heads = []
    for h in range(ATTN_HPG):
        hs = slice(h * ATTN_HEAD_DIM, (h + 1) * ATTN_HEAD_DIM)
        outs, lses = [], []
        for g, c_ref in enumerate(caches):
            base = g * 3 * ATTN_WIDTH
            ok_p, dist_p, ok_n, dist_n = masks[g]
            slope = slope_ref[g * ATTN_HPG + h]
            q_h = (qkv[:, base:base + ATTN_WIDTH][:, hs] * (ATTN_HEAD_DIM ** -0.5)).astype(BF16)
            k_new = qkv[:, base + ATTN_WIDTH:base + 2 * ATTN_WIDTH][:, hs].astype(BF16)
            v_new = qkv[:, base + 2 * ATTN_WIDTH:base + 3 * ATTN_WIDTH][:, hs].astype(BF16)
            k_t = c_ref[0, h * ATTN_HEAD_DIM:(h + 1) * ATTN_HEAD_DIM, :].astype(BF16)
            v_t = c_ref[0, ATTN_WIDTH + h * ATTN_HEAD_DIM:ATTN_WIDTH + (h + 1) * ATTN_HEAD_DIM, :].astype(BF16)
            s_p = jnp.dot(q_h, k_t, preferred_element_type=F32) - slope * dist_p
            s_p = jnp.where(ok_p, s_p, NEG_INF)
            s_n = lax.dot_general(q_h, k_new, NT_DIMS, preferred_element_type=F32) - slope * dist_n
            s_n = jnp.where(ok_n, s_n, NEG_INF)
            mx = jnp.maximum(jnp.max(s_p, axis=-1, keepdims=True), jnp.max(s_n, axis=-1, keepdims=True))
            e_p = jnp.exp(s_p - mx)
            e_n = jnp.exp(s_n - mx)
            l = jnp.sum(e_p, axis=-1, keepdims=True) + jnp.sum(e_n, axis=-1, keepdims=True)
            o = (lax.dot_general(e_p.astype(BF16), v_t, NT_DIMS, preferred_element_type=F32)
                 + jnp.dot(e_n.astype(BF16), v_new, preferred_element_type=F32)) / l
            outs.append(o)
            lses.append(mx + jnp.log(l))
        top = jnp.maximum(jnp.maximum(lses[0], lses[1]), lses[2])
        es = [jnp.exp(l - top) for l in lses]
        heads.append((es[0] * outs[0] + es[1] * outs[1] + es[2] * outs[2]) / (es[0] + es[1] + es[2]))
    o_ref[...] = jnp.concatenate(heads, axis=1)


def _attn_sample(qkv, caches_t, slope_tab, batch, n_new):
    specs = [pl.BlockSpec((1,) + c.shape[1:], lambda b: (b, 0, 0)) for c in caches_t]
    return pl.pallas_call(
        functools.partial(_attn_sample_body, n_new=n_new),
        out_shape=jax.ShapeDtypeStruct((batch * n_new, ATTN_WIDTH), F32),
        grid=(batch,),
        in_specs=[pl.BlockSpec(memory_space=pltpu.SMEM), pl.BlockSpec((n_new, QKV_WIDTH), lambda b: (b, 0))] + specs,
        out_specs=pl.BlockSpec((n_new, ATTN_WIDTH), lambda b: (b, 0)),
        compiler_params=_params("parallel"),
        name="attn_sample",
    )(slope_tab, qkv, *caches_t)


def _alibi_slopes():
    n_heads = N_ATTN_GROUPS * ATTN_HPG
    return [2.0 ** (-8.0 * (h + 1) / n_heads) for h in range(n_heads)]


def kernel(x_prompt, x_sample, state_conv, state_ssm, cache_kv_g0, cache_kv_g1, cache_kv_g2, norm_w, w_ffn_in,
           w_ffn_out, ssm_w_in, ssm_conv_w, ssm_conv_b, ssm_dt_bias, ssm_a_log, ssm_d, ssm_norm_w, ssm_w_out,
           attn_w_qkv, attn_w_o, norm_f):
    bp, lp, d = x_prompt.shape
    bs, ls, _ = x_sample.shape
    xs_all = [x_prompt.reshape(bp * lp, d), x_sample.reshape(bs * ls, d)]
    dims = [(bp, lp), (bs, ls)]
    hist = CONV_W - 1
    hp = SSD_HEADS * SSD_HEADDIM

    w_in = w_ffn_in.astype(BF16)
    w_out = w_ffn_out.astype(BF16)
    pad = LANES - SSD_HEADS
    w_proj = jnp.pad(ssm_w_in[0], ((0, 0), (0, pad))).astype(BF16)
    ssd_prm = (ssm_conv_w[0], ssm_conv_b[0].reshape(1, CONV_DIM),
               jnp.pad(ssm_dt_bias[0], (0, pad)).reshape(1, LANES),
               jnp.pad(ssm_a_log[0], (0, pad)).reshape(1, LANES),
               jnp.repeat(ssm_d[0], SSD_HEADDIM).reshape(1, D_INNER),
               ssm_norm_w[0].reshape(1, D_INNER))
    w_ssm_out = ssm_w_out[0].astype(BF16)
    w_qkv = attn_w_qkv[0].astype(BF16)
    w_o = attn_w_o[0].astype(BF16)
    slope_tab = jnp.asarray(_alibi_slopes(), F32)

    conv_states = [jnp.zeros((bp, hist, CONV_DIM), F32), state_conv[0]]
    ssm_states = [jnp.zeros((bp, hp, D_STATE), F32), state_ssm[0].reshape(bs, hp, D_STATE)]
    caches_t = [jnp.transpose(c[0], (0, 2, 3, 4, 1)).reshape(bs, 2 * ATTN_WIDTH, c.shape[2])
                for c in (cache_kv_g0, cache_kv_g1, cache_kv_g2)]

    conv_out, ssm_out = [], []
    for n, (x, (b, l)) in enumerate(zip(xs_all, dims)):
        x = _ffn(x, norm_w[0, 0], w_in[0, 0], w_out[0, 0])
        z, xbc, dt_raw = _norm_mm(x, norm_w[0, 1], w_proj, (D_INNER, CONV_DIM, LANES))
        gz, new_conv, new_ssm = _ssd(z, xbc, dt_raw, conv_states[n], ssm_states[n], ssd_prm, b, l,
                                     BF16 if l % SSD_CHUNK == 0 else F32)
        conv_out.append(new_conv[None])
        ssm_out.append(new_ssm.reshape(1, b, SSD_HEADS, SSD_HEADDIM, D_STATE))
        x = _mm_res(gz, w_ssm_out, x)
        xs_all[n] = _ffn(x, norm_w[0, 2], w_in[0, 1], w_out[0, 1])

    kv_out = []
    for n, (x, (b, l)) in enumerate(zip(xs_all, dims)):
        x = _ffn(x, norm_w[1, 0], w_in[1, 0], w_out[1, 0])
        (qkv,) = _norm_mm(x, norm_w[1, 1], w_qkv, (QKV_WIDTH,))
        if n == 0:
            o = _attn_prompt(qkv, slope_tab, b, l)
            kv_t = [_kv_tail_t(qkv, g, b, l) for g in range(N_ATTN_GROUPS)]
            kv_out.append([jnp.transpose(t.reshape(b, 2, ATTN_HPG, ATTN_HEAD_DIM, t.shape[2]), (0, 4, 1, 2, 3))[None]
                           for t in kv_t])
        else:
            o = _attn_sample(qkv, caches_t, slope_tab, b, l)
            qkv5 = qkv.reshape(b, l, N_ATTN_GROUPS, 3, ATTN_HPG, ATTN_HEAD_DIM)
            kv_out.append([qkv5[:, :, g, 1:][None] for g in range(N_ATTN_GROUPS)])
        x = _mm_res(o, w_o, x)
        xs_all[n] = _ffn(x, norm_w[1, 2], w_in[1, 1], w_out[1, 1], g_final=norm_f)

    return (xs_all[0].reshape(bp, lp, d), xs_all[1].reshape(bs, ls, d),
            conv_out[0], conv_out[1], ssm_out[0], ssm_out[1],
            kv_out[0][0], kv_out[1][0], kv_out[0][1], kv_out[1][1], kv_out[0][2], kv_out[1][2])
```

```python
import functools

import jax
import jax.numpy as jnp
from jax import lax
from jax.experimental import pallas as pl
from jax.experimental.pallas import tpu as pltpu

F32 = jnp.float32
BF16 = jnp.bfloat16

EPS = 1e-6
NEG_INF = -1e30

D_MODEL = 1024
D_FF = 2816
D_INNER = 2048
SSD_HEADS = 32
SSD_HEADDIM = 64
SSD_GROUPS = 8
SSD_HPG = SSD_HEADS // SSD_GROUPS
GROUP_W = SSD_HPG * SSD_HEADDIM
D_STATE = 128
CONV_W = 4
CONV_DIM = D_INNER + 2 * SSD_GROUPS * D_STATE
SSD_CHUNK = 128
ATTN_PATTERNS = ((128, 1), (512, 4), (2048, 16))
N_ATTN_GROUPS = 3
ATTN_HPG = 8
ATTN_HEAD_DIM = 64
ATTN_WIDTH = ATTN_HPG * ATTN_HEAD_DIM
QKV_WIDTH = N_ATTN_GROUPS * 3 * ATTN_WIDTH
ATTN_QBLOCK = 128
ATTN_NKEYS = 128

LANES = 128
VMEM_LIMIT = 56 * 1024 * 1024

NT_DIMS = (((1,), (1,)), ((), ()))
TN_DIMS = (((0,), (0,)), ((), ()))


def _const_spec(shape):
    zeros = (0,) * len(shape)
    return pl.BlockSpec(shape, lambda *_: zeros, pipeline_mode=pl.Buffered(1))


def _params(*semantics):
    return pltpu.CompilerParams(dimension_semantics=semantics, vmem_limit_bytes=VMEM_LIMIT)


def _rms(x, g):
    return x * lax.rsqrt(jnp.mean(x * x, axis=-1, keepdims=True) + EPS) * g


def _silu(x):
    return x * jax.nn.sigmoid(x)


def _softplus(x):
    return jnp.maximum(x, 0.0) + jnp.log1p(jnp.exp(-jnp.abs(x)))


def _ffn_body(x_ref, g_ref, win_ref, wout_ref, *rest, final_norm):
    if final_norm:
        gf_ref, o_ref = rest
    else:
        (o_ref,) = rest
    x = x_ref[...]
    h = _rms(x, g_ref[...]).astype(BF16)
    a = jnp.dot(h, win_ref[:, :D_FF], preferred_element_type=F32)
    b = jnp.dot(h, win_ref[:, D_FF:], preferred_element_type=F32)
    t = (_silu(a) * b).astype(BF16)
    y = x + 0.5 * jnp.dot(t, wout_ref[...], preferred_element_type=F32)
    if final_norm:
        y = _rms(y, gf_ref[...])
    o_ref[...] = y


def _ffn(x, g, w_in, w_out, g_final=None, tm=512):
    m, d = x.shape
    tm = min(tm, m)
    row = pl.BlockSpec((tm, d), lambda i: (i, 0))
    in_specs = [row, _const_spec((1, d)), _const_spec(w_in.shape), _const_spec(w_out.shape)]
    args = [x, g.reshape(1, d), w_in, w_out]
    if g_final is not None:
        in_specs.append(_const_spec((1, d)))
        args.append(g_final.reshape(1, d))
    return pl.pallas_call(
        functools.partial(_ffn_body, final_norm=g_final is not None),
        out_shape=jax.ShapeDtypeStruct((m, d), F32),
        grid=(m // tm,),
        in_specs=in_specs,
        out_specs=row,
        compiler_params=_params("parallel"),
        name="ffn",
    )(*args)


def _norm_mm_body(x_ref, g_ref, w_ref, *o_refs, splits):
    h = _rms(x_ref[...], g_ref[...]).astype(BF16)
    off = 0
    for o_ref, n in zip(o_refs, splits):
        o_ref[...] = jnp.dot(h, w_ref[:, off:off + n], preferred_element_type=F32)
        off += n


def _norm_mm(x, g, w, splits, tm=256):
    m, d = x.shape
    tm = min(tm, m)
    return pl.pallas_call(
        functools.partial(_norm_mm_body, splits=splits),
        out_shape=[jax.ShapeDtypeStruct((m, n), F32) for n in splits],
        grid=(m // tm,),
        in_specs=[pl.BlockSpec((tm, d), lambda i: (i, 0)), _const_spec((1, d)), _const_spec(w.shape)],
        out_specs=[pl.BlockSpec((tm, n), lambda i: (i, 0)) for n in splits],
        compiler_params=_params("parallel"),
        name="norm_mm",
    )(x, g.reshape(1, d), w)


def _mm_res_body(y_ref, w_ref, x_ref, o_ref):
    o_ref[...] = x_ref[...] + jnp.dot(y_ref[...].astype(BF16), w_ref[...], preferred_element_type=F32)


def _mm_res(y, w, x, tm=512):
    m, k = y.shape
    d = x.shape[1]
    tm = min(tm, m)
    return pl.pallas_call(
        _mm_res_body,
        out_shape=jax.ShapeDtypeStruct((m, d), F32),
        grid=(m // tm,),
        in_specs=[pl.BlockSpec((tm, k), lambda i: (i, 0)), _const_spec(w.shape),
                  pl.BlockSpec((tm, d), lambda i: (i, 0))],
        out_specs=pl.BlockSpec((tm, d), lambda i: (i, 0)),
        compiler_params=_params("parallel"),
        name="mm_res",
    )(y, w, x)


def _expand_heads(v, first, width):
    rows = v.shape[0]
    return jnp.concatenate(
        [jnp.broadcast_to(v[:, first + j:first + j + 1], (rows, width)) for j in range(SSD_HPG)], axis=1)


def _ssd_chunk(q, xpad_ref, z, dt_raw, cw_ref, cb_ref, dtb_ref, alog_ref, drep_ref, ng_ref,
               h_in_ref, h_out_ref, gz_ref):
    def conv(slab):
        sl = slice(slab * LANES, (slab + 1) * LANES)
        acc = cb_ref[:, sl]
        for k in range(CONV_W):
            lo = PAD_ROWS - (CONV_W - 1) + k
            acc = acc + cw_ref[k:k + 1, sl] * xpad_ref[slab, lo:lo + q, :]
        return _silu(acc)

    x_slabs = D_INNER // LANES
    bc_slabs = SSD_GROUPS * D_STATE // LANES

    dt = _softplus(dt_raw + dtb_ref[...])
    a = -jnp.exp(alog_ref[...])
    row = lax.broadcasted_iota(jnp.int32, (q, q), 0)
    col = lax.broadcasted_iota(jnp.int32, (q, q), 1)
    causal = row >= col
    a_cum = jnp.dot(causal.astype(F32), dt * a, precision=lax.Precision.HIGHEST, preferred_element_type=F32)
    a_last = a_cum[q - 1:q, :]
    w_state = jnp.exp(a_last - a_cum) * dt
    e_cum = jnp.exp(a_cum)
    e_last = jnp.exp(a_last)
    a_cum_t = a_cum.T
    dt_t = dt.T

    for g in range(SSD_GROUPS):
        h0 = g * SSD_HPG
        c0 = g * GROUP_W
        bmb = conv(x_slabs + g).astype(BF16)
        cmb = conv(x_slabs + bc_slabs + g).astype(BF16)
        cb = lax.dot_general(cmb, bmb, NT_DIMS, preferred_element_type=F32)
        x_g = jnp.concatenate([conv(c0 // LANES + i) for i in range(GROUP_W // LANES)], axis=1)
        y_heads = []
        for j in range(SSD_HPG):
            h = h0 + j
            seg = a_cum[:, h:h + 1] - a_cum_t[h:h + 1, :]
            decay = jnp.exp(jnp.where(causal, seg, -jnp.inf))
            wts = (cb * decay * dt_t[h:h + 1, :]).astype(BF16)
            x_h = x_g[:, j * SSD_HEADDIM:(j + 1) * SSD_HEADDIM].astype(BF16)
            y_heads.append(jnp.dot(wts, x_h, preferred_element_type=F32))
        y_diag = jnp.concatenate(y_heads, axis=1)

        h_prev = h_in_ref[c0:c0 + GROUP_W, :]
        y_off = lax.dot_general(cmb, h_prev.astype(BF16), NT_DIMS, preferred_element_type=F32)
        y_off = y_off * _expand_heads(e_cum, h0, SSD_HEADDIM)
        xw = (x_g * _expand_heads(w_state, h0, SSD_HEADDIM)).astype(BF16)
        states = lax.dot_general(xw, bmb, TN_DIMS, preferred_element_type=F32)
        carry = jnp.concatenate(
            [jnp.broadcast_to(e_last[:, h0 + j:h0 + j + 1], (SSD_HEADDIM, D_STATE)) for j in range(SSD_HPG)], axis=0)
        h_out_ref[c0:c0 + GROUP_W, :] = carry * h_prev + states

        y = y_diag + y_off + drep_ref[:, c0:c0 + GROUP_W] * x_g
        gz = y * _silu(z[:, c0:c0 + GROUP_W])
        gz = gz * lax.rsqrt(jnp.mean(gz * gz, axis=-1, keepdims=True) + EPS) * ng_ref[:, c0:c0 + GROUP_W]
        gz_ref[:, c0:c0 + GROUP_W] = gz.astype(gz_ref.dtype)


PAD_ROWS = 8


def _ssd_body(z_ref, xbc_ref, dt_ref, cs_ref, h0_ref, cw_ref, cb_ref, dtb_ref, alog_ref, drep_ref, ng_ref,
              gz_ref, nc_ref, hout_ref, xpad_ref, *, q, single_chunk):
    hist = CONV_W - 1
    n_slabs = CONV_DIM // LANES

    def load_history():
        for j in range(n_slabs):
            xpad_ref[j, PAD_ROWS - hist:PAD_ROWS, :] = cs_ref[0, :, j * LANES:(j + 1) * LANES]

    if single_chunk:
        load_history()
        h_in = h0_ref.at[0]
    else:
        @pl.when(pl.program_id(1) == 0)
        def _():
            load_history()
            hout_ref[0] = h0_ref[0]
        h_in = hout_ref.at[0]
    for j in range(n_slabs):
        xpad_ref[j, PAD_ROWS:PAD_ROWS + q, :] = xbc_ref[:, j * LANES:(j + 1) * LANES]
    _ssd_chunk(q, xpad_ref, z_ref[...], dt_ref[...], cw_ref, cb_ref, dtb_ref, alog_ref, drep_ref,
               ng_ref, h_in, hout_ref.at[0], gz_ref)
    for j in range(n_slabs):
        tail = xpad_ref[j, PAD_ROWS + q - hist:PAD_ROWS + q, :]
        nc_ref[0, :, j * LANES:(j + 1) * LANES] = tail
        if not single_chunk:
            xpad_ref[j, PAD_ROWS - hist:PAD_ROWS, :] = tail


def _ssd(z, xbc, dt_raw, conv_state, ssm_state, prm, batch, seq, gz_dtype):
    q = SSD_CHUNK if seq % SSD_CHUNK == 0 else seq
    nc = seq // q
    hist = CONV_W - 1
    hp = SSD_HEADS * SSD_HEADDIM
    row = lambda b, c: (b * nc + c, 0)
    per_b = lambda b, c: (b, 0, 0)
    cw, cb, dtb, alog, drep, ng = prm
    return pl.pallas_call(
        functools.partial(_ssd_body, q=q, single_chunk=nc == 1),
        out_shape=[jax.ShapeDtypeStruct((batch * seq, D_INNER), gz_dtype),
                   jax.ShapeDtypeStruct((batch, hist, CONV_DIM), F32),
                   jax.ShapeDtypeStruct((batch, hp, D_STATE), F32)],
        grid=(batch, nc),
        in_specs=[pl.BlockSpec((q, D_INNER), row), pl.BlockSpec((q, CONV_DIM), row), pl.BlockSpec((q, LANES), row),
                  pl.BlockSpec((1, hist, CONV_DIM), per_b), pl.BlockSpec((1, hp, D_STATE), per_b),
                  _const_spec(cw.shape), _const_spec(cb.shape), _const_spec(dtb.shape), _const_spec(alog.shape),
                  _const_spec(drep.shape), _const_spec(ng.shape)],
        out_specs=[pl.BlockSpec((q, D_INNER), row), pl.BlockSpec((1, hist, CONV_DIM), per_b),
                   pl.BlockSpec((1, hp, D_STATE), per_b)],
        scratch_shapes=[pltpu.VMEM((CONV_DIM // LANES, PAD_ROWS + q, LANES), F32)],
        compiler_params=_params("parallel", "arbitrary"),
        name="ssd",
    )(z, xbc, dt_raw, conv_state, ssm_state, cw, cb, dtb, alog, drep, ng)


ATTN_SLAB = ATTN_QBLOCK * 16
HEAD_PAIRS = ATTN_HPG // 2


def _attn_prompt_body(slope_ref, *refs):
    qkv_refs = refs[:9]
    o_ref = refs[9]
    m_sc, l_sc, acc_sc = refs[10:13]
    kv_ext = refs[13:]
    pair = pl.program_id(1)
    s_idx = pl.program_id(2)
    qb = ATTN_QBLOCK

    for g, (_, dil) in enumerate(ATTN_PATTERNS):
        hist = qb * dil
        for ext, src in zip(kv_ext[2 * g:2 * g + 2], qkv_refs[3 * g + 1:3 * g + 3]):
            @pl.when(s_idx == 0)
            def _(ext=ext, hist=hist):
                ext[0:hist, :] = jnp.zeros((hist, LANES), F32)

            ext[hist:hist + ATTN_SLAB, :] = src[...]

    row = lax.broadcasted_iota(jnp.int32, (qb, 2 * qb), 0)
    col = lax.broadcasted_iota(jnp.int32, (qb, 2 * qb), 1)
    dist = qb + row - col
    in_window = (dist >= 0) & (dist <= ATTN_NKEYS)
    dist_f = dist.astype(F32)
    lane = lax.broadcasted_iota(jnp.int32, (qb, LANES), 1)
    low_half = lane < ATTN_HEAD_DIM

    for g, (_, dil) in enumerate(ATTN_PATTERNS):
        q_ref = qkv_refs[3 * g]
        k_ext, v_ext = kv_ext[2 * g:2 * g + 2]
        n_blk = ATTN_SLAB // (qb * dil)
        slopes = [slope_ref[g * ATTN_HPG + 2 * pair + e] * float(dil) for e in range(2)]

        def block(idx, carry, g=g, dil=dil, n_blk=n_blk, q_ref=q_ref, k_ext=k_ext, v_ext=v_ext, slopes=slopes):
            r = idx // n_blk
            blk = idx % n_blk
            if dil == 1:
                start = pl.multiple_of(qb * blk, qb)
                rows = pl.ds(start, qb)
                key_rows = pl.ds(start, 2 * qb)
            else:
                start = r + (dil * qb) * blk
                rows = pl.ds(start, qb, stride=dil)
                key_rows = pl.ds(start, 2 * qb, stride=dil)
            k2 = k_ext[key_rows, :].astype(BF16)
            v2 = v_ext[key_rows, :].astype(BF16)
            q2 = q_ref[rows, :] * (ATTN_HEAD_DIM ** -0.5)
            first_col = jnp.where((blk > 0) | (s_idx > 0), 0, qb)
            valid = in_window & (col >= first_col)
            ms, ls, pvs = [], [], []
            for e in range(2):
                qm = jnp.where(low_half if e == 0 else ~low_half, q2, 0.0).astype(BF16)
                s = lax.dot_general(qm, k2, NT_DIMS, preferred_element_type=F32) - slopes[e] * dist_f
                s = jnp.where(valid, s, NEG_INF)
                m = jnp.max(s, axis=-1, keepdims=True)
                p = jnp.exp(s - m)
                ms.append(m)
                ls.append(jnp.sum(p, axis=-1, keepdims=True))
                pvs.append(jnp.dot(p.astype(BF16), v2, preferred_element_type=F32))
            m_new = jnp.where(low_half, ms[0], ms[1])
            l_new = jnp.where(low_half, ls[0], ls[1])
            pv_new = jnp.where(low_half, pvs[0], pvs[1])
            if g > 0:
                m_old = m_sc[rows, :]
                m_tot = jnp.maximum(m_old, m_new)
                a_old = jnp.exp(m_old - m_tot)
                a_new = jnp.exp(m_new - m_tot)
                l_new = a_old * l_sc[rows, :] + a_new * l_new
                pv_new = a_old * acc_sc[rows, :] + a_new * pv_new
                m_new = m_tot
            if g == N_ATTN_GROUPS - 1:
                o_ref[rows, :] = pv_new / l_new
            else:
                m_sc[rows, :] = m_new
                l_sc[rows, :] = l_new
                acc_sc[rows, :] = pv_new
            return carry

        lax.fori_loop(0, ATTN_SLAB // qb, block, 0, unroll=2)

    for g, (_, dil) in enumerate(ATTN_PATTERNS):
        hist = qb * dil
        for ext in kv_ext[2 * g:2 * g + 2]:
            ext[0:hist, :] = ext[ATTN_SLAB:ATTN_SLAB + hist, :]


def _attn_prompt(qkv, slope_tab, batch, seq):
    n_slab = seq // ATTN_SLAB
    col_blocks = ATTN_WIDTH // LANES

    def spec(g, which):
        base = (3 * g + which) * col_blocks
        return pl.BlockSpec((ATTN_SLAB, LANES), lambda b, p, s: (b * n_slab + s, base + p))

    in_specs = [pl.BlockSpec(memory_space=pltpu.SMEM)]
    in_specs += [spec(g, which) for g in range(N_ATTN_GROUPS) for which in range(3)]
    return pl.pallas_call(
        _attn_prompt_body,
        out_shape=jax.ShapeDtypeStruct((batch * seq, ATTN_WIDTH), F32),
        grid=(batch, HEAD_PAIRS, n_slab),
        in_specs=in_specs,
        out_specs=pl.BlockSpec((ATTN_SLAB, LANES), lambda b, p, s: (b * n_slab + s, p)),
        scratch_shapes=[pltpu.VMEM((ATTN_SLAB, LANES), F32)] * 3 + [
            pltpu.VMEM((ATTN_QBLOCK * dil + ATTN_SLAB, LANES), F32) for _, dil in ATTN_PATTERNS for _ in range(2)],
        compiler_params=_params("parallel", "parallel", "arbitrary"),
        name="attn_prompt",
    )(slope_tab, *([qkv] * 9))


def _kv_tail_body(k_ref, v_ref, o_ref):
    o_ref[0, :ATTN_WIDTH, :] = k_ref[...].T
    o_ref[0, ATTN_WIDTH:, :] = v_ref[...].T


def _kv_tail_t(qkv, g, batch, seq, tm=128):
    win = min(ATTN_PATTERNS[g][0], seq)
    first = (seq - win) // tm
    per_b = seq // tm

    def spec(which):
        return pl.BlockSpec((tm, ATTN_WIDTH), lambda b, j: (b * per_b + first + j, 3 * g + which))

    return pl.pallas_call(
        _kv_tail_body,
        out_shape=jax.ShapeDtypeStruct((batch, 2 * ATTN_WIDTH, win), F32),
        grid=(batch, win // tm),
        in_specs=[spec(1), spec(2)],
        out_specs=pl.BlockSpec((1, 2 * ATTN_WIDTH, tm), lambda b, j: (b, 0, j)),
        compiler_params=_params("parallel", "parallel"),
        name=f"kv_tail_g{g}",
    )(qkv, qkv)


def _attn_sample_body(slope_ref, qkv_ref, c0_ref, c1_ref, c2_ref, o_ref, *, n_new):
    qkv = qkv_ref[...]
    caches = (c0_ref, c1_ref, c2_ref)
    rows = ATTN_HPG * n_new
    row_head = lax.broadcasted_iota(jnp.int32, (rows, ATTN_WIDTH), 0) // n_new
    lane_head = lax.broadcasted_iota(jnp.int32, (rows, ATTN_WIDTH), 1) // ATTN_HEAD_DIM
    diag = row_head == lane_head
    head_col = lax.broadcasted_iota(jnp.int32, (rows, 1), 0) // n_new
    qi_n = lax.broadcasted_iota(jnp.int32, (rows, n_new), 0) % n_new
    gap_n = qi_n - lax.broadcasted_iota(jnp.int32, (rows, n_new), 1)
    outs, lses = [], []
    for g, (c_ref, (win, dil)) in enumerate(zip(caches, ATTN_PATTERNS)):
        base = g * 3 * ATTN_WIDTH
        past_len = c_ref.shape[2]
        slope = jnp.zeros((rows, 1), F32)
        for h in range(ATTN_HPG):
            slope = jnp.where(head_col == h, slope_ref[g * ATTN_HPG + h], slope)
        gap_p = (past_len + lax.broadcasted_iota(jnp.int32, (rows, past_len), 0) % n_new
                 - lax.broadcasted_iota(jnp.int32, (rows, past_len), 1))
        ok_p = (gap_p <= win) & ((gap_p & (dil - 1)) == 0)
        ok_n = (gap_n >= 0) & ((gap_n & (dil - 1)) == 0)
        q = qkv[:, base:base + ATTN_WIDTH] * (ATTN_HEAD_DIM ** -0.5)
        q_bd = jnp.where(diag, jnp.concatenate([q] * ATTN_HPG, axis=0), 0.0).astype(BF16)
        k_new = qkv[:, base + ATTN_WIDTH:base + 2 * ATTN_WIDTH].astype(BF16)
        v_new = qkv[:, base + 2 * ATTN_WIDTH:base + 3 * ATTN_WIDTH].astype(BF16)
        k_t = c_ref[0, :ATTN_WIDTH, :].astype(BF16)
        v_t = c_ref[0, ATTN_WIDTH:, :].astype(BF16)
        s_p = jnp.dot(q_bd, k_t, preferred_element_type=F32) - slope * gap_p.astype(F32)
        s_p = jnp.where(ok_p, s_p, NEG_INF)
        s_n = lax.dot_general(q_bd, k_new, NT_DIMS, preferred_element_type=F32) - slope * gap_n.astype(F32)
        s_n = jnp.where(ok_n, s_n, NEG_INF)
        mx = jnp.maximum(jnp.max(s_p, axis=-1, keepdims=True), jnp.max(s_n, axis=-1, keepdims=True))
        e_p = jnp.exp(s_p - mx)
        e_n = jnp.exp(s_n - mx)
        l = jnp.sum(e_p, axis=-1, keepdims=True) + jnp.sum(e_n, axis=-1, keepdims=True)
        o = (lax.dot_general(e_p.astype(BF16), v_t, NT_DIMS, preferred_element_type=F32)
             + jnp.dot(e_n.astype(BF16), v_new, preferred_element_type=F32)) / l
        outs.append(o)
        lses.append(mx + jnp.log(l))
    top = jnp.maximum(jnp.maximum(lses[0], lses[1]), lses[2])
    es = [jnp.exp(l - top) for l in lses]
    merged = (es[0] * outs[0] + es[1] * outs[1] + es[2] * outs[2]) / (es[0] + es[1] + es[2])
    merged = jnp.where(diag, merged, 0.0).reshape(ATTN_HPG, n_new, ATTN_WIDTH)
    o_ref[...] = jnp.sum(merged, axis=0)


def _attn_sample(qkv, caches_t, slope_tab, batch, n_new):
    specs = [pl.BlockSpec((1,) + c.shape[1:], lambda b: (b, 0, 0)) for c in caches_t]
    return pl.pallas_call(
        functools.partial(_attn_sample_body, n_new=n_new),
        out_shape=jax.ShapeDtypeStruct((batch * n_new, ATTN_WIDTH), F32),
        grid=(batch,),
        in_specs=[pl.BlockSpec(memory_space=pltpu.SMEM), pl.BlockSpec((n_new, QKV_WIDTH), lambda b: (b, 0))] + specs,
        out_specs=pl.BlockSpec((n_new, ATTN_WIDTH), lambda b: (b, 0)),
        compiler_params=_params("parallel"),
        name="attn_sample",
    )(slope_tab, qkv, *caches_t)


def _alibi_slopes():
    n_heads = N_ATTN_GROUPS * ATTN_HPG
    return [2.0 ** (-8.0 * (h + 1) / n_heads) for h in range(n_heads)]


def kernel(x_prompt, x_sample, state_conv, state_ssm, cache_kv_g0, cache_kv_g1, cache_kv_g2, norm_w, w_ffn_in,
           w_ffn_out, ssm_w_in, ssm_conv_w, ssm_conv_b, ssm_dt_bias, ssm_a_log, ssm_d, ssm_norm_w, ssm_w_out,
           attn_w_qkv, attn_w_o, norm_f):
    bp, lp, d = x_prompt.shape
    bs, ls, _ = x_sample.shape
    xs_all = [x_prompt.reshape(bp * lp, d), x_sample.reshape(bs * ls, d)]
    dims = [(bp, lp), (bs, ls)]
    hist = CONV_W - 1
    hp = SSD_HEADS * SSD_HEADDIM

    w_in = w_ffn_in.astype(BF16)
    w_out = w_ffn_out.astype(BF16)
    pad = LANES - SSD_HEADS
    w_proj = jnp.pad(ssm_w_in[0], ((0, 0), (0, pad))).astype(BF16)
    ssd_prm = (ssm_conv_w[0], ssm_conv_b[0].reshape(1, CONV_DIM),
               jnp.pad(ssm_dt_bias[0], (0, pad)).reshape(1, LANES),
               jnp.pad(ssm_a_log[0], (0, pad)).reshape(1, LANES),
               jnp.repeat(ssm_d[0], SSD_HEADDIM).reshape(1, D_INNER),
               ssm_norm_w[0].reshape(1, D_INNER))
    w_ssm_out = ssm_w_out[0].astype(BF16)
    w_qkv = attn_w_qkv[0].astype(BF16)
    w_o = attn_w_o[0].astype(BF16)
    slope_tab = jnp.asarray(_alibi_slopes(), F32)

    conv_states = [jnp.zeros((bp, hist, CONV_DIM), F32), state_conv[0]]
    ssm_states = [jnp.zeros((bp, hp, D_STATE), F32), state_ssm[0].reshape(bs, hp, D_STATE)]
    caches_t = [jnp.transpose(c[0], (0, 2, 3, 4, 1)).reshape(bs, 2 * ATTN_WIDTH, c.shape[2])
                for c in (cache_kv_g0, cache_kv_g1, cache_kv_g2)]

    conv_out, ssm_out = [], []
    for n, (x, (b, l)) in enumerate(zip(xs_all, dims)):
        x = _ffn(x, norm_w[0, 0], w_in[0, 0], w_out[0, 0])
        z, xbc, dt_raw = _norm_mm(x, norm_w[0, 1], w_proj, (D_INNER, CONV_DIM, LANES))
        gz, new_conv, new_ssm = _ssd(z, xbc, dt_raw, conv_states[n], ssm_states[n], ssd_prm, b, l,
                                     BF16 if l % SSD_CHUNK == 0 else F32)
        conv_out.append(new_conv[None])
        ssm_out.append(new_ssm.reshape(1, b, SSD_HEADS, SSD_HEADDIM, D_STATE))
        x = _mm_res(gz, w_ssm_out, x)
        xs_all[n] = _ffn(x, norm_w[0, 2], w_in[0, 1], w_out[0, 1])

    kv_out = []
    for n, (x, (b, l)) in enumerate(zip(xs_all, dims)):
        x = _ffn(x, norm_w[1, 0], w_in[1, 0], w_out[1, 0])
        (qkv,) = _norm_mm(x, norm_w[1, 1], w_qkv, (QKV_WIDTH,))
        if n == 0:
            o = _attn_prompt(qkv, slope_tab, b, l)
            kv_t = [_kv_tail_t(qkv, g, b, l) for g in range(N_ATTN_GROUPS)]
            kv_out.append([jnp.transpose(t.reshape(b, 2, ATTN_HPG, ATTN_HEAD_DIM, t.shape[2]), (0, 4, 1, 2, 3))[None]
                           for t in kv_t])
        else:
            o = _attn_sample(qkv, caches_t, slope_tab, b, l)
            qkv5 = qkv.reshape(b, l, N_ATTN_GROUPS, 3, ATTN_HPG, ATTN_HEAD_DIM)
            kv_out.append([qkv5[:, :, g, 1:][None] for g in range(N_ATTN_GROUPS)])
        x = _mm_res(o, w_o, x)
        xs_all[n] = _ffn(x, norm_w[1, 2], w_in[1, 1], w_out[1, 1], g_final=norm_f)

    return (xs_all[0].reshape(bp, lp, d), xs_all[1].reshape(bs, ls, d),
            conv_out[0], conv_out[1], ssm_out[0], ssm_out[1],
            kv_out[0][0], kv_out[1][0], kv_out[0][1], kv_out[1][1], kv_out[0][2], kv_out[1][2])
```

```python
import functools

import jax
import jax.numpy as jnp
from jax import lax
from jax.experimental import pallas as pl
from jax.experimental.pallas import tpu as pltpu

F32 = jnp.float32
BF16 = jnp.bfloat16

EPS = 1e-6
NEG_INF = -1e30

D_MODEL = 1024
D_FF = 2816
D_INNER = 2048
SSD_HEADS = 32
SSD_HEADDIM = 64
SSD_GROUPS = 8
SSD_HPG = SSD_HEADS // SSD_GROUPS
GROUP_W = SSD_HPG * SSD_HEADDIM
D_STATE = 128
CONV_W = 4
CONV_DIM = D_INNER + 2 * SSD_GROUPS * D_STATE
SSD_CHUNK = 128
ATTN_PATTERNS = ((128, 1), (512, 4), (2048, 16))
N_ATTN_GROUPS = 3
ATTN_HPG = 8
ATTN_HEAD_DIM = 64
ATTN_WIDTH = ATTN_HPG * ATTN_HEAD_DIM
QKV_WIDTH = N_ATTN_GROUPS * 3 * ATTN_WIDTH
ATTN_QBLOCK = 128
ATTN_NKEYS = 128

LANES = 128
VMEM_LIMIT = 56 * 1024 * 1024

NT_DIMS = (((1,), (1,)), ((), ()))
TN_DIMS = (((0,), (0,)), ((), ()))


def _const_spec(shape):
    zeros = (0,) * len(shape)
    return pl.BlockSpec(shape, lambda *_: zeros, pipeline_mode=pl.Buffered(1))


def _params(*semantics):
    return pltpu.CompilerParams(dimension_semantics=semantics, vmem_limit_bytes=VMEM_LIMIT)


def _rms(x, g):
    return x * lax.rsqrt(jnp.mean(x * x, axis=-1, keepdims=True) + EPS) * g


def _silu(x):
    return x * jax.nn.sigmoid(x)


def _softplus(x):
    return jnp.maximum(x, 0.0) + jnp.log1p(jnp.exp(-jnp.abs(x)))


def _ffn_body(x_ref, g_ref, win_ref, wout_ref, *rest, final_norm):
    if final_norm:
        gf_ref, o_ref = rest
    else:
        (o_ref,) = rest
    x = x_ref[...]
    h = _rms(x, g_ref[...]).astype(BF16)
    a = jnp.dot(h, win_ref[:, :D_FF], preferred_element_type=F32)
    b = jnp.dot(h, win_ref[:, D_FF:], preferred_element_type=F32)
    t = (_silu(a) * b).astype(BF16)
    y = x + 0.5 * jnp.dot(t, wout_ref[...], preferred_element_type=F32)
    if final_norm:
        y = _rms(y, gf_ref[...])
    o_ref[...] = y


def _ffn(x, g, w_in, w_out, g_final=None, tm=512):
    m, d = x.shape
    tm = min(tm, m)
    row = pl.BlockSpec((tm, d), lambda i: (i, 0))
    in_specs = [row, _const_spec((1, d)), _const_spec(w_in.shape), _const_spec(w_out.shape)]
    args = [x, g.reshape(1, d), w_in, w_out]
    if g_final is not None:
        in_specs.append(_const_spec((1, d)))
        args.append(g_final.reshape(1, d))
    return pl.pallas_call(
        functools.partial(_ffn_body, final_norm=g_final is not None),
        out_shape=jax.ShapeDtypeStruct((m, d), F32),
        grid=(m // tm,),
        in_specs=in_specs,
        out_specs=row,
        compiler_params=_params("parallel"),
        name="ffn",
    )(*args)


def _norm_mm_body(x_ref, g_ref, w_ref, *o_refs, splits):
    h = _rms(x_ref[...], g_ref[...]).astype(BF16)
    off = 0
    for o_ref, n in zip(o_refs, splits):
        o_ref[...] = jnp.dot(h, w_ref[:, off:off + n], preferred_element_type=F32)
        off += n


def _norm_mm(x, g, w, splits, tm=256):
    m, d = x.shape
    tm = min(tm, m)
    return pl.pallas_call(
        functools.partial(_norm_mm_body, splits=splits),
        out_shape=[jax.ShapeDtypeStruct((m, n), F32) for n in splits],
        grid=(m // tm,),
        in_specs=[pl.BlockSpec((tm, d), lambda i: (i, 0)), _const_spec((1, d)), _const_spec(w.shape)],
        out_specs=[pl.BlockSpec((tm, n), lambda i: (i, 0)) for n in splits],
        compiler_params=_params("parallel"),
        name="norm_mm",
    )(x, g.reshape(1, d), w)


def _mm_res_body(y_ref, w_ref, x_ref, o_ref):
    o_ref[...] = x_ref[...] + jnp.dot(y_ref[...].astype(BF16), w_ref[...], preferred_element_type=F32)


def _mm_res(y, w, x, tm=512):
    m, k = y.shape
    d = x.shape[1]
    tm = min(tm, m)
    return pl.pallas_call(
        _mm_res_body,
        out_shape=jax.ShapeDtypeStruct((m, d), F32),
        grid=(m // tm,),
        in_specs=[pl.BlockSpec((tm, k), lambda i: (i, 0)), _const_spec(w.shape),
                  pl.BlockSpec((tm, d), lambda i: (i, 0))],
        out_specs=pl.BlockSpec((tm, d), lambda i: (i, 0)),
        compiler_params=_params("parallel"),
        name="mm_res",
    )(y, w, x)


def _expand_heads(v, first, width):
    rows = v.shape[0]
    return jnp.concatenate(
        [jnp.broadcast_to(v[:, first + j:first + j + 1], (rows, width)) for j in range(SSD_HPG)], axis=1)


def _ssd_chunk(q, xpad_ref, row0, z, dt_raw, cw_ref, cb_ref, dtb_ref, alog_ref, drep_ref, ng_ref,
               h_in_ref, h_out_ref, gz_ref):
    def conv(slab):
        sl = slice(slab * LANES, (slab + 1) * LANES)
        acc = cb_ref[:, sl]
        for k in range(CONV_W):
            lo = row0 - (CONV_W - 1) + k
            acc = acc + cw_ref[k:k + 1, sl] * xpad_ref[slab, lo:lo + q, :]
        return _silu(acc)

    x_slabs = D_INNER // LANES
    bc_slabs = SSD_GROUPS * D_STATE // LANES

    dt = _softplus(dt_raw + dtb_ref[...])
    a = -jnp.exp(alog_ref[...])
    row = lax.broadcasted_iota(jnp.int32, (q, q), 0)
    col = lax.broadcasted_iota(jnp.int32, (q, q), 1)
    causal = row >= col
    a_cum = jnp.dot(causal.astype(F32), dt * a, precision=lax.Precision.HIGHEST, preferred_element_type=F32)
    a_last = a_cum[q - 1:q, :]
    w_state = jnp.exp(a_last - a_cum) * dt
    e_cum = jnp.exp(a_cum)
    e_last = jnp.exp(a_last)
    a_cum_t = a_cum.T
    dt_t = dt.T

    for g in range(SSD_GROUPS):
        h0 = g * SSD_HPG
        c0 = g * GROUP_W
        bmb = conv(x_slabs + g).astype(BF16)
        cmb = conv(x_slabs + bc_slabs + g).astype(BF16)
        cb = lax.dot_general(cmb, bmb, NT_DIMS, preferred_element_type=F32)
        x_g = jnp.concatenate([conv(c0 // LANES + i) for i in range(GROUP_W // LANES)], axis=1)
        y_heads = []
        for j in range(SSD_HPG):
            h = h0 + j
            seg = a_cum[:, h:h + 1] - a_cum_t[h:h + 1, :]
            decay = jnp.exp(jnp.where(causal, seg, -jnp.inf))
            wts = (cb * decay * dt_t[h:h + 1, :]).astype(BF16)
            x_h = x_g[:, j * SSD_HEADDIM:(j + 1) * SSD_HEADDIM].astype(BF16)
            y_heads.append(jnp.dot(wts, x_h, preferred_element_type=F32))
        y_diag = jnp.concatenate(y_heads, axis=1)

        h_prev = h_in_ref[c0:c0 + GROUP_W, :]
        y_off = lax.dot_general(cmb, h_prev.astype(BF16), NT_DIMS, preferred_element_type=F32)
        y_off = y_off * _expand_heads(e_cum, h0, SSD_HEADDIM)
        xw = (x_g * _expand_heads(w_state, h0, SSD_HEADDIM)).astype(BF16)
        states = lax.dot_general(xw, bmb, TN_DIMS, preferred_element_type=F32)
        carry = jnp.concatenate(
            [jnp.broadcast_to(e_last[:, h0 + j:h0 + j + 1], (SSD_HEADDIM, D_STATE)) for j in range(SSD_HPG)], axis=0)
        h_out_ref[c0:c0 + GROUP_W, :] = carry * h_prev + states

        y = y_diag + y_off + drep_ref[:, c0:c0 + GROUP_W] * x_g
        gz = y * _silu(z[:, c0:c0 + GROUP_W])
        gz = gz * lax.rsqrt(jnp.mean(gz * gz, axis=-1, keepdims=True) + EPS) * ng_ref[:, c0:c0 + GROUP_W]
        gz_ref[:, c0:c0 + GROUP_W] = gz.astype(gz_ref.dtype)


PAD_ROWS = 8


CONV_SLABS = CONV_DIM // LANES
CONV_HIST = CONV_W - 1
CHUNKS_PER_STEP = 2


def _ssd_prompt_body(x_ref, g_ref, wp_ref, cs_ref, h0_ref, cw_ref, cb_ref, dtb_ref, alog_ref, drep_ref, ng_ref,
                     wo_ref, o_ref, nc_ref, hout_ref, xpad_ref, gz_ref):
    q = SSD_CHUNK
    rows = CHUNKS_PER_STEP * q

    @pl.when(pl.program_id(1) == 0)
    def _():
        for j in range(CONV_SLABS):
            xpad_ref[j, PAD_ROWS - CONV_HIST:PAD_ROWS, :] = cs_ref[0, :, j * LANES:(j + 1) * LANES]
        hout_ref[0] = h0_ref[0]

    x = x_ref[...]
    h = _rms(x, g_ref[...]).astype(BF16)
    z = jnp.dot(h, wp_ref[:, :D_INNER], preferred_element_type=F32)
    xbc = jnp.dot(h, wp_ref[:, D_INNER:D_INNER + CONV_DIM], preferred_element_type=F32)
    dt_raw = jnp.dot(h, wp_ref[:, D_INNER + CONV_DIM:], preferred_element_type=F32)
    for j in range(CONV_SLABS):
        xpad_ref[j, PAD_ROWS:PAD_ROWS + rows, :] = xbc[:, j * LANES:(j + 1) * LANES]
    state = hout_ref.at[0]
    for c in range(CHUNKS_PER_STEP):
        sl = slice(c * q, (c + 1) * q)
        _ssd_chunk(q, xpad_ref, PAD_ROWS + c * q, z[sl], dt_raw[sl], cw_ref, cb_ref, dtb_ref, alog_ref, drep_ref,
                   ng_ref, state, state, gz_ref.at[pl.ds(c * q, q)])
    o_ref[...] = x + jnp.dot(gz_ref[...], wo_ref[...], preferred_element_type=F32)
    for j in range(CONV_SLABS):
        tail = xpad_ref[j, PAD_ROWS + rows - CONV_HIST:PAD_ROWS + rows, :]
        nc_ref[0, :, j * LANES:(j + 1) * LANES] = tail
        xpad_ref[j, PAD_ROWS - CONV_HIST:PAD_ROWS, :] = tail


def _ssd_prompt(x, g, w_proj, conv_state, ssm_state, prm, w_out, batch, seq):
    rows = CHUNKS_PER_STEP * SSD_CHUNK
    steps = seq // rows
    d = x.shape[1]
    hp = SSD_HEADS * SSD_HEADDIM
    row = lambda b, c: (b * steps + c, 0)
    per_b = lambda b, c: (b, 0, 0)
    consts = list(prm) + [w_out]
    return pl.pallas_call(
        _ssd_prompt_body,
        out_shape=[jax.ShapeDtypeStruct(x.shape, F32),
                   jax.ShapeDtypeStruct((batch, CONV_HIST, CONV_DIM), F32),
                   jax.ShapeDtypeStruct((batch, hp, D_STATE), F32)],
        grid=(batch, steps),
        in_specs=[pl.BlockSpec((rows, d), row), _const_spec((1, d)), _const_spec(w_proj.shape),
                  pl.BlockSpec((1, CONV_HIST, CONV_DIM), per_b), pl.BlockSpec((1, hp, D_STATE), per_b)]
        + [_const_spec(c.shape) for c in consts],
        out_specs=[pl.BlockSpec((rows, d), row), pl.BlockSpec((1, CONV_HIST, CONV_DIM), per_b),
                   pl.BlockSpec((1, hp, D_STATE), per_b)],
        scratch_shapes=[pltpu.VMEM((CONV_SLABS, PAD_ROWS + rows, LANES), F32), pltpu.VMEM((rows, D_INNER), BF16)],
        compiler_params=_params("parallel", "arbitrary"),
        name="ssd_prompt",
    )(x, g.reshape(1, d), w_proj, conv_state, ssm_state, *consts)


SEQS_PER_STEP = 2


def _ssd_sample_body(z_ref, xbc_ref, dt_ref, cs_ref, h0_ref, cw_ref, cb_ref, dtb_ref, alog_ref, drep_ref, ng_ref,
                     gz_ref, nc_ref, hout_ref, xpad_ref, *, q):
    span = PAD_ROWS + q
    for e in range(SEQS_PER_STEP):
        row0 = e * span + PAD_ROWS
        sl = slice(e * q, (e + 1) * q)
        for j in range(CONV_SLABS):
            lanes = slice(j * LANES, (j + 1) * LANES)
            xpad_ref[j, row0 - CONV_HIST:row0, :] = cs_ref[e, :, lanes]
            xpad_ref[j, row0:row0 + q, :] = xbc_ref[sl, lanes]
        _ssd_chunk(q, xpad_ref, row0, z_ref[sl, :], dt_ref[sl, :], cw_ref, cb_ref, dtb_ref, alog_ref, drep_ref,
                   ng_ref, h0_ref.at[e], hout_ref.at[e], gz_ref.at[pl.ds(e * q, q)])
        for j in range(CONV_SLABS):
            nc_ref[e, :, j * LANES:(j + 1) * LANES] = xpad_ref[j, row0 + q - CONV_HIST:row0 + q, :]


def _ssd_sample(z, xbc, dt_raw, conv_state, ssm_state, prm, batch, seq):
    n = SEQS_PER_STEP
    hp = SSD_HEADS * SSD_HEADDIM
    row = lambda b: (b, 0)
    per_b = lambda b: (b, 0, 0)
    return pl.pallas_call(
        functools.partial(_ssd_sample_body, q=seq),
        out_shape=[jax.ShapeDtypeStruct((batch * seq, D_INNER), F32),
                   jax.ShapeDtypeStruct((batch, CONV_HIST, CONV_DIM), F32),
                   jax.ShapeDtypeStruct((batch, hp, D_STATE), F32)],
        grid=(batch // n,),
        in_specs=[pl.BlockSpec((n * seq, D_INNER), row), pl.BlockSpec((n * seq, CONV_DIM), row),
                  pl.BlockSpec((n * seq, LANES), row),
                  pl.BlockSpec((n, CONV_HIST, CONV_DIM), per_b), pl.BlockSpec((n, hp, D_STATE), per_b)]
        + [_const_spec(c.shape) for c in prm],
        out_specs=[pl.BlockSpec((n * seq, D_INNER), row), pl.BlockSpec((n, CONV_HIST, CONV_DIM), per_b),
                   pl.BlockSpec((n, hp, D_STATE), per_b)],
        scratch_shapes=[pltpu.VMEM((CONV_SLABS, n * (PAD_ROWS + seq), LANES), F32)],
        compiler_params=_params("parallel"),
        name="ssd_sample",
    )(z, xbc, dt_raw, conv_state, ssm_state, *prm)


ATTN_SLAB = ATTN_QBLOCK * 16
HEAD_PAIRS = ATTN_HPG // 2


def _attn_prompt_body(slope_ref, *refs):
    qkv_refs = refs[:9]
    o_ref = refs[9]
    m_sc, l_sc, acc_sc = refs[10:13]
    kv_ext = refs[13:]
    pair = pl.program_id(1)
    s_idx = pl.program_id(2)
    qb = ATTN_QBLOCK

    for g, (_, dil) in enumerate(ATTN_PATTERNS):
        hist = qb * dil
        for ext, src in zip(kv_ext[2 * g:2 * g + 2], qkv_refs[3 * g + 1:3 * g + 3]):
            @pl.when(s_idx == 0)
            def _(ext=ext, hist=hist):
                ext[0:hist, :] = jnp.zeros((hist, LANES), F32)

            ext[hist:hist + ATTN_SLAB, :] = src[...]

    row = lax.broadcasted_iota(jnp.int32, (qb, 2 * qb), 0)
    col = lax.broadcasted_iota(jnp.int32, (qb, 2 * qb), 1)
    dist = qb + row - col
    in_window = (dist >= 0) & (dist <= ATTN_NKEYS)
    dist_f = dist.astype(F32)
    lane = lax.broadcasted_iota(jnp.int32, (qb, LANES), 1)
    low_half = lane < ATTN_HEAD_DIM

    for g, (_, dil) in enumerate(ATTN_PATTERNS):
        q_ref = qkv_refs[3 * g]
        k_ext, v_ext = kv_ext[2 * g:2 * g + 2]
        n_blk = ATTN_SLAB // (qb * dil)
        slopes = [slope_ref[g * ATTN_HPG + 2 * pair + e] * float(dil) for e in range(2)]

        def block(idx, carry, g=g, dil=dil, n_blk=n_blk, q_ref=q_ref, k_ext=k_ext, v_ext=v_ext, slopes=slopes):
            r = idx // n_blk
            blk = idx % n_blk
            if dil == 1:
                start = pl.multiple_of(qb * blk, qb)
                rows = pl.ds(start, qb)
                key_rows = pl.ds(start, 2 * qb)
            else:
                start = r + (dil * qb) * blk
                rows = pl.ds(start, qb, stride=dil)
                key_rows = pl.ds(start, 2 * qb, stride=dil)
            k2 = k_ext[key_rows, :].astype(BF16)
            v2 = v_ext[key_rows, :].astype(BF16)
            q2 = q_ref[rows, :] * (ATTN_HEAD_DIM ** -0.5)
            first_col = jnp.where((blk > 0) | (s_idx > 0), 0, qb)
            valid = in_window & (col >= first_col)
            ms, ls, pvs = [], [], []
            for e in range(2):
                qm = jnp.where(low_half if e == 0 else ~low_half, q2, 0.0).astype(BF16)
                s = lax.dot_general(qm, k2, NT_DIMS, preferred_element_type=F32) - slopes[e] * dist_f
                s = jnp.where(valid, s, NEG_INF)
                m = jnp.max(s, axis=-1, keepdims=True)
                p = jnp.exp(s - m)
                ms.append(m)
                ls.append(jnp.sum(p, axis=-1, keepdims=True))
                pvs.append(jnp.dot(p.astype(BF16), v2, preferred_element_type=F32))
            m_new = jnp.where(low_half, ms[0], ms[1])
            l_new = jnp.where(low_half, ls[0], ls[1])
            pv_new = jnp.where(low_half, pvs[0], pvs[1])
            if g > 0:
                m_old = m_sc[rows, :]
                m_tot = jnp.maximum(m_old, m_new)
                a_old = jnp.exp(m_old - m_tot)
                a_new = jnp.exp(m_new - m_tot)
                l_new = a_old * l_sc[rows, :] + a_new * l_new
                pv_new = a_old * acc_sc[rows, :] + a_new * pv_new
                m_new = m_tot
            if g == N_ATTN_GROUPS - 1:
                o_ref[rows, :] = pv_new / l_new
            else:
                m_sc[rows, :] = m_new
                l_sc[rows, :] = l_new
                acc_sc[rows, :] = pv_new
            return carry

        lax.fori_loop(0, ATTN_SLAB // qb, block, 0, unroll=2)

    for g, (_, dil) in enumerate(ATTN_PATTERNS):
        hist = qb * dil
        for ext in kv_ext[2 * g:2 * g + 2]:
            ext[0:hist, :] = ext[ATTN_SLAB:ATTN_SLAB + hist, :]


def _attn_prompt(qkv, slope_tab, batch, seq):
    n_slab = seq // ATTN_SLAB
    col_blocks = ATTN_WIDTH // LANES

    def spec(g, which):
        base = (3 * g + which) * col_blocks
        return pl.BlockSpec((ATTN_SLAB, LANES), lambda b, p, s: (b * n_slab + s, base + p))

    in_specs = [pl.BlockSpec(memory_space=pltpu.SMEM)]
    in_specs += [spec(g, which) for g in range(N_ATTN_GROUPS) for which in range(3)]
    return pl.pallas_call(
        _attn_prompt_body,
        out_shape=jax.ShapeDtypeStruct((batch * seq, ATTN_WIDTH), F32),
        grid=(batch, HEAD_PAIRS, n_slab),
        in_specs=in_specs,
        out_specs=pl.BlockSpec((ATTN_SLAB, LANES), lambda b, p, s: (b * n_slab + s, p)),
        scratch_shapes=[pltpu.VMEM((ATTN_SLAB, LANES), F32)] * 3 + [
            pltpu.VMEM((ATTN_QBLOCK * dil + ATTN_SLAB, LANES), F32) for _, dil in ATTN_PATTERNS for _ in range(2)],
        compiler_params=_params("parallel", "parallel", "arbitrary"),
        name="attn_prompt",
    )(slope_tab, *([qkv] * 9))


def _kv_tail_body(k_ref, v_ref, o_ref):
    o_ref[0, :ATTN_WIDTH, :] = k_ref[...].T
    o_ref[0, ATTN_WIDTH:, :] = v_ref[...].T


def _kv_tail_t(qkv, g, batch, seq, tm=512):
    win = min(ATTN_PATTERNS[g][0], seq)
    tm = min(tm, win)
    first = (seq - win) // tm
    per_b = seq // tm

    def spec(which):
        return pl.BlockSpec((tm, ATTN_WIDTH), lambda b, j: (b * per_b + first + j, 3 * g + which))

    return pl.pallas_call(
        _kv_tail_body,
        out_shape=jax.ShapeDtypeStruct((batch, 2 * ATTN_WIDTH, win), F32),
        grid=(batch, win // tm),
        in_specs=[spec(1), spec(2)],
        out_specs=pl.BlockSpec((1, 2 * ATTN_WIDTH, tm), lambda b, j: (b, 0, j)),
        compiler_params=_params("parallel", "parallel"),
        name=f"kv_tail_g{g}",
    )(qkv, qkv)


def _attn_sample_body(slope_ref, qkv_ref, c0_ref, c1_ref, c2_ref, o_ref, *, n_new):
    qkv = qkv_ref[...]
    caches = (c0_ref, c1_ref, c2_ref)
    rows = ATTN_HPG * n_new
    row_head = lax.broadcasted_iota(jnp.int32, (rows, ATTN_WIDTH), 0) // n_new
    lane_head = lax.broadcasted_iota(jnp.int32, (rows, ATTN_WIDTH), 1) // ATTN_HEAD_DIM
    diag = row_head == lane_head
    head_col = lax.broadcasted_iota(jnp.int32, (rows, 1), 0) // n_new
    qi_n = lax.broadcasted_iota(jnp.int32, (rows, n_new), 0) % n_new
    gap_n = qi_n - lax.broadcasted_iota(jnp.int32, (rows, n_new), 1)
    outs, lses = [], []
    for g, (c_ref, (win, dil)) in enumerate(zip(caches, ATTN_PATTERNS)):
        base = g * 3 * ATTN_WIDTH
        past_len = c_ref.shape[2]
        slope = jnp.zeros((rows, 1), F32)
        for h in range(ATTN_HPG):
            slope = jnp.where(head_col == h, slope_ref[g * ATTN_HPG + h], slope)
        gap_p = (past_len + lax.broadcasted_iota(jnp.int32, (rows, past_len), 0) % n_new
                 - lax.broadcasted_iota(jnp.int32, (rows, past_len), 1))
        ok_p = (gap_p <= win) & ((gap_p & (dil - 1)) == 0)
        ok_n = (gap_n >= 0) & ((gap_n & (dil - 1)) == 0)
        q = qkv[:, base:base + ATTN_WIDTH] * (ATTN_HEAD_DIM ** -0.5)
        q_bd = jnp.where(diag, jnp.concatenate([q] * ATTN_HPG, axis=0), 0.0).astype(BF16)
        k_new = qkv[:, base + ATTN_WIDTH:base + 2 * ATTN_WIDTH].astype(BF16)
        v_new = qkv[:, base + 2 * ATTN_WIDTH:base + 3 * ATTN_WIDTH].astype(BF16)
        k_t = c_ref[0, :ATTN_WIDTH, :].astype(BF16)
        v_t = c_ref[0, ATTN_WIDTH:, :].astype(BF16)
        s_p = jnp.dot(q_bd, k_t, preferred_element_type=F32) - slope * gap_p.astype(F32)
        s_p = jnp.where(ok_p, s_p, NEG_INF)
        s_n = lax.dot_general(q_bd, k_new, NT_DIMS, preferred_element_type=F32) - slope * gap_n.astype(F32)
        s_n = jnp.where(ok_n, s_n, NEG_INF)
        mx = jnp.maximum(jnp.max(s_p, axis=-1, keepdims=True), jnp.max(s_n, axis=-1, keepdims=True))
        e_p = jnp.exp(s_p - mx)
        e_n = jnp.exp(s_n - mx)
        l = jnp.sum(e_p, axis=-1, keepdims=True) + jnp.sum(e_n, axis=-1, keepdims=True)
        o = (lax.dot_general(e_p.astype(BF16), v_t, NT_DIMS, preferred_element_type=F32)
             + jnp.dot(e_n.astype(BF16), v_new, preferred_element_type=F32)) / l
        outs.append(o)
        lses.append(mx + jnp.log(l))
    top = jnp.maximum(jnp.maximum(lses[0], lses[1]), lses[2])
    es = [jnp.exp(l - top) for l in lses]
    merged = (es[0] * outs[0] + es[1] * outs[1] + es[2] * outs[2]) / (es[0] + es[1] + es[2])
    merged = jnp.where(diag, merged, 0.0).reshape(ATTN_HPG, n_new, ATTN_WIDTH)
    o_ref[...] = jnp.sum(merged, axis=0)


def _attn_sample(qkv, caches_t, slope_tab, batch, n_new):
    specs = [pl.BlockSpec((1,) + c.shape[1:], lambda b: (b, 0, 0)) for c in caches_t]
    return pl.pallas_call(
        functools.partial(_attn_sample_body, n_new=n_new),
        out_shape=jax.ShapeDtypeStruct((batch * n_new, ATTN_WIDTH), F32),
        grid=(batch,),
        in_specs=[pl.BlockSpec(memory_space=pltpu.SMEM), pl.BlockSpec((n_new, QKV_WIDTH), lambda b: (b, 0))] + specs,
        out_specs=pl.BlockSpec((n_new, ATTN_WIDTH), lambda b: (b, 0)),
        compiler_params=_params("parallel"),
        name="attn_sample",
    )(slope_tab, qkv, *caches_t)


def _alibi_slopes():
    n_heads = N_ATTN_GROUPS * ATTN_HPG
    return [2.0 ** (-8.0 * (h + 1) / n_heads) for h in range(n_heads)]


def kernel(x_prompt, x_sample, state_conv, state_ssm, cache_kv_g0, cache_kv_g1, cache_kv_g2, norm_w, w_ffn_in,
           w_ffn_out, ssm_w_in, ssm_conv_w, ssm_conv_b, ssm_dt_bias, ssm_a_log, ssm_d, ssm_norm_w, ssm_w_out,
           attn_w_qkv, attn_w_o, norm_f):
    bp, lp, d = x_prompt.shape
    bs, ls, _ = x_sample.shape
    xs_all = [x_prompt.reshape(bp * lp, d), x_sample.reshape(bs * ls, d)]
    dims = [(bp, lp), (bs, ls)]
    hist = CONV_W - 1
    hp = SSD_HEADS * SSD_HEADDIM

    w_in = w_ffn_in.astype(BF16)
    w_out = w_ffn_out.astype(BF16)
    pad = LANES - SSD_HEADS
    w_proj = jnp.pad(ssm_w_in[0], ((0, 0), (0, pad))).astype(BF16)
    ssd_prm = (ssm_conv_w[0], ssm_conv_b[0].reshape(1, CONV_DIM),
               jnp.pad(ssm_dt_bias[0], (0, pad)).reshape(1, LANES),
               jnp.pad(ssm_a_log[0], (0, pad)).reshape(1, LANES),
               jnp.repeat(ssm_d[0], SSD_HEADDIM).reshape(1, D_INNER),
               ssm_norm_w[0].reshape(1, D_INNER))
    w_ssm_out = ssm_w_out[0].astype(BF16)
    w_qkv = attn_w_qkv[0].astype(BF16)
    w_o = attn_w_o[0].astype(BF16)
    slope_tab = jnp.asarray(_alibi_slopes(), F32)

    conv_states = [jnp.zeros((bp, hist, CONV_DIM), F32), state_conv[0]]
    ssm_states = [jnp.zeros((bp, hp, D_STATE), F32), state_ssm[0].reshape(bs, hp, D_STATE)]
    caches_t = [jnp.transpose(c[0], (0, 2, 3, 4, 1)).reshape(bs, 2 * ATTN_WIDTH, c.shape[2])
                for c in (cache_kv_g0, cache_kv_g1, cache_kv_g2)]

    conv_out, ssm_out = [], []
    for n, (x, (b, l)) in enumerate(zip(xs_all, dims)):
        x = _ffn(x, norm_w[0, 0], w_in[0, 0], w_out[0, 0])
        if l % (CHUNKS_PER_STEP * SSD_CHUNK) == 0:
            x, new_conv, new_ssm = _ssd_prompt(x, norm_w[0, 1], w_proj, conv_states[n], ssm_states[n], ssd_prm,
                                               w_ssm_out, b, l)
        else:
            z, xbc, dt_raw = _norm_mm(x, norm_w[0, 1], w_proj, (D_INNER, CONV_DIM, LANES))
            gz, new_conv, new_ssm = _ssd_sample(z, xbc, dt_raw, conv_states[n], ssm_states[n], ssd_prm, b, l)
            x = _mm_res(gz, w_ssm_out, x)
        conv_out.append(new_conv[None])
        ssm_out.append(new_ssm.reshape(1, b, SSD_HEADS, SSD_HEADDIM, D_STATE))
        xs_all[n] = _ffn(x, norm_w[0, 2], w_in[0, 1], w_out[0, 1])

    kv_out = []
    for n, (x, (b, l)) in enumerate(zip(xs_all, dims)):
        x = _ffn(x, norm_w[1, 0], w_in[1, 0], w_out[1, 0])
        (qkv,) = _norm_mm(x, norm_w[1, 1], w_qkv, (QKV_WIDTH,))
        if n == 0:
            o = _attn_prompt(qkv, slope_tab, b, l)
            kv_t = [_kv_tail_t(qkv, g, b, l) for g in range(N_ATTN_GROUPS)]
            kv_out.append([jnp.transpose(t.reshape(b, 2, ATTN_HPG, ATTN_HEAD_DIM, t.shape[2]), (0, 4, 1, 2, 3))[None]
                           for t in kv_t])
        else:
            o = _attn_sample(qkv, caches_t, slope_tab, b, l)
            qkv5 = qkv.reshape(b, l, N_ATTN_GROUPS, 3, ATTN_HPG, ATTN_HEAD_DIM)
            kv_out.append([qkv5[:, :, g, 1:][None] for g in range(N_ATTN_GROUPS)])
        x = _mm_res(o, w_o, x)
        xs_all[n] = _ffn(x, norm_w[1, 2], w_in[1, 1], w_out[1, 1], g_final=norm_f)

    return (xs_all[0].reshape(bp, lp, d), xs_all[1].reshape(bs, ls, d),
            conv_out[0], conv_out[1], ssm_out[0], ssm_out[1],
            kv_out[0][0], kv_out[1][0], kv_out[0][1], kv_out[1][1], kv_out[0][2], kv_out[1][2])
```

```python
import functools

import jax
import jax.numpy as jnp
from jax import lax
from jax.experimental import pallas as pl
from jax.experimental.pallas import tpu as pltpu

F32 = jnp.float32
BF16 = jnp.bfloat16

EPS = 1e-6
NEG_INF = -1e30

D_MODEL = 1024
D_FF = 2816
D_INNER = 2048
SSD_HEADS = 32
SSD_HEADDIM = 64
SSD_GROUPS = 8
SSD_HPG = SSD_HEADS // SSD_GROUPS
GROUP_W = SSD_HPG * SSD_HEADDIM
D_STATE = 128
CONV_W = 4
CONV_DIM = D_INNER + 2 * SSD_GROUPS * D_STATE
SSD_CHUNK = 128
ATTN_PATTERNS = ((128, 1), (512, 4), (2048, 16))
N_ATTN_GROUPS = 3
ATTN_HPG = 8
ATTN_HEAD_DIM = 64
ATTN_WIDTH = ATTN_HPG * ATTN_HEAD_DIM
QKV_WIDTH = N_ATTN_GROUPS * 3 * ATTN_WIDTH
ATTN_QBLOCK = 128
ATTN_NKEYS = 128

LANES = 128
VMEM_LIMIT = 56 * 1024 * 1024

NT_DIMS = (((1,), (1,)), ((), ()))
TN_DIMS = (((0,), (0,)), ((), ()))


def _const_spec(shape):
    zeros = (0,) * len(shape)
    return pl.BlockSpec(shape, lambda *_: zeros, pipeline_mode=pl.Buffered(1))


def _params(*semantics):
    return pltpu.CompilerParams(dimension_semantics=semantics, vmem_limit_bytes=VMEM_LIMIT)


def _rms(x, g):
    return x * lax.rsqrt(jnp.mean(x * x, axis=-1, keepdims=True) + EPS) * g


def _silu(x):
    return x * jax.nn.sigmoid(x)


def _softplus(x):
    return jnp.maximum(x, 0.0) + jnp.log1p(jnp.exp(-jnp.abs(x)))


def _ffn_body(x_ref, g_ref, win_ref, wout_ref, *rest, final_norm):
    if final_norm:
        gf_ref, o_ref = rest
    else:
        (o_ref,) = rest
    x = x_ref[...]
    h = _rms(x, g_ref[...]).astype(BF16)
    a = jnp.dot(h, win_ref[:, :D_FF], preferred_element_type=F32)
    b = jnp.dot(h, win_ref[:, D_FF:], preferred_element_type=F32)
    t = (_silu(a) * b).astype(BF16)
    y = x + 0.5 * jnp.dot(t, wout_ref[...], preferred_element_type=F32)
    if final_norm:
        y = _rms(y, gf_ref[...])
    o_ref[...] = y


def _ffn(x, g, w_in, w_out, g_final=None, tm=512):
    m, d = x.shape
    tm = min(tm, m)
    row = pl.BlockSpec((tm, d), lambda i: (i, 0))
    in_specs = [row, _const_spec((1, d)), _const_spec(w_in.shape), _const_spec(w_out.shape)]
    args = [x, g.reshape(1, d), w_in, w_out]
    if g_final is not None:
        in_specs.append(_const_spec((1, d)))
        args.append(g_final.reshape(1, d))
    return pl.pallas_call(
        functools.partial(_ffn_body, final_norm=g_final is not None),
        out_shape=jax.ShapeDtypeStruct((m, d), F32),
        grid=(m // tm,),
        in_specs=in_specs,
        out_specs=row,
        compiler_params=_params("parallel"),
        name="ffn",
    )(*args)


def _norm_mm_body(x_ref, g_ref, w_ref, *o_refs, splits):
    h = _rms(x_ref[...], g_ref[...]).astype(BF16)
    off = 0
    for o_ref, n in zip(o_refs, splits):
        o_ref[...] = jnp.dot(h, w_ref[:, off:off + n], preferred_element_type=F32)
        off += n


def _norm_mm(x, g, w, splits, tm=256):
    m, d = x.shape
    tm = min(tm, m)
    return pl.pallas_call(
        functools.partial(_norm_mm_body, splits=splits),
        out_shape=[jax.ShapeDtypeStruct((m, n), F32) for n in splits],
        grid=(m // tm,),
        in_specs=[pl.BlockSpec((tm, d), lambda i: (i, 0)), _const_spec((1, d)), _const_spec(w.shape)],
        out_specs=[pl.BlockSpec((tm, n), lambda i: (i, 0)) for n in splits],
        compiler_params=_params("parallel"),
        name="norm_mm",
    )(x, g.reshape(1, d), w)


def _mm_res_body(y_ref, w_ref, x_ref, o_ref):
    o_ref[...] = x_ref[...] + jnp.dot(y_ref[...].astype(BF16), w_ref[...], preferred_element_type=F32)


def _mm_res(y, w, x, tm=512):
    m, k = y.shape
    d = x.shape[1]
    tm = min(tm, m)
    return pl.pallas_call(
        _mm_res_body,
        out_shape=jax.ShapeDtypeStruct((m, d), F32),
        grid=(m // tm,),
        in_specs=[pl.BlockSpec((tm, k), lambda i: (i, 0)), _const_spec(w.shape),
                  pl.BlockSpec((tm, d), lambda i: (i, 0))],
        out_specs=pl.BlockSpec((tm, d), lambda i: (i, 0)),
        compiler_params=_params("parallel"),
        name="mm_res",
    )(y, w, x)


def _expand_heads(v, first, width):
    rows = v.shape[0]
    return jnp.concatenate(
        [jnp.broadcast_to(v[:, first + j:first + j + 1], (rows, width)) for j in range(SSD_HPG)], axis=1)


def _ssd_chunk(q, xpad_ref, row0, z, dt_raw, cw_ref, cb_ref, dtb_ref, alog_ref, drep_ref, ng_ref,
               h_in_ref, h_out_ref, gz_ref):
    def conv(slab):
        sl = slice(slab * LANES, (slab + 1) * LANES)
        acc = cb_ref[:, sl]
        for k in range(CONV_W):
            lo = row0 - (CONV_W - 1) + k
            acc = acc + cw_ref[k:k + 1, sl] * xpad_ref[slab, lo:lo + q, :]
        return _silu(acc)

    x_slabs = D_INNER // LANES
    bc_slabs = SSD_GROUPS * D_STATE // LANES

    dt = _softplus(dt_raw + dtb_ref[...])
    a = -jnp.exp(alog_ref[...])
    row = lax.broadcasted_iota(jnp.int32, (q, q), 0)
    col = lax.broadcasted_iota(jnp.int32, (q, q), 1)
    causal = row >= col
    a_cum = jnp.dot(causal.astype(F32), dt * a, precision=lax.Precision.HIGHEST, preferred_element_type=F32)
    a_last = a_cum[q - 1:q, :]
    w_state = jnp.exp(a_last - a_cum) * dt
    e_cum = jnp.exp(a_cum)
    e_last = jnp.exp(a_last)
    a_cum_t = a_cum.T
    dt_t = dt.T

    for g in range(SSD_GROUPS):
        h0 = g * SSD_HPG
        c0 = g * GROUP_W
        bmb = conv(x_slabs + g).astype(BF16)
        cmb = conv(x_slabs + bc_slabs + g).astype(BF16)
        cb = lax.dot_general(cmb, bmb, NT_DIMS, preferred_element_type=F32)
        x_g = jnp.concatenate([conv(c0 // LANES + i) for i in range(GROUP_W // LANES)], axis=1)
        y_heads = []
        for j in range(SSD_HPG):
            h = h0 + j
            seg = a_cum[:, h:h + 1] - a_cum_t[h:h + 1, :]
            decay = jnp.exp(jnp.where(causal, seg, -jnp.inf))
            wts = (cb * decay * dt_t[h:h + 1, :]).astype(BF16)
            x_h = x_g[:, j * SSD_HEADDIM:(j + 1) * SSD_HEADDIM].astype(BF16)
            y_heads.append(jnp.dot(wts, x_h, preferred_element_type=F32))
        y_diag = jnp.concatenate(y_heads, axis=1)

        h_prev = h_in_ref[c0:c0 + GROUP_W, :]
        y_off = lax.dot_general(cmb, h_prev.astype(BF16), NT_DIMS, preferred_element_type=F32)
        y_off = y_off * _expand_heads(e_cum, h0, SSD_HEADDIM)
        xw = (x_g * _expand_heads(w_state, h0, SSD_HEADDIM)).astype(BF16)
        states = lax.dot_general(xw, bmb, TN_DIMS, preferred_element_type=F32)
        carry = jnp.concatenate(
            [jnp.broadcast_to(e_last[:, h0 + j:h0 + j + 1], (SSD_HEADDIM, D_STATE)) for j in range(SSD_HPG)], axis=0)
        h_out_ref[c0:c0 + GROUP_W, :] = carry * h_prev + states

        y = y_diag + y_off + drep_ref[:, c0:c0 + GROUP_W] * x_g
        gz = y * _silu(z[:, c0:c0 + GROUP_W])
        gz = gz * lax.rsqrt(jnp.mean(gz * gz, axis=-1, keepdims=True) + EPS) * ng_ref[:, c0:c0 + GROUP_W]
        gz_ref[:, c0:c0 + GROUP_W] = gz.astype(gz_ref.dtype)


PAD_ROWS = 8


CONV_SLABS = CONV_DIM // LANES
CONV_HIST = CONV_W - 1
CHUNKS_PER_STEP = 2


def _ssd_prompt_body(x_ref, g_ref, wp_ref, cs_ref, h0_ref, cw_ref, cb_ref, dtb_ref, alog_ref, drep_ref, ng_ref,
                     wo_ref, o_ref, nc_ref, hout_ref, xpad_ref, gz_ref):
    q = SSD_CHUNK
    rows = CHUNKS_PER_STEP * q

    @pl.when(pl.program_id(1) == 0)
    def _():
        for j in range(CONV_SLABS):
            xpad_ref[j, PAD_ROWS - CONV_HIST:PAD_ROWS, :] = cs_ref[0, :, j * LANES:(j + 1) * LANES]
        hout_ref[0] = h0_ref[0]

    x = x_ref[...]
    h = _rms(x, g_ref[...]).astype(BF16)
    z = jnp.dot(h, wp_ref[:, :D_INNER], preferred_element_type=F32)
    xbc = jnp.dot(h, wp_ref[:, D_INNER:D_INNER + CONV_DIM], preferred_element_type=F32)
    dt_raw = jnp.dot(h, wp_ref[:, D_INNER + CONV_DIM:], preferred_element_type=F32)
    for j in range(CONV_SLABS):
        xpad_ref[j, PAD_ROWS:PAD_ROWS + rows, :] = xbc[:, j * LANES:(j + 1) * LANES]
    state = hout_ref.at[0]
    for c in range(CHUNKS_PER_STEP):
        sl = slice(c * q, (c + 1) * q)
        _ssd_chunk(q, xpad_ref, PAD_ROWS + c * q, z[sl], dt_raw[sl], cw_ref, cb_ref, dtb_ref, alog_ref, drep_ref,
                   ng_ref, state, state, gz_ref.at[pl.ds(c * q, q)])
    o_ref[...] = x + jnp.dot(gz_ref[...], wo_ref[...], preferred_element_type=F32)
    for j in range(CONV_SLABS):
        tail = xpad_ref[j, PAD_ROWS + rows - CONV_HIST:PAD_ROWS + rows, :]
        nc_ref[0, :, j * LANES:(j + 1) * LANES] = tail
        xpad_ref[j, PAD_ROWS - CONV_HIST:PAD_ROWS, :] = tail


def _ssd_prompt(x, g, w_proj, conv_state, ssm_state, prm, w_out, batch, seq):
    rows = CHUNKS_PER_STEP * SSD_CHUNK
    steps = seq // rows
    d = x.shape[1]
    hp = SSD_HEADS * SSD_HEADDIM
    row = lambda b, c: (b * steps + c, 0)
    per_b = lambda b, c: (b, 0, 0)
    consts = list(prm) + [w_out]
    return pl.pallas_call(
        _ssd_prompt_body,
        out_shape=[jax.ShapeDtypeStruct(x.shape, F32),
                   jax.ShapeDtypeStruct((batch, CONV_HIST, CONV_DIM), F32),
                   jax.ShapeDtypeStruct((batch, hp, D_STATE), F32)],
        grid=(batch, steps),
        in_specs=[pl.BlockSpec((rows, d), row), _const_spec((1, d)), _const_spec(w_proj.shape),
                  pl.BlockSpec((1, CONV_HIST, CONV_DIM), per_b), pl.BlockSpec((1, hp, D_STATE), per_b)]
        + [_const_spec(c.shape) for c in consts],
        out_specs=[pl.BlockSpec((rows, d), row), pl.BlockSpec((1, CONV_HIST, CONV_DIM), per_b),
                   pl.BlockSpec((1, hp, D_STATE), per_b)],
        scratch_shapes=[pltpu.VMEM((CONV_SLABS, PAD_ROWS + rows, LANES), F32), pltpu.VMEM((rows, D_INNER), BF16)],
        compiler_params=_params("parallel", "arbitrary"),
        name="ssd_prompt",
    )(x, g.reshape(1, d), w_proj, conv_state, ssm_state, *consts)


SEQS_PER_STEP = 2


def _ssd_sample_body(z_ref, xbc_ref, dt_ref, cs_ref, h0_ref, cw_ref, cb_ref, dtb_ref, alog_ref, drep_ref, ng_ref,
                     gz_ref, nc_ref, hout_ref, xpad_ref, *, q):
    span = PAD_ROWS + q
    for e in range(SEQS_PER_STEP):
        row0 = e * span + PAD_ROWS
        sl = slice(e * q, (e + 1) * q)
        for j in range(CONV_SLABS):
            lanes = slice(j * LANES, (j + 1) * LANES)
            xpad_ref[j, row0 - CONV_HIST:row0, :] = cs_ref[e, :, lanes]
            xpad_ref[j, row0:row0 + q, :] = xbc_ref[sl, lanes]
        _ssd_chunk(q, xpad_ref, row0, z_ref[sl, :], dt_ref[sl, :], cw_ref, cb_ref, dtb_ref, alog_ref, drep_ref,
                   ng_ref, h0_ref.at[e], hout_ref.at[e], gz_ref.at[pl.ds(e * q, q)])
        for j in range(CONV_SLABS):
            nc_ref[e, :, j * LANES:(j + 1) * LANES] = xpad_ref[j, row0 + q - CONV_HIST:row0 + q, :]


def _ssd_sample(z, xbc, dt_raw, conv_state, ssm_state, prm, batch, seq):
    n = SEQS_PER_STEP
    hp = SSD_HEADS * SSD_HEADDIM
    row = lambda b: (b, 0)
    per_b = lambda b: (b, 0, 0)
    return pl.pallas_call(
        functools.partial(_ssd_sample_body, q=seq),
        out_shape=[jax.ShapeDtypeStruct((batch * seq, D_INNER), F32),
                   jax.ShapeDtypeStruct((batch, CONV_HIST, CONV_DIM), F32),
                   jax.ShapeDtypeStruct((batch, hp, D_STATE), F32)],
        grid=(batch // n,),
        in_specs=[pl.BlockSpec((n * seq, D_INNER), row), pl.BlockSpec((n * seq, CONV_DIM), row),
                  pl.BlockSpec((n * seq, LANES), row),
                  pl.BlockSpec((n, CONV_HIST, CONV_DIM), per_b), pl.BlockSpec((n, hp, D_STATE), per_b)]
        + [_const_spec(c.shape) for c in prm],
        out_specs=[pl.BlockSpec((n * seq, D_INNER), row), pl.BlockSpec((n, CONV_HIST, CONV_DIM), per_b),
                   pl.BlockSpec((n, hp, D_STATE), per_b)],
        scratch_shapes=[pltpu.VMEM((CONV_SLABS, n * (PAD_ROWS + seq), LANES), F32)],
        compiler_params=_params("parallel"),
        name="ssd_sample",
    )(z, xbc, dt_raw, conv_state, ssm_state, *prm)


ATTN_SLAB = ATTN_QBLOCK * 16
HEAD_PAIRS = ATTN_HPG // 2


def _attn_prompt_body(slope_ref, *refs):
    qkv_refs = refs[:9]
    o_ref = refs[9]
    m_sc, l_sc, acc_sc = refs[10:13]
    kv_ext = refs[13:]
    pair = pl.program_id(1)
    s_idx = pl.program_id(2)
    qb = ATTN_QBLOCK

    for g, (_, dil) in enumerate(ATTN_PATTERNS):
        hist = qb * dil
        for ext, src in zip(kv_ext[2 * g:2 * g + 2], qkv_refs[3 * g + 1:3 * g + 3]):
            @pl.when(s_idx == 0)
            def _(ext=ext, hist=hist):
                ext[0:hist, :] = jnp.zeros((hist, LANES), F32)

            ext[hist:hist + ATTN_SLAB, :] = src[...]

    row = lax.broadcasted_iota(jnp.int32, (qb, 2 * qb), 0)
    col = lax.broadcasted_iota(jnp.int32, (qb, 2 * qb), 1)
    dist = qb + row - col
    in_window = (dist >= 0) & (dist <= ATTN_NKEYS)
    valid_first = in_window & (col >= jnp.where(s_idx > 0, 0, qb))
    dist_f = dist.astype(F32)
    lane = lax.broadcasted_iota(jnp.int32, (qb, LANES), 1)
    low_half = lane < ATTN_HEAD_DIM

    for g, (_, dil) in enumerate(ATTN_PATTERNS):
        q_ref = qkv_refs[3 * g]
        k_ext, v_ext = kv_ext[2 * g:2 * g + 2]
        n_blk = ATTN_SLAB // (qb * dil)
        slopes = [slope_ref[g * ATTN_HPG + 2 * pair + e] * float(dil) for e in range(2)]

        for r, blk in [(r, blk) for r in range(dil) for blk in range(n_blk)]:
            start = r + (dil * qb) * blk
            rows = pl.ds(start, qb) if dil == 1 else pl.ds(start, qb, stride=dil)
            key_rows = pl.ds(start, 2 * qb) if dil == 1 else pl.ds(start, 2 * qb, stride=dil)
            k2 = k_ext[key_rows, :].astype(BF16)
            v2 = v_ext[key_rows, :].astype(BF16)
            q2 = q_ref[rows, :] * (ATTN_HEAD_DIM ** -0.5)
            valid = in_window if blk > 0 else valid_first
            ms, ls, pvs = [], [], []
            for e in range(2):
                qm = jnp.where(low_half if e == 0 else ~low_half, q2, 0.0).astype(BF16)
                s = lax.dot_general(qm, k2, NT_DIMS, preferred_element_type=F32) - slopes[e] * dist_f
                s = jnp.where(valid, s, NEG_INF)
                m = jnp.max(s, axis=-1, keepdims=True)
                p = jnp.exp(s - m)
                ms.append(m)
                ls.append(jnp.sum(p, axis=-1, keepdims=True))
                pvs.append(jnp.dot(p.astype(BF16), v2, preferred_element_type=F32))
            m_new = jnp.where(low_half, ms[0], ms[1])
            l_new = jnp.where(low_half, ls[0], ls[1])
            pv_new = jnp.where(low_half, pvs[0], pvs[1])
            if g > 0:
                m_old = m_sc[rows, :]
                m_tot = jnp.maximum(m_old, m_new)
                a_old = jnp.exp(m_old - m_tot)
                a_new = jnp.exp(m_new - m_tot)
                l_new = a_old * l_sc[rows, :] + a_new * l_new
                pv_new = a_old * acc_sc[rows, :] + a_new * pv_new
                m_new = m_tot
            if g == N_ATTN_GROUPS - 1:
                o_ref[rows, :] = pv_new / l_new
            else:
                m_sc[rows, :] = m_new
                l_sc[rows, :] = l_new
                acc_sc[rows, :] = pv_new

    for g, (_, dil) in enumerate(ATTN_PATTERNS):
        hist = qb * dil
        for ext in kv_ext[2 * g:2 * g + 2]:
            ext[0:hist, :] = ext[ATTN_SLAB:ATTN_SLAB + hist, :]


def _attn_prompt(qkv, slope_tab, batch, seq):
    n_slab = seq // ATTN_SLAB
    col_blocks = ATTN_WIDTH // LANES

    def spec(g, which):
        base = (3 * g + which) * col_blocks
        return pl.BlockSpec((ATTN_SLAB, LANES), lambda b, p, s: (b * n_slab + s, base + p))

    in_specs = [pl.BlockSpec(memory_space=pltpu.SMEM)]
    in_specs += [spec(g, which) for g in range(N_ATTN_GROUPS) for which in range(3)]
    return pl.pallas_call(
        _attn_prompt_body,
        out_shape=jax.ShapeDtypeStruct((batch * seq, ATTN_WIDTH), F32),
        grid=(batch, HEAD_PAIRS, n_slab),
        in_specs=in_specs,
        out_specs=pl.BlockSpec((ATTN_SLAB, LANES), lambda b, p, s: (b * n_slab + s, p)),
        scratch_shapes=[pltpu.VMEM((ATTN_SLAB, LANES), F32)] * 3 + [
            pltpu.VMEM((ATTN_QBLOCK * dil + ATTN_SLAB, LANES), F32) for _, dil in ATTN_PATTERNS for _ in range(2)],
        compiler_params=_params("parallel", "parallel", "arbitrary"),
        name="attn_prompt",
    )(slope_tab, *([qkv] * 9))


def _kv_tail_body(k_ref, v_ref, o_ref):
    o_ref[0, :ATTN_WIDTH, :] = k_ref[...].T
    o_ref[0, ATTN_WIDTH:, :] = v_ref[...].T


def _kv_tail_t(qkv, g, batch, seq, tm=512):
    win = min(ATTN_PATTERNS[g][0], seq)
    tm = min(tm, win)
    first = (seq - win) // tm
    per_b = seq // tm

    def spec(which):
        return pl.BlockSpec((tm, ATTN_WIDTH), lambda b, j: (b * per_b + first + j, 3 * g + which))

    return pl.pallas_call(
        _kv_tail_body,
        out_shape=jax.ShapeDtypeStruct((batch, 2 * ATTN_WIDTH, win), F32),
        grid=(batch, win // tm),
        in_specs=[spec(1), spec(2)],
        out_specs=pl.BlockSpec((1, 2 * ATTN_WIDTH, tm), lambda b, j: (b, 0, j)),
        compiler_params=_params("parallel", "parallel"),
        name=f"kv_tail_g{g}",
    )(qkv, qkv)


def _attn_sample_body(slope_ref, qkv_ref, c0_ref, c1_ref, c2_ref, o_ref, *, n_new):
    qkv = qkv_ref[...]
    caches = (c0_ref, c1_ref, c2_ref)
    rows = ATTN_HPG * n_new
    row_head = lax.broadcasted_iota(jnp.int32, (rows, ATTN_WIDTH), 0) // n_new
    lane_head = lax.broadcasted_iota(jnp.int32, (rows, ATTN_WIDTH), 1) // ATTN_HEAD_DIM
    diag = row_head == lane_head
    head_col = lax.broadcasted_iota(jnp.int32, (rows, 1), 0) // n_new
    qi_n = lax.broadcasted_iota(jnp.int32, (rows, n_new), 0) % n_new
    gap_n = qi_n - lax.broadcasted_iota(jnp.int32, (rows, n_new), 1)
    outs, lses = [], []
    for g, (c_ref, (win, dil)) in enumerate(zip(caches, ATTN_PATTERNS)):
        base = g * 3 * ATTN_WIDTH
        past_len = c_ref.shape[2]
        slope = jnp.zeros((rows, 1), F32)
        for h in range(ATTN_HPG):
            slope = jnp.where(head_col == h, slope_ref[g * ATTN_HPG + h], slope)
        gap_p = (past_len + lax.broadcasted_iota(jnp.int32, (rows, past_len), 0) % n_new
                 - lax.broadcasted_iota(jnp.int32, (rows, past_len), 1))
        ok_p = (gap_p <= win) & ((gap_p & (dil - 1)) == 0)
        ok_n = (gap_n >= 0) & ((gap_n & (dil - 1)) == 0)
        q = qkv[:, base:base + ATTN_WIDTH] * (ATTN_HEAD_DIM ** -0.5)
        q_bd = jnp.where(diag, jnp.concatenate([q] * ATTN_HPG, axis=0), 0.0).astype(BF16)
        k_new = qkv[:, base + ATTN_WIDTH:base + 2 * ATTN_WIDTH].astype(BF16)
        v_new = qkv[:, base + 2 * ATTN_WIDTH:base + 3 * ATTN_WIDTH].astype(BF16)
        k_t = c_ref[0, :ATTN_WIDTH, :].astype(BF16)
        v_t = c_ref[0, ATTN_WIDTH:, :].astype(BF16)
        s_p = jnp.dot(q_bd, k_t, preferred_element_type=F32) - slope * gap_p.astype(F32)
        s_p = jnp.where(ok_p, s_p, NEG_INF)
        s_n = lax.dot_general(q_bd, k_new, NT_DIMS, preferred_element_type=F32) - slope * gap_n.astype(F32)
        s_n = jnp.where(ok_n, s_n, NEG_INF)
        mx = jnp.maximum(jnp.max(s_p, axis=-1, keepdims=True), jnp.max(s_n, axis=-1, keepdims=True))
        e_p = jnp.exp(s_p - mx)
        e_n = jnp.exp(s_n - mx)
        l = jnp.sum(e_p, axis=-1, keepdims=True) + jnp.sum(e_n, axis=-1, keepdims=True)
        o = (lax.dot_general(e_p.astype(BF16), v_t, NT_DIMS, preferred_element_type=F32)
             + jnp.dot(e_n.astype(BF16), v_new, preferred_element_type=F32)) / l
        outs.append(o)
        lses.append(mx + jnp.log(l))
    top = jnp.maximum(jnp.maximum(lses[0], lses[1]), lses[2])
    es = [jnp.exp(l - top) for l in lses]
    merged = (es[0] * outs[0] + es[1] * outs[1] + es[2] * outs[2]) / (es[0] + es[1] + es[2])
    merged = jnp.where(diag, merged, 0.0).reshape(ATTN_HPG, n_new, ATTN_WIDTH)
    o_ref[...] = jnp.sum(merged, axis=0)


def _attn_sample(qkv, caches_t, slope_tab, batch, n_new):
    specs = [pl.BlockSpec((1,) + c.shape[1:], lambda b: (b, 0, 0)) for c in caches_t]
    return pl.pallas_call(
        functools.partial(_attn_sample_body, n_new=n_new),
        out_shape=jax.ShapeDtypeStruct((batch * n_new, ATTN_WIDTH), F32),
        grid=(batch,),
        in_specs=[pl.BlockSpec(memory_space=pltpu.SMEM), pl.BlockSpec((n_new, QKV_WIDTH), lambda b: (b, 0))] + specs,
        out_specs=pl.BlockSpec((n_new, ATTN_WIDTH), lambda b: (b, 0)),
        compiler_params=_params("parallel"),
        name="attn_sample",
    )(slope_tab, qkv, *caches_t)


def _alibi_slopes():
    n_heads = N_ATTN_GROUPS * ATTN_HPG
    return [2.0 ** (-8.0 * (h + 1) / n_heads) for h in range(n_heads)]


def kernel(x_prompt, x_sample, state_conv, state_ssm, cache_kv_g0, cache_kv_g1, cache_kv_g2, norm_w, w_ffn_in,
           w_ffn_out, ssm_w_in, ssm_conv_w, ssm_conv_b, ssm_dt_bias, ssm_a_log, ssm_d, ssm_norm_w, ssm_w_out,
           attn_w_qkv, attn_w_o, norm_f):
    bp, lp, d = x_prompt.shape
    bs, ls, _ = x_sample.shape
    xs_all = [x_prompt.reshape(bp * lp, d), x_sample.reshape(bs * ls, d)]
    dims = [(bp, lp), (bs, ls)]
    hist = CONV_W - 1
    hp = SSD_HEADS * SSD_HEADDIM

    w_in = w_ffn_in.astype(BF16)
    w_out = w_ffn_out.astype(BF16)
    pad = LANES - SSD_HEADS
    w_proj = jnp.pad(ssm_w_in[0], ((0, 0), (0, pad))).astype(BF16)
    ssd_prm = (ssm_conv_w[0], ssm_conv_b[0].reshape(1, CONV_DIM),
               jnp.pad(ssm_dt_bias[0], (0, pad)).reshape(1, LANES),
               jnp.pad(ssm_a_log[0], (0, pad)).reshape(1, LANES),
               jnp.repeat(ssm_d[0], SSD_HEADDIM).reshape(1, D_INNER),
               ssm_norm_w[0].reshape(1, D_INNER))
    w_ssm_out = ssm_w_out[0].astype(BF16)
    w_qkv = attn_w_qkv[0].astype(BF16)
    w_o = attn_w_o[0].astype(BF16)
    slope_tab = jnp.asarray(_alibi_slopes(), F32)

    conv_states = [jnp.zeros((bp, hist, CONV_DIM), F32), state_conv[0]]
    ssm_states = [jnp.zeros((bp, hp, D_STATE), F32), state_ssm[0].reshape(bs, hp, D_STATE)]
    caches_t = [jnp.transpose(c[0], (0, 2, 3, 4, 1)).reshape(bs, 2 * ATTN_WIDTH, c.shape[2])
                for c in (cache_kv_g0, cache_kv_g1, cache_kv_g2)]

    conv_out, ssm_out = [], []
    for n, (x, (b, l)) in enumerate(zip(xs_all, dims)):
        x = _ffn(x, norm_w[0, 0], w_in[0, 0], w_out[0, 0])
        if l % (CHUNKS_PER_STEP * SSD_CHUNK) == 0:
            x, new_conv, new_ssm = _ssd_prompt(x, norm_w[0, 1], w_proj, conv_states[n], ssm_states[n], ssd_prm,
                                               w_ssm_out, b, l)
        else:
            z, xbc, dt_raw = _norm_mm(x, norm_w[0, 1], w_proj, (D_INNER, CONV_DIM, LANES))
            gz, new_conv, new_ssm = _ssd_sample(z, xbc, dt_raw, conv_states[n], ssm_states[n], ssd_prm, b, l)
            x = _mm_res(gz, w_ssm_out, x)
        conv_out.append(new_conv[None])
        ssm_out.append(new_ssm.reshape(1, b, SSD_HEADS, SSD_HEADDIM, D_STATE))
        xs_all[n] = _ffn(x, norm_w[0, 2], w_in[0, 1], w_out[0, 1])

    kv_out = []
    for n, (x, (b, l)) in enumerate(zip(xs_all, dims)):
        x = _ffn(x, norm_w[1, 0], w_in[1, 0], w_out[1, 0])
        (qkv,) = _norm_mm(x, norm_w[1, 1], w_qkv, (QKV_WIDTH,))
        if n == 0:
            o = _attn_prompt(qkv, slope_tab, b, l)
            kv_t = [_kv_tail_t(qkv, g, b, l) for g in range(N_ATTN_GROUPS)]
            kv_out.append([jnp.transpose(t.reshape(b, 2, ATTN_HPG, ATTN_HEAD_DIM, t.shape[2]), (0, 4, 1, 2, 3))[None]
                           for t in kv_t])
        else:
            o = _attn_sample(qkv, caches_t, slope_tab, b, l)
            qkv5 = qkv.reshape(b, l, N_ATTN_GROUPS, 3, ATTN_HPG, ATTN_HEAD_DIM)
            kv_out.append([qkv5[:, :, g, 1:][None] for g in range(N_ATTN_GROUPS)])
        x = _mm_res(o, w_o, x)
        xs_all[n] = _ffn(x, norm_w[1, 2], w_in[1, 1], w_out[1, 1], g_final=norm_f)

    return (xs_all[0].reshape(bp, lp, d), xs_all[1].reshape(bs, ls, d),
            conv_out[0], conv_out[1], ssm_out[0], ssm_out[1],
            kv_out[0][0], kv_out[1][0], kv_out[0][1], kv_out[1][1], kv_out[0][2], kv_out[1][2])
```

```python
import functools

import jax
import jax.numpy as jnp
from jax import lax
from jax.experimental import pallas as pl
from jax.experimental.pallas import tpu as pltpu

F32 = jnp.float32
BF16 = jnp.bfloat16

EPS = 1e-6
NEG_INF = -1e30

D_MODEL = 1024
D_FF = 2816
D_INNER = 2048
SSD_HEADS = 32
SSD_HEADDIM = 64
SSD_GROUPS = 8
SSD_HPG = SSD_HEADS // SSD_GROUPS
GROUP_W = SSD_HPG * SSD_HEADDIM
D_STATE = 128
CONV_W = 4
CONV_DIM = D_INNER + 2 * SSD_GROUPS * D_STATE
SSD_CHUNK = 128
ATTN_PATTERNS = ((128, 1), (512, 4), (2048, 16))
N_ATTN_GROUPS = 3
ATTN_HPG = 8
ATTN_HEAD_DIM = 64
ATTN_WIDTH = ATTN_HPG * ATTN_HEAD_DIM
QKV_WIDTH = N_ATTN_GROUPS * 3 * ATTN_WIDTH
ATTN_QBLOCK = 128
ATTN_NKEYS = 128

LANES = 128
VMEM_LIMIT = 56 * 1024 * 1024

NT_DIMS = (((1,), (1,)), ((), ()))
TN_DIMS = (((0,), (0,)), ((), ()))


def _const_spec(shape):
    zeros = (0,) * len(shape)
    return pl.BlockSpec(shape, lambda *_: zeros, pipeline_mode=pl.Buffered(1))


def _params(*semantics):
    return pltpu.CompilerParams(dimension_semantics=semantics, vmem_limit_bytes=VMEM_LIMIT)


def _rms(x, g):
    return x * lax.rsqrt(jnp.mean(x * x, axis=-1, keepdims=True) + EPS) * g


def _silu(x):
    return x * jax.nn.sigmoid(x)


def _softplus(x):
    return jnp.maximum(x, 0.0) + jnp.log1p(jnp.exp(-jnp.abs(x)))


def _ffn_body(x_ref, g_ref, win_ref, wout_ref, *rest, final_norm):
    if final_norm:
        gf_ref, o_ref = rest
    else:
        (o_ref,) = rest
    x = x_ref[...]
    h = _rms(x, g_ref[...]).astype(BF16)
    a = jnp.dot(h, win_ref[:, :D_FF], preferred_element_type=F32)
    b = jnp.dot(h, win_ref[:, D_FF:], preferred_element_type=F32)
    t = (_silu(a) * b).astype(BF16)
    y = x + 0.5 * jnp.dot(t, wout_ref[...], preferred_element_type=F32)
    if final_norm:
        y = _rms(y, gf_ref[...])
    o_ref[...] = y


def _ffn(x, g, w_in, w_out, which, g_final=None, tm=512):
    m, d = x.shape
    tm = min(tm, m)
    row = pl.BlockSpec((tm, d), lambda i: (i, 0))

    def picked(w):
        return pl.BlockSpec((None, None) + w.shape[2:], lambda i: which + (0, 0), pipeline_mode=pl.Buffered(1))

    in_specs = [row, _const_spec((1, d)), picked(w_in), picked(w_out)]
    args = [x, g.reshape(1, d), w_in, w_out]
    if g_final is not None:
        in_specs.append(_const_spec((1, d)))
        args.append(g_final.reshape(1, d))
    return pl.pallas_call(
        functools.partial(_ffn_body, final_norm=g_final is not None),
        out_shape=jax.ShapeDtypeStruct((m, d), F32),
        grid=(m // tm,),
        in_specs=in_specs,
        out_specs=row,
        compiler_params=_params("parallel"),
        name="ffn",
    )(*args)


def _norm_mm_body(x_ref, g_ref, w_ref, *o_refs, splits):
    h = _rms(x_ref[...], g_ref[...]).astype(BF16)
    off = 0
    for o_ref, n in zip(o_refs, splits):
        o_ref[...] = jnp.dot(h, w_ref[:, off:off + n], preferred_element_type=F32)
        off += n


def _norm_mm(x, g, w, splits, tm=256):
    m, d = x.shape
    tm = min(tm, m)
    return pl.pallas_call(
        functools.partial(_norm_mm_body, splits=splits),
        out_shape=[jax.ShapeDtypeStruct((m, n), F32) for n in splits],
        grid=(m // tm,),
        in_specs=[pl.BlockSpec((tm, d), lambda i: (i, 0)), _const_spec((1, d)), _const_spec(w.shape)],
        out_specs=[pl.BlockSpec((tm, n), lambda i: (i, 0)) for n in splits],
        compiler_params=_params("parallel"),
        name="norm_mm",
    )(x, g.reshape(1, d), w)


def _mm_res_body(y_ref, w_ref, x_ref, o_ref):
    o_ref[...] = x_ref[...] + jnp.dot(y_ref[...].astype(BF16), w_ref[...], preferred_element_type=F32)


def _mm_res(y, w, x, tm=512):
    m, k = y.shape
    d = x.shape[1]
    tm = min(tm, m)
    return pl.pallas_call(
        _mm_res_body,
        out_shape=jax.ShapeDtypeStruct((m, d), F32),
        grid=(m // tm,),
        in_specs=[pl.BlockSpec((tm, k), lambda i: (i, 0)), _const_spec(w.shape),
                  pl.BlockSpec((tm, d), lambda i: (i, 0))],
        out_specs=pl.BlockSpec((tm, d), lambda i: (i, 0)),
        compiler_params=_params("parallel"),
        name="mm_res",
    )(y, w, x)


def _expand_heads(v, first, width):
    rows = v.shape[0]
    return jnp.concatenate(
        [jnp.broadcast_to(v[:, first + j:first + j + 1], (rows, width)) for j in range(SSD_HPG)], axis=1)


def _split_bf16(v):
    parts, rest = [], v
    for _ in range(3):
        parts.append(rest.astype(BF16))
        rest = rest - parts[-1].astype(F32)
    return jnp.concatenate(parts, axis=1)


def _ssd_chunk(q, xpad_ref, row0, z, dt_raw, cw_ref, cb_ref, dtb_ref, alog_ref, drep_ref, ng_ref,
               h_in_ref, h_out_ref, gz_ref, spread_refs=None):
    def conv(slab):
        sl = slice(slab * LANES, (slab + 1) * LANES)
        acc = cb_ref[:, sl]
        for k in range(CONV_W):
            lo = row0 - (CONV_W - 1) + k
            acc = acc + cw_ref[k:k + 1, sl] * xpad_ref[slab, lo:lo + q, :]
        return _silu(acc)

    x_slabs = D_INNER // LANES
    bc_slabs = SSD_GROUPS * D_STATE // LANES

    dt = _softplus(dt_raw + dtb_ref[...])
    a = -jnp.exp(alog_ref[...])
    row = lax.broadcasted_iota(jnp.int32, (q, q), 0)
    col = lax.broadcasted_iota(jnp.int32, (q, q), 1)
    causal = row >= col
    a_cum = jnp.dot(causal.astype(F32), dt * a, precision=lax.Precision.HIGHEST, preferred_element_type=F32)
    a_last = a_cum[q - 1:q, :]
    w_state = jnp.exp(a_last - a_cum) * dt
    e_cum = jnp.exp(a_cum)
    e_last = jnp.exp(a_last)
    a_cum_t = a_cum.T
    dt_t = dt.T
    if spread_refs is not None:
        spread64_ref, spread128_ref = spread_refs
        e_cum_x = jnp.dot(_split_bf16(e_cum), spread64_ref[...], preferred_element_type=F32)
        w_state_x = jnp.dot(_split_bf16(w_state), spread64_ref[...], preferred_element_type=F32)
        a_cum_x = jnp.dot(_split_bf16(a_cum), spread128_ref[...], preferred_element_type=F32)

    for g in range(SSD_GROUPS):
        h0 = g * SSD_HPG
        c0 = g * GROUP_W
        bmb = conv(x_slabs + g).astype(BF16)
        cmb = conv(x_slabs + bc_slabs + g).astype(BF16)
        cb = lax.dot_general(cmb, bmb, NT_DIMS, preferred_element_type=F32)
        x_g = jnp.concatenate([conv(c0 // LANES + i) for i in range(GROUP_W // LANES)], axis=1)
        y_heads = []
        for j in range(SSD_HPG):
            h = h0 + j
            a_col = a_cum[:, h:h + 1] if spread_refs is None else a_cum_x[:, h * LANES:(h + 1) * LANES]
            seg = a_col - a_cum_t[h:h + 1, :]
            decay = jnp.exp(jnp.where(causal, seg, -jnp.inf))
            wts = (cb * decay * dt_t[h:h + 1, :]).astype(BF16)
            x_h = x_g[:, j * SSD_HEADDIM:(j + 1) * SSD_HEADDIM].astype(BF16)
            y_heads.append(jnp.dot(wts, x_h, preferred_element_type=F32))
        y_diag = jnp.concatenate(y_heads, axis=1)

        h_prev = h_in_ref[c0:c0 + GROUP_W, :]
        y_off = lax.dot_general(cmb, h_prev.astype(BF16), NT_DIMS, preferred_element_type=F32)
        if spread_refs is None:
            y_off = y_off * _expand_heads(e_cum, h0, SSD_HEADDIM)
            xw = (x_g * _expand_heads(w_state, h0, SSD_HEADDIM)).astype(BF16)
        else:
            y_off = y_off * e_cum_x[:, c0:c0 + GROUP_W]
            xw = (x_g * w_state_x[:, c0:c0 + GROUP_W]).astype(BF16)
        states = lax.dot_general(xw, bmb, TN_DIMS, preferred_element_type=F32)
        carry = jnp.concatenate(
            [jnp.broadcast_to(e_last[:, h0 + j:h0 + j + 1], (SSD_HEADDIM, D_STATE)) for j in range(SSD_HPG)], axis=0)
        h_out_ref[c0:c0 + GROUP_W, :] = carry * h_prev + states

        y = y_diag + y_off + drep_ref[:, c0:c0 + GROUP_W] * x_g
        gz = y * _silu(z[:, c0:c0 + GROUP_W])
        gz = gz * lax.rsqrt(jnp.mean(gz * gz, axis=-1, keepdims=True) + EPS) * ng_ref[:, c0:c0 + GROUP_W]
        gz_ref[:, c0:c0 + GROUP_W] = gz.astype(gz_ref.dtype)


PAD_ROWS = 8


CONV_SLABS = CONV_DIM // LANES
CONV_HIST = CONV_W - 1
CHUNKS_PER_STEP = 2


def _ssd_prompt_body(x_ref, g_ref, wp_ref, cs_ref, h0_ref, cw_ref, cb_ref, dtb_ref, alog_ref, drep_ref, ng_ref,
                     wo_ref, spread64_ref, spread128_ref, o_ref, nc_ref, hout_ref, xpad_ref, gz_ref):
    q = SSD_CHUNK
    rows = CHUNKS_PER_STEP * q

    @pl.when(pl.program_id(1) == 0)
    def _():
        for j in range(CONV_SLABS):
            xpad_ref[j, PAD_ROWS - CONV_HIST:PAD_ROWS, :] = cs_ref[0, :, j * LANES:(j + 1) * LANES]
        hout_ref[0] = h0_ref[0]

    x = x_ref[...]
    h = _rms(x, g_ref[...]).astype(BF16)
    z = jnp.dot(h, wp_ref[:, :D_INNER], preferred_element_type=F32)
    xbc = jnp.dot(h, wp_ref[:, D_INNER:D_INNER + CONV_DIM], preferred_element_type=F32)
    dt_raw = jnp.dot(h, wp_ref[:, D_INNER + CONV_DIM:], preferred_element_type=F32)
    for j in range(CONV_SLABS):
        xpad_ref[j, PAD_ROWS:PAD_ROWS + rows, :] = xbc[:, j * LANES:(j + 1) * LANES]
    state = hout_ref.at[0]
    for c in range(CHUNKS_PER_STEP):
        sl = slice(c * q, (c + 1) * q)
        _ssd_chunk(q, xpad_ref, PAD_ROWS + c * q, z[sl], dt_raw[sl], cw_ref, cb_ref, dtb_ref, alog_ref, drep_ref,
                   ng_ref, state, state, gz_ref.at[pl.ds(c * q, q)], (spread64_ref, spread128_ref))
    o_ref[...] = x + jnp.dot(gz_ref[...], wo_ref[...], preferred_element_type=F32)
    for j in range(CONV_SLABS):
        tail = xpad_ref[j, PAD_ROWS + rows - CONV_HIST:PAD_ROWS + rows, :]
        nc_ref[0, :, j * LANES:(j + 1) * LANES] = tail
        xpad_ref[j, PAD_ROWS - CONV_HIST:PAD_ROWS, :] = tail


def _ssd_prompt(x, g, w_proj, conv_state, ssm_state, prm, w_out, batch, seq):
    rows = CHUNKS_PER_STEP * SSD_CHUNK
    steps = seq // rows
    d = x.shape[1]
    hp = SSD_HEADS * SSD_HEADDIM
    row = lambda b, c: (b * steps + c, 0)
    per_b = lambda b, c: (b, 0, 0)
    head_of_row = jnp.arange(3 * LANES)[:, None] % LANES
    spreads = [(head_of_row == jnp.arange(SSD_HEADS * w)[None, :] // w).astype(BF16) for w in (SSD_HEADDIM, LANES)]
    consts = list(prm) + [w_out] + spreads
    return pl.pallas_call(
        _ssd_prompt_body,
        out_shape=[jax.ShapeDtypeStruct(x.shape, F32),
                   jax.ShapeDtypeStruct((batch, CONV_HIST, CONV_DIM), F32),
                   jax.ShapeDtypeStruct((batch, hp, D_STATE), F32)],
        grid=(batch, steps),
        in_specs=[pl.BlockSpec((rows, d), row), _const_spec((1, d)), _const_spec(w_proj.shape),
                  pl.BlockSpec((1, CONV_HIST, CONV_DIM), per_b), pl.BlockSpec((1, hp, D_STATE), per_b)]
        + [_const_spec(c.shape) for c in consts],
        out_specs=[pl.BlockSpec((rows, d), row), pl.BlockSpec((1, CONV_HIST, CONV_DIM), per_b),
                   pl.BlockSpec((1, hp, D_STATE), per_b)],
        scratch_shapes=[pltpu.VMEM((CONV_SLABS, PAD_ROWS + rows, LANES), F32), pltpu.VMEM((rows, D_INNER), BF16)],
        compiler_params=_params("parallel", "arbitrary"),
        name="ssd_prompt",
    )(x, g.reshape(1, d), w_proj, conv_state, ssm_state, *consts)


SEQS_PER_STEP = 2


def _ssd_sample_body(z_ref, xbc_ref, dt_ref, cs_ref, h0_ref, cw_ref, cb_ref, dtb_ref, alog_ref, drep_ref, ng_ref,
                     gz_ref, nc_ref, hout_ref, xpad_ref, *, q):
    span = PAD_ROWS + q
    for e in range(SEQS_PER_STEP):
        row0 = e * span + PAD_ROWS
        sl = slice(e * q, (e + 1) * q)
        for j in range(CONV_SLABS):
            lanes = slice(j * LANES, (j + 1) * LANES)
            xpad_ref[j, row0 - CONV_HIST:row0, :] = cs_ref[e, :, lanes]
            xpad_ref[j, row0:row0 + q, :] = xbc_ref[sl, lanes]
        _ssd_chunk(q, xpad_ref, row0, z_ref[sl, :], dt_ref[sl, :], cw_ref, cb_ref, dtb_ref, alog_ref, drep_ref,
                   ng_ref, h0_ref.at[e], hout_ref.at[e], gz_ref.at[pl.ds(e * q, q)])
        for j in range(CONV_SLABS):
            nc_ref[e, :, j * LANES:(j + 1) * LANES] = xpad_ref[j, row0 + q - CONV_HIST:row0 + q, :]


def _ssd_sample(z, xbc, dt_raw, conv_state, ssm_state, prm, batch, seq):
    n = SEQS_PER_STEP
    hp = SSD_HEADS * SSD_HEADDIM
    row = lambda b: (b, 0)
    per_b = lambda b: (b, 0, 0)
    return pl.pallas_call(
        functools.partial(_ssd_sample_body, q=seq),
        out_shape=[jax.ShapeDtypeStruct((batch * seq, D_INNER), F32),
                   jax.ShapeDtypeStruct((batch, CONV_HIST, CONV_DIM), F32),
                   jax.ShapeDtypeStruct((batch, hp, D_STATE), F32)],
        grid=(batch // n,),
        in_specs=[pl.BlockSpec((n * seq, D_INNER), row), pl.BlockSpec((n * seq, CONV_DIM), row),
                  pl.BlockSpec((n * seq, LANES), row),
                  pl.BlockSpec((n, CONV_HIST, CONV_DIM), per_b), pl.BlockSpec((n, hp, D_STATE), per_b)]
        + [_const_spec(c.shape) for c in prm],
        out_specs=[pl.BlockSpec((n * seq, D_INNER), row), pl.BlockSpec((n, CONV_HIST, CONV_DIM), per_b),
                   pl.BlockSpec((n, hp, D_STATE), per_b)],
        scratch_shapes=[pltpu.VMEM((CONV_SLABS, n * (PAD_ROWS + seq), LANES), F32)],
        compiler_params=_params("parallel"),
        name="ssd_sample",
    )(z, xbc, dt_raw, conv_state, ssm_state, *prm)


ATTN_SLAB = ATTN_QBLOCK * 16
HEAD_PAIRS = ATTN_HPG // 2


def _attn_prompt_body(slope_ref, *refs):
    qkv_refs = refs[:9]
    o_ref = refs[9]
    m_sc, l_sc, acc_sc = refs[10:13]
    kv_ext = refs[13:]
    pair = pl.program_id(1)
    s_idx = pl.program_id(2)
    qb = ATTN_QBLOCK

    for g, (_, dil) in enumerate(ATTN_PATTERNS):
        hist = qb * dil
        for ext, src in zip(kv_ext[2 * g:2 * g + 2], qkv_refs[3 * g + 1:3 * g + 3]):
            @pl.when(s_idx == 0)
            def _(ext=ext, hist=hist):
                ext[0:hist, :] = jnp.zeros((hist, LANES), F32)

            ext[hist:hist + ATTN_SLAB, :] = src[...]

    row = lax.broadcasted_iota(jnp.int32, (qb, 2 * qb), 0)
    col = lax.broadcasted_iota(jnp.int32, (qb, 2 * qb), 1)
    dist = qb + row - col
    in_window = (dist >= 0) & (dist <= ATTN_NKEYS)
    valid_first = in_window & (col >= jnp.where(s_idx > 0, 0, qb))
    dist_f = dist.astype(F32)
    lane = lax.broadcasted_iota(jnp.int32, (qb, LANES), 1)
    low_half = lane < ATTN_HEAD_DIM

    for g, (_, dil) in enumerate(ATTN_PATTERNS):
        q_ref = qkv_refs[3 * g]
        k_ext, v_ext = kv_ext[2 * g:2 * g + 2]
        n_blk = ATTN_SLAB // (qb * dil)
        slopes = [slope_ref[g * ATTN_HPG + 2 * pair + e] * float(dil) for e in range(2)]

        for r, blk in [(r, blk) for r in range(dil) for blk in range(n_blk)]:
            start = r + (dil * qb) * blk
            rows = pl.ds(start, qb) if dil == 1 else pl.ds(start, qb, stride=dil)
            key_rows = pl.ds(start, 2 * qb) if dil == 1 else pl.ds(start, 2 * qb, stride=dil)
            k2 = k_ext[key_rows, :].astype(BF16)
            v2 = v_ext[key_rows, :].astype(BF16)
            q2 = q_ref[rows, :] * (ATTN_HEAD_DIM ** -0.5)
            valid = in_window if blk > 0 else valid_first
            ms, ls, pvs = [], [], []
            for e in range(2):
                qm = jnp.where(low_half if e == 0 else ~low_half, q2, 0.0).astype(BF16)
                s = lax.dot_general(qm, k2, NT_DIMS, preferred_element_type=F32) - slopes[e] * dist_f
                s = jnp.where(valid, s, NEG_INF)
                m = jnp.max(s, axis=-1, keepdims=True)
                p = jnp.exp(s - m)
                ms.append(m)
                ls.append(jnp.sum(p, axis=-1, keepdims=True))
                pvs.append(jnp.dot(p.astype(BF16), v2, preferred_element_type=F32))
            m_new = jnp.where(low_half, ms[0], ms[1])
            l_new = jnp.where(low_half, ls[0], ls[1])
            pv_new = jnp.where(low_half, pvs[0], pvs[1])
            if g > 0:
                m_old = m_sc[rows, :]
                m_tot = jnp.maximum(m_old, m_new)
                a_old = jnp.exp(m_old - m_tot)
                a_new = jnp.exp(m_new - m_tot)
                l_new = a_old * l_sc[rows, :] + a_new * l_new
                pv_new = a_old * acc_sc[rows, :] + a_new * pv_new
                m_new = m_tot
            if g == N_ATTN_GROUPS - 1:
                o_ref[rows, :] = pv_new / l_new
            else:
                m_sc[rows, :] = m_new
                l_sc[rows, :] = l_new
                acc_sc[rows, :] = pv_new

    for g, (_, dil) in enumerate(ATTN_PATTERNS):
        hist = qb * dil
        for ext in kv_ext[2 * g:2 * g + 2]:
            ext[0:hist, :] = ext[ATTN_SLAB:ATTN_SLAB + hist, :]


def _attn_prompt(qkv, slope_tab, batch, seq):
    n_slab = seq // ATTN_SLAB
    col_blocks = ATTN_WIDTH // LANES

    def spec(g, which):
        base = (3 * g + which) * col_blocks
        return pl.BlockSpec((ATTN_SLAB, LANES), lambda b, p, s: (b * n_slab + s, base + p))

    in_specs = [pl.BlockSpec(memory_space=pltpu.SMEM)]
    in_specs += [spec(g, which) for g in range(N_ATTN_GROUPS) for which in range(3)]
    return pl.pallas_call(
        _attn_prompt_body,
        out_shape=jax.ShapeDtypeStruct((batch * seq, ATTN_WIDTH), F32),
        grid=(batch, HEAD_PAIRS, n_slab),
        in_specs=in_specs,
        out_specs=pl.BlockSpec((ATTN_SLAB, LANES), lambda b, p, s: (b * n_slab + s, p)),
        scratch_shapes=[pltpu.VMEM((ATTN_SLAB, LANES), F32)] * 3 + [
            pltpu.VMEM((ATTN_QBLOCK * dil + ATTN_SLAB, LANES), F32) for _, dil in ATTN_PATTERNS for _ in range(2)],
        compiler_params=_params("parallel", "parallel", "arbitrary"),
        name="attn_prompt",
    )(slope_tab, *([qkv] * 9))


def _kv_tail_body(k_ref, v_ref, o_ref):
    o_ref[0, :ATTN_WIDTH, :] = k_ref[...].T
    o_ref[0, ATTN_WIDTH:, :] = v_ref[...].T


def _kv_tail_t(qkv, g, batch, seq, tm=512):
    win = min(ATTN_PATTERNS[g][0], seq)
    tm = min(tm, win)
    first = (seq - win) // tm
    per_b = seq // tm

    def spec(which):
        return pl.BlockSpec((tm, ATTN_WIDTH), lambda b, j: (b * per_b + first + j, 3 * g + which))

    return pl.pallas_call(
        _kv_tail_body,
        out_shape=jax.ShapeDtypeStruct((batch, 2 * ATTN_WIDTH, win), F32),
        grid=(batch, win // tm),
        in_specs=[spec(1), spec(2)],
        out_specs=pl.BlockSpec((1, 2 * ATTN_WIDTH, tm), lambda b, j: (b, 0, j)),
        compiler_params=_params("parallel", "parallel"),
        name=f"kv_tail_g{g}",
    )(qkv, qkv)


def _attn_sample_body(slope_ref, qkv_ref, c0_ref, c1_ref, c2_ref, o_ref, *, n_new):
    qkv = qkv_ref[...]
    caches = (c0_ref, c1_ref, c2_ref)
    rows = ATTN_HPG * n_new
    row_head = lax.broadcasted_iota(jnp.int32, (rows, ATTN_WIDTH), 0) // n_new
    lane_head = lax.broadcasted_iota(jnp.int32, (rows, ATTN_WIDTH), 1) // ATTN_HEAD_DIM
    diag = row_head == lane_head
    head_col = lax.broadcasted_iota(jnp.int32, (rows, 1), 0) // n_new
    qi_n = lax.broadcasted_iota(jnp.int32, (rows, n_new), 0) % n_new
    gap_n = qi_n - lax.broadcasted_iota(jnp.int32, (rows, n_new), 1)
    outs, lses = [], []
    for g, (c_ref, (win, dil)) in enumerate(zip(caches, ATTN_PATTERNS)):
        base = g * 3 * ATTN_WIDTH
        past_len = c_ref.shape[2]
        slope = jnp.zeros((rows, 1), F32)
        for h in range(ATTN_HPG):
            slope = jnp.where(head_col == h, slope_ref[g * ATTN_HPG + h], slope)
        gap_p = (past_len + lax.broadcasted_iota(jnp.int32, (rows, past_len), 0) % n_new
                 - lax.broadcasted_iota(jnp.int32, (rows, past_len), 1))
        ok_p = (gap_p <= win) & ((gap_p & (dil - 1)) == 0)
        ok_n = (gap_n >= 0) & ((gap_n & (dil - 1)) == 0)
        q = qkv[:, base:base + ATTN_WIDTH] * (ATTN_HEAD_DIM ** -0.5)
        q_bd = jnp.where(diag, jnp.concatenate([q] * ATTN_HPG, axis=0), 0.0).astype(BF16)
        k_new = qkv[:, base + ATTN_WIDTH:base + 2 * ATTN_WIDTH].astype(BF16)
        v_new = qkv[:, base + 2 * ATTN_WIDTH:base + 3 * ATTN_WIDTH].astype(BF16)
        k_t = c_ref[0, :ATTN_WIDTH, :].astype(BF16)
        v_t = c_ref[0, ATTN_WIDTH:, :].astype(BF16)
        s_p = jnp.dot(q_bd, k_t, preferred_element_type=F32) - slope * gap_p.astype(F32)
        s_p = jnp.where(ok_p, s_p, NEG_INF)
        s_n = lax.dot_general(q_bd, k_new, NT_DIMS, preferred_element_type=F32) - slope * gap_n.astype(F32)
        s_n = jnp.where(ok_n, s_n, NEG_INF)
        mx = jnp.maximum(jnp.max(s_p, axis=-1, keepdims=True), jnp.max(s_n, axis=-1, keepdims=True))
        e_p = jnp.exp(s_p - mx)
        e_n = jnp.exp(s_n - mx)
        l = jnp.sum(e_p, axis=-1, keepdims=True) + jnp.sum(e_n, axis=-1, keepdims=True)
        o = (lax.dot_general(e_p.astype(BF16), v_t, NT_DIMS, preferred_element_type=F32)
             + jnp.dot(e_n.astype(BF16), v_new, preferred_element_type=F32)) / l
        outs.append(o)
        lses.append(mx + jnp.log(l))
    top = jnp.maximum(jnp.maximum(lses[0], lses[1]), lses[2])
    es = [jnp.exp(l - top) for l in lses]
    merged = (es[0] * outs[0] + es[1] * outs[1] + es[2] * outs[2]) / (es[0] + es[1] + es[2])
    merged = jnp.where(diag, merged, 0.0).reshape(ATTN_HPG, n_new, ATTN_WIDTH)
    o_ref[...] = jnp.sum(merged, axis=0)


def _attn_sample(qkv, caches_t, slope_tab, batch, n_new):
    specs = [pl.BlockSpec((1,) + c.shape[1:], lambda b: (b, 0, 0)) for c in caches_t]
    return pl.pallas_call(
        functools.partial(_attn_sample_body, n_new=n_new),
        out_shape=jax.ShapeDtypeStruct((batch * n_new, ATTN_WIDTH), F32),
        grid=(batch,),
        in_specs=[pl.BlockSpec(memory_space=pltpu.SMEM), pl.BlockSpec((n_new, QKV_WIDTH), lambda b: (b, 0))] + specs,
        out_specs=pl.BlockSpec((n_new, ATTN_WIDTH), lambda b: (b, 0)),
        compiler_params=_params("parallel"),
        name="attn_sample",
    )(slope_tab, qkv, *caches_t)


KV_ROWS_SEQS = 16


def _kv_rows_body(qkv_ref, *o_refs, n_new):
    for g, o_ref in enumerate(o_refs):
        lo = (3 * g + 1) * ATTN_WIDTH
        for b in range(KV_ROWS_SEQS):
            rows = qkv_ref[b * n_new:(b + 1) * n_new, lo:lo + 2 * ATTN_WIDTH]
            o_ref[0, b] = rows.reshape(n_new, 2, ATTN_HPG, ATTN_HEAD_DIM)


def _kv_rows(qkv, batch, n_new):
    shape = (1, batch, n_new, 2, ATTN_HPG, ATTN_HEAD_DIM)
    block = (1, KV_ROWS_SEQS) + shape[2:]
    return pl.pallas_call(
        functools.partial(_kv_rows_body, n_new=n_new),
        out_shape=[jax.ShapeDtypeStruct(shape, F32)] * N_ATTN_GROUPS,
        grid=(batch // KV_ROWS_SEQS,),
        in_specs=[pl.BlockSpec((KV_ROWS_SEQS * n_new, QKV_WIDTH), lambda i: (i, 0))],
        out_specs=[pl.BlockSpec(block, lambda i: (0, i, 0, 0, 0, 0))] * N_ATTN_GROUPS,
        compiler_params=_params("parallel"),
        name="kv_rows",
    )(qkv)


def _alibi_slopes():
    n_heads = N_ATTN_GROUPS * ATTN_HPG
    return [2.0 ** (-8.0 * (h + 1) / n_heads) for h in range(n_heads)]


def kernel(x_prompt, x_sample, state_conv, state_ssm, cache_kv_g0, cache_kv_g1, cache_kv_g2, norm_w, w_ffn_in,
           w_ffn_out, ssm_w_in, ssm_conv_w, ssm_conv_b, ssm_dt_bias, ssm_a_log, ssm_d, ssm_norm_w, ssm_w_out,
           attn_w_qkv, attn_w_o, norm_f):
    bp, lp, d = x_prompt.shape
    bs, ls, _ = x_sample.shape
    xs_all = [x_prompt.reshape(bp * lp, d), x_sample.reshape(bs * ls, d)]
    dims = [(bp, lp), (bs, ls)]
    hist = CONV_W - 1
    hp = SSD_HEADS * SSD_HEADDIM

    w_in = w_ffn_in.astype(BF16)
    w_out = w_ffn_out.astype(BF16)
    pad = LANES - SSD_HEADS
    w_proj = jnp.pad(ssm_w_in[0], ((0, 0), (0, pad))).astype(BF16)
    ssd_prm = (ssm_conv_w[0], ssm_conv_b[0].reshape(1, CONV_DIM),
               jnp.pad(ssm_dt_bias[0], (0, pad)).reshape(1, LANES),
               jnp.pad(ssm_a_log[0], (0, pad)).reshape(1, LANES),
               jnp.repeat(ssm_d[0], SSD_HEADDIM).reshape(1, D_INNER),
               ssm_norm_w[0].reshape(1, D_INNER))
    w_ssm_out = ssm_w_out[0].astype(BF16)
    w_qkv = attn_w_qkv[0].astype(BF16)
    w_o = attn_w_o[0].astype(BF16)
    slope_tab = jnp.asarray(_alibi_slopes(), F32)

    conv_states = [jnp.zeros((bp, hist, CONV_DIM), F32), state_conv[0]]
    ssm_states = [jnp.zeros((bp, hp, D_STATE), F32), state_ssm[0].reshape(bs, hp, D_STATE)]
    caches_t = [jnp.transpose(c[0], (0, 2, 3, 4, 1)).reshape(bs, 2 * ATTN_WIDTH, c.shape[2])
                for c in (cache_kv_g0, cache_kv_g1, cache_kv_g2)]

    conv_out, ssm_out = [], []
    for n, (x, (b, l)) in enumerate(zip(xs_all, dims)):
        x = _ffn(x, norm_w[0, 0], w_in, w_out, (0, 0))
        if l % (CHUNKS_PER_STEP * SSD_CHUNK) == 0:
            x, new_conv, new_ssm = _ssd_prompt(x, norm_w[0, 1], w_proj, conv_states[n], ssm_states[n], ssd_prm,
                                               w_ssm_out, b, l)
        else:
            z, xbc, dt_raw = _norm_mm(x, norm_w[0, 1], w_proj, (D_INNER, CONV_DIM, LANES))
            gz, new_conv, new_ssm = _ssd_sample(z, xbc, dt_raw, conv_states[n], ssm_states[n], ssd_prm, b, l)
            x = _mm_res(gz, w_ssm_out, x)
        conv_out.append(new_conv[None])
        ssm_out.append(new_ssm.reshape(1, b, SSD_HEADS, SSD_HEADDIM, D_STATE))
        xs_all[n] = _ffn(x, norm_w[0, 2], w_in, w_out, (0, 1))

    kv_out = []
    for n, (x, (b, l)) in enumerate(zip(xs_all, dims)):
        x = _ffn(x, norm_w[1, 0], w_in, w_out, (1, 0))
        (qkv,) = _norm_mm(x, norm_w[1, 1], w_qkv, (QKV_WIDTH,))
        if n == 0:
            o = _attn_prompt(qkv, slope_tab, b, l)
            kv_t = [_kv_tail_t(qkv, g, b, l) for g in range(N_ATTN_GROUPS)]
            kv_out.append([jnp.transpose(t.reshape(b, 2, ATTN_HPG, ATTN_HEAD_DIM, t.shape[2]), (0, 4, 1, 2, 3))[None]
                           for t in kv_t])
        else:
            o = _attn_sample(qkv, caches_t, slope_tab, b, l)
            kv_out.append(_kv_rows(qkv, b, l))
        x = _mm_res(o, w_o, x)
        xs_all[n] = _ffn(x, norm_w[1, 2], w_in, w_out, (1, 1), g_final=norm_f)

    return (xs_all[0].reshape(bp, lp, d), xs_all[1].reshape(bs, ls, d),
            conv_out[0], conv_out[1], ssm_out[0], ssm_out[1],
            kv_out[0][0], kv_out[1][0], kv_out[0][1], kv_out[1][1], kv_out[0][2], kv_out[1][2])
```

```python
import functools

import jax
import jax.numpy as jnp
from jax import lax
from jax.experimental import pallas as pl
from jax.experimental.pallas import tpu as pltpu

F32 = jnp.float32
BF16 = jnp.bfloat16

EPS = 1e-6
NEG_INF = -1e30

D_MODEL = 1024
D_FF = 2816
D_INNER = 2048
SSD_HEADS = 32
SSD_HEADDIM = 64
SSD_GROUPS = 8
SSD_HPG = SSD_HEADS // SSD_GROUPS
GROUP_W = SSD_HPG * SSD_HEADDIM
D_STATE = 128
CONV_W = 4
CONV_DIM = D_INNER + 2 * SSD_GROUPS * D_STATE
SSD_CHUNK = 128
ATTN_PATTERNS = ((128, 1), (512, 4), (2048, 16))
N_ATTN_GROUPS = 3
ATTN_HPG = 8
ATTN_HEAD_DIM = 64
ATTN_WIDTH = ATTN_HPG * ATTN_HEAD_DIM
QKV_WIDTH = N_ATTN_GROUPS * 3 * ATTN_WIDTH
ATTN_QBLOCK = 128
ATTN_NKEYS = 128

LANES = 128
VMEM_LIMIT = 56 * 1024 * 1024

NT_DIMS = (((1,), (1,)), ((), ()))
TN_DIMS = (((0,), (0,)), ((), ()))


def _const_spec(shape):
    zeros = (0,) * len(shape)
    return pl.BlockSpec(shape, lambda *_: zeros, pipeline_mode=pl.Buffered(1))


def _params(*semantics):
    return pltpu.CompilerParams(dimension_semantics=semantics, vmem_limit_bytes=VMEM_LIMIT)


def _rms(x, g):
    return x * lax.rsqrt(jnp.mean(x * x, axis=-1, keepdims=True) + EPS) * g


def _silu(x):
    return x * jax.nn.sigmoid(x)


def _softplus(x):
    return jnp.maximum(x, 0.0) + jnp.log1p(jnp.exp(-jnp.abs(x)))


def _ffn_body(x_ref, g_ref, win_ref, wout_ref, *rest, final_norm):
    if final_norm:
        gf_ref, o_ref = rest
    else:
        (o_ref,) = rest
    x = x_ref[...]
    h = _rms(x, g_ref[...]).astype(BF16)
    a = jnp.dot(h, win_ref[:, :D_FF], preferred_element_type=F32)
    b = jnp.dot(h, win_ref[:, D_FF:], preferred_element_type=F32)
    t = (_silu(a) * b).astype(BF16)
    y = x + 0.5 * jnp.dot(t, wout_ref[...], preferred_element_type=F32)
    if final_norm:
        y = _rms(y, gf_ref[...])
    o_ref[...] = y


def _ffn(x, g, w_in, w_out, which, g_final=None, tm=512):
    m, d = x.shape
    tm = min(tm, m)
    row = pl.BlockSpec((tm, d), lambda i: (i, 0))

    def picked(w):
        return pl.BlockSpec((None, None) + w.shape[2:], lambda i: which + (0, 0), pipeline_mode=pl.Buffered(1))

    in_specs = [row, _const_spec((1, d)), picked(w_in), picked(w_out)]
    args = [x, g.reshape(1, d), w_in, w_out]
    if g_final is not None:
        in_specs.append(_const_spec((1, d)))
        args.append(g_final.reshape(1, d))
    return pl.pallas_call(
        functools.partial(_ffn_body, final_norm=g_final is not None),
        out_shape=jax.ShapeDtypeStruct((m, d), F32),
        grid=(m // tm,),
        in_specs=in_specs,
        out_specs=row,
        compiler_params=_params("parallel"),
        name="ffn",
    )(*args)


def _norm_mm_body(x_ref, g_ref, w_ref, *o_refs, splits):
    h = _rms(x_ref[...], g_ref[...]).astype(BF16)
    off = 0
    for o_ref, n in zip(o_refs, splits):
        o_ref[...] = jnp.dot(h, w_ref[:, off:off + n], preferred_element_type=F32)
        off += n


def _norm_mm(x, g, w, splits, tm=256):
    m, d = x.shape
    tm = min(tm, m)
    return pl.pallas_call(
        functools.partial(_norm_mm_body, splits=splits),
        out_shape=[jax.ShapeDtypeStruct((m, n), F32) for n in splits],
        grid=(m // tm,),
        in_specs=[pl.BlockSpec((tm, d), lambda i: (i, 0)), _const_spec((1, d)), _const_spec(w.shape)],
        out_specs=[pl.BlockSpec((tm, n), lambda i: (i, 0)) for n in splits],
        compiler_params=_params("parallel"),
        name="norm_mm",
    )(x, g.reshape(1, d), w)


def _mm_res_body(y_ref, w_ref, x_ref, o_ref):
    o_ref[...] = x_ref[...] + jnp.dot(y_ref[...].astype(BF16), w_ref[...], preferred_element_type=F32)


def _mm_res(y, w, x, tm=512):
    m, k = y.shape
    d = x.shape[1]
    tm = min(tm, m)
    return pl.pallas_call(
        _mm_res_body,
        out_shape=jax.ShapeDtypeStruct((m, d), F32),
        grid=(m // tm,),
        in_specs=[pl.BlockSpec((tm, k), lambda i: (i, 0)), _const_spec(w.shape),
                  pl.BlockSpec((tm, d), lambda i: (i, 0))],
        out_specs=pl.BlockSpec((tm, d), lambda i: (i, 0)),
        compiler_params=_params("parallel"),
        name="mm_res",
    )(y, w, x)


def _expand_heads(v, first, width):
    rows = v.shape[0]
    return jnp.concatenate(
        [jnp.broadcast_to(v[:, first + j:first + j + 1], (rows, width)) for j in range(SSD_HPG)], axis=1)


def _split_bf16(v):
    parts, rest = [], v
    for _ in range(3):
        parts.append(rest.astype(BF16))
        rest = rest - parts[-1].astype(F32)
    return jnp.concatenate(parts, axis=1)


def _ssd_chunk(q, xpad_ref, row0, z, dt_raw, cw_ref, cb_ref, dtb_ref, alog_ref, drep_ref, ng_ref,
               h_in_ref, h_out_ref, gz_ref, spread_refs=None):
    def conv(slab):
        sl = slice(slab * LANES, (slab + 1) * LANES)
        acc = cb_ref[:, sl]
        for k in range(CONV_W):
            lo = row0 - (CONV_W - 1) + k
            acc = acc + cw_ref[k:k + 1, sl] * xpad_ref[slab, lo:lo + q, :]
        return _silu(acc)

    x_slabs = D_INNER // LANES
    bc_slabs = SSD_GROUPS * D_STATE // LANES

    dt = _softplus(dt_raw + dtb_ref[...])
    a = -jnp.exp(alog_ref[...])
    row = lax.broadcasted_iota(jnp.int32, (q, q), 0)
    col = lax.broadcasted_iota(jnp.int32, (q, q), 1)
    causal = row >= col
    a_cum = jnp.dot(causal.astype(F32), dt * a, precision=lax.Precision.HIGHEST, preferred_element_type=F32)
    a_last = a_cum[q - 1:q, :]
    w_state = jnp.exp(a_last - a_cum) * dt
    e_cum = jnp.exp(a_cum)
    e_last = jnp.exp(a_last)
    a_cum_t = a_cum.T
    dt_t = dt.T
    if spread_refs is not None:
        spread64_ref, spread128_ref = spread_refs
        lane_head = lax.broadcasted_iota(jnp.int32, (q, GROUP_W), 1) // SSD_HEADDIM
        e_cum_x = jnp.dot(_split_bf16(e_cum), spread64_ref[...], preferred_element_type=F32)
        w_state_x = jnp.dot(_split_bf16(w_state), spread64_ref[...], preferred_element_type=F32)
        a_cum_x = jnp.dot(_split_bf16(a_cum), spread128_ref[...], preferred_element_type=F32)

    for g in range(SSD_GROUPS):
        h0 = g * SSD_HPG
        c0 = g * GROUP_W
        bmb = conv(x_slabs + g).astype(BF16)
        cmb = conv(x_slabs + bc_slabs + g).astype(BF16)
        cb = lax.dot_general(cmb, bmb, NT_DIMS, preferred_element_type=F32)
        x_g = jnp.concatenate([conv(c0 // LANES + i) for i in range(GROUP_W // LANES)], axis=1)
        wts = []
        for j in range(SSD_HPG):
            h = h0 + j
            a_col = a_cum[:, h:h + 1] if spread_refs is None else a_cum_x[:, h * LANES:(h + 1) * LANES]
            seg = a_col - a_cum_t[h:h + 1, :]
            decay = jnp.exp(jnp.where(causal, seg, -jnp.inf))
            wts.append((cb * decay * dt_t[h:h + 1, :]).astype(BF16))
        if spread_refs is None:
            y_diag = jnp.concatenate(
                [jnp.dot(w, x_g[:, j * SSD_HEADDIM:(j + 1) * SSD_HEADDIM].astype(BF16), preferred_element_type=F32)
                 for j, w in enumerate(wts)], axis=1)
        else:
            x_heads = jnp.concatenate([jnp.where(lane_head == j, x_g, 0.0) for j in range(SSD_HPG)], axis=0)
            y_diag = jnp.dot(jnp.concatenate(wts, axis=1), x_heads.astype(BF16), preferred_element_type=F32)

        h_prev = h_in_ref[c0:c0 + GROUP_W, :]
        y_off = lax.dot_general(cmb, h_prev.astype(BF16), NT_DIMS, preferred_element_type=F32)
        if spread_refs is None:
            y_off = y_off * _expand_heads(e_cum, h0, SSD_HEADDIM)
            xw = (x_g * _expand_heads(w_state, h0, SSD_HEADDIM)).astype(BF16)
        else:
            y_off = y_off * e_cum_x[:, c0:c0 + GROUP_W]
            xw = (x_g * w_state_x[:, c0:c0 + GROUP_W]).astype(BF16)
        states = lax.dot_general(xw, bmb, TN_DIMS, preferred_element_type=F32)
        carry = jnp.concatenate(
            [jnp.broadcast_to(e_last[:, h0 + j:h0 + j + 1], (SSD_HEADDIM, D_STATE)) for j in range(SSD_HPG)], axis=0)
        h_out_ref[c0:c0 + GROUP_W, :] = carry * h_prev + states

        y = y_diag + y_off + drep_ref[:, c0:c0 + GROUP_W] * x_g
        gz = y * _silu(z[:, c0:c0 + GROUP_W])
        gz = gz * lax.rsqrt(jnp.mean(gz * gz, axis=-1, keepdims=True) + EPS) * ng_ref[:, c0:c0 + GROUP_W]
        gz_ref[:, c0:c0 + GROUP_W] = gz.astype(gz_ref.dtype)


PAD_ROWS = 8


CONV_SLABS = CONV_DIM // LANES
CONV_HIST = CONV_W - 1
CHUNKS_PER_STEP = 2


def _ssd_prompt_body(x_ref, g_ref, wp_ref, cs_ref, h0_ref, cw_ref, cb_ref, dtb_ref, alog_ref, drep_ref, ng_ref,
                     wo_ref, spread64_ref, spread128_ref, o_ref, nc_ref, hout_ref, xpad_ref, gz_ref):
    q = SSD_CHUNK
    rows = CHUNKS_PER_STEP * q

    @pl.when(pl.program_id(1) == 0)
    def _():
        for j in range(CONV_SLABS):
            xpad_ref[j, PAD_ROWS - CONV_HIST:PAD_ROWS, :] = cs_ref[0, :, j * LANES:(j + 1) * LANES]
        hout_ref[0] = h0_ref[0]

    x = x_ref[...]
    h = _rms(x, g_ref[...]).astype(BF16)
    z = jnp.dot(h, wp_ref[:, :D_INNER], preferred_element_type=F32)
    xbc = jnp.dot(h, wp_ref[:, D_INNER:D_INNER + CONV_DIM], preferred_element_type=F32)
    dt_raw = jnp.dot(h, wp_ref[:, D_INNER + CONV_DIM:], preferred_element_type=F32)
    for j in range(CONV_SLABS):
        xpad_ref[j, PAD_ROWS:PAD_ROWS + rows, :] = xbc[:, j * LANES:(j + 1) * LANES]
    state = hout_ref.at[0]
    for c in range(CHUNKS_PER_STEP):
        sl = slice(c * q, (c + 1) * q)
        _ssd_chunk(q, xpad_ref, PAD_ROWS + c * q, z[sl], dt_raw[sl], cw_ref, cb_ref, dtb_ref, alog_ref, drep_ref,
                   ng_ref, state, state, gz_ref.at[pl.ds(c * q, q)], (spread64_ref, spread128_ref))
    o_ref[...] = x + jnp.dot(gz_ref[...], wo_ref[...], preferred_element_type=F32)
    for j in range(CONV_SLABS):
        tail = xpad_ref[j, PAD_ROWS + rows - CONV_HIST:PAD_ROWS + rows, :]
        nc_ref[0, :, j * LANES:(j + 1) * LANES] = tail
        xpad_ref[j, PAD_ROWS - CONV_HIST:PAD_ROWS, :] = tail


def _ssd_prompt(x, g, w_proj, conv_state, ssm_state, prm, w_out, batch, seq):
    rows = CHUNKS_PER_STEP * SSD_CHUNK
    steps = seq // rows
    d = x.shape[1]
    hp = SSD_HEADS * SSD_HEADDIM
    row = lambda b, c: (b * steps + c, 0)
    per_b = lambda b, c: (b, 0, 0)
    head_of_row = jnp.arange(3 * LANES)[:, None] % LANES
    spreads = [(head_of_row == jnp.arange(SSD_HEADS * w)[None, :] // w).astype(BF16) for w in (SSD_HEADDIM, LANES)]
    consts = list(prm) + [w_out] + spreads
    return pl.pallas_call(
        _ssd_prompt_body,
        out_shape=[jax.ShapeDtypeStruct(x.shape, F32),
                   jax.ShapeDtypeStruct((batch, CONV_HIST, CONV_DIM), F32),
                   jax.ShapeDtypeStruct((batch, hp, D_STATE), F32)],
        grid=(batch, steps),
        in_specs=[pl.BlockSpec((rows, d), row), _const_spec((1, d)), _const_spec(w_proj.shape),
                  pl.BlockSpec((1, CONV_HIST, CONV_DIM), per_b), pl.BlockSpec((1, hp, D_STATE), per_b)]
        + [_const_spec(c.shape) for c in consts],
        out_specs=[pl.BlockSpec((rows, d), row), pl.BlockSpec((1, CONV_HIST, CONV_DIM), per_b),
                   pl.BlockSpec((1, hp, D_STATE), per_b)],
        scratch_shapes=[pltpu.VMEM((CONV_SLABS, PAD_ROWS + rows, LANES), F32), pltpu.VMEM((rows, D_INNER), BF16)],
        compiler_params=_params("parallel", "arbitrary"),
        name="ssd_prompt",
    )(x, g.reshape(1, d), w_proj, conv_state, ssm_state, *consts)


SEQS_PER_STEP = 2


def _ssd_sample_body(z_ref, xbc_ref, dt_ref, cs_ref, h0_ref, cw_ref, cb_ref, dtb_ref, alog_ref, drep_ref, ng_ref,
                     gz_ref, nc_ref, hout_ref, xpad_ref, *, q):
    span = PAD_ROWS + q
    for e in range(SEQS_PER_STEP):
        row0 = e * span + PAD_ROWS
        sl = slice(e * q, (e + 1) * q)
        for j in range(CONV_SLABS):
            lanes = slice(j * LANES, (j + 1) * LANES)
            xpad_ref[j, row0 - CONV_HIST:row0, :] = cs_ref[e, :, lanes]
            xpad_ref[j, row0:row0 + q, :] = xbc_ref[sl, lanes]
        _ssd_chunk(q, xpad_ref, row0, z_ref[sl, :], dt_ref[sl, :], cw_ref, cb_ref, dtb_ref, alog_ref, drep_ref,
                   ng_ref, h0_ref.at[e], hout_ref.at[e], gz_ref.at[pl.ds(e * q, q)])
        for j in range(CONV_SLABS):
            nc_ref[e, :, j * LANES:(j + 1) * LANES] = xpad_ref[j, row0 + q - CONV_HIST:row0 + q, :]


def _ssd_sample(z, xbc, dt_raw, conv_state, ssm_state, prm, batch, seq):
    n = SEQS_PER_STEP
    hp = SSD_HEADS * SSD_HEADDIM
    row = lambda b: (b, 0)
    per_b = lambda b: (b, 0, 0)
    return pl.pallas_call(
        functools.partial(_ssd_sample_body, q=seq),
        out_shape=[jax.ShapeDtypeStruct((batch * seq, D_INNER), F32),
                   jax.ShapeDtypeStruct((batch, CONV_HIST, CONV_DIM), F32),
                   jax.ShapeDtypeStruct((batch, hp, D_STATE), F32)],
        grid=(batch // n,),
        in_specs=[pl.BlockSpec((n * seq, D_INNER), row), pl.BlockSpec((n * seq, CONV_DIM), row),
                  pl.BlockSpec((n * seq, LANES), row),
                  pl.BlockSpec((n, CONV_HIST, CONV_DIM), per_b), pl.BlockSpec((n, hp, D_STATE), per_b)]
        + [_const_spec(c.shape) for c in prm],
        out_specs=[pl.BlockSpec((n * seq, D_INNER), row), pl.BlockSpec((n, CONV_HIST, CONV_DIM), per_b),
                   pl.BlockSpec((n, hp, D_STATE), per_b)],
        scratch_shapes=[pltpu.VMEM((CONV_SLABS, n * (PAD_ROWS + seq), LANES), F32)],
        compiler_params=_params("parallel"),
        name="ssd_sample",
    )(z, xbc, dt_raw, conv_state, ssm_state, *prm)


ATTN_SLAB = ATTN_QBLOCK * 16
HEAD_PAIRS = ATTN_HPG // 2


def _attn_prompt_body(slope_ref, *refs):
    qkv_refs = refs[:9]
    o_ref = refs[9]
    m_sc, l_sc, acc_sc = refs[10:13]
    kv_ext = refs[13:]
    pair = pl.program_id(1)
    s_idx = pl.program_id(2)
    qb = ATTN_QBLOCK

    for g, (_, dil) in enumerate(ATTN_PATTERNS):
        hist = qb * dil
        for ext, src in zip(kv_ext[2 * g:2 * g + 2], qkv_refs[3 * g + 1:3 * g + 3]):
            @pl.when(s_idx == 0)
            def _(ext=ext, hist=hist):
                ext[0:hist, :] = jnp.zeros((hist, LANES), F32)

            ext[hist:hist + ATTN_SLAB, :] = src[...]

    row = lax.broadcasted_iota(jnp.int32, (qb, 2 * qb), 0)
    col = lax.broadcasted_iota(jnp.int32, (qb, 2 * qb), 1)
    dist = qb + row - col
    in_window = (dist >= 0) & (dist <= ATTN_NKEYS)
    valid_first = in_window & (col >= jnp.where(s_idx > 0, 0, qb))
    dist_f = dist.astype(F32)
    lane = lax.broadcasted_iota(jnp.int32, (qb, LANES), 1)
    low_half = lane < ATTN_HEAD_DIM

    for g, (_, dil) in enumerate(ATTN_PATTERNS):
        q_ref = qkv_refs[3 * g]
        k_ext, v_ext = kv_ext[2 * g:2 * g + 2]
        n_blk = ATTN_SLAB // (qb * dil)
        slopes = [slope_ref[g * ATTN_HPG + 2 * pair + e] * float(dil) for e in range(2)]

        for r, blk in [(r, blk) for r in range(dil) for blk in range(n_blk)]:
            start = r + (dil * qb) * blk
            rows = pl.ds(start, qb) if dil == 1 else pl.ds(start, qb, stride=dil)
            key_rows = pl.ds(start, 2 * qb) if dil == 1 else pl.ds(start, 2 * qb, stride=dil)
            k2 = k_ext[key_rows, :].astype(BF16)
            v2 = v_ext[key_rows, :].astype(BF16)
            q2 = q_ref[rows, :] * (ATTN_HEAD_DIM ** -0.5)
            valid = in_window if blk > 0 else valid_first
            ms, ls, pvs = [], [], []
            for e in range(2):
                qm = jnp.where(low_half if e == 0 else ~low_half, q2, 0.0).astype(BF16)
                s = lax.dot_general(qm, k2, NT_DIMS, preferred_element_type=F32) - slopes[e] * dist_f
                s = jnp.where(valid, s, NEG_INF)
                m = jnp.max(s, axis=-1, keepdims=True)
                p = jnp.exp(s - m)
                ms.append(m)
                ls.append(jnp.sum(p, axis=-1, keepdims=True))
                pvs.append(jnp.dot(p.astype(BF16), v2, preferred_element_type=F32))
            m_new = jnp.where(low_half, ms[0], ms[1])
            l_new = jnp.where(low_half, ls[0], ls[1])
            pv_new = jnp.where(low_half, pvs[0], pvs[1])
            if g > 0:
                m_old = m_sc[rows, :]
                m_tot = jnp.maximum(m_old, m_new)
                a_old = jnp.exp(m_old - m_tot)
                a_new = jnp.exp(m_new - m_tot)
                l_new = a_old * l_sc[rows, :] + a_new * l_new
                pv_new = a_old * acc_sc[rows, :] + a_new * pv_new
                m_new = m_tot
            if g == N_ATTN_GROUPS - 1:
                o_ref[rows, :] = pv_new / l_new
            else:
                m_sc[rows, :] = m_new
                l_sc[rows, :] = l_new
                acc_sc[rows, :] = pv_new

    for g, (_, dil) in enumerate(ATTN_PATTERNS):
        hist = qb * dil
        for ext in kv_ext[2 * g:2 * g + 2]:
            ext[0:hist, :] = ext[ATTN_SLAB:ATTN_SLAB + hist, :]


def _attn_prompt(qkv, slope_tab, batch, seq):
    n_slab = seq // ATTN_SLAB
    col_blocks = ATTN_WIDTH // LANES

    def spec(g, which):
        base = (3 * g + which) * col_blocks
        return pl.BlockSpec((ATTN_SLAB, LANES), lambda b, p, s: (b * n_slab + s, base + p))

    in_specs = [pl.BlockSpec(memory_space=pltpu.SMEM)]
    in_specs += [spec(g, which) for g in range(N_ATTN_GROUPS) for which in range(3)]
    return pl.pallas_call(
        _attn_prompt_body,
        out_shape=jax.ShapeDtypeStruct((batch * seq, ATTN_WIDTH), F32),
        grid=(batch, HEAD_PAIRS, n_slab),
        in_specs=in_specs,
        out_specs=pl.BlockSpec((ATTN_SLAB, LANES), lambda b, p, s: (b * n_slab + s, p)),
        scratch_shapes=[pltpu.VMEM((ATTN_SLAB, LANES), F32)] * 3 + [
            pltpu.VMEM((ATTN_QBLOCK * dil + ATTN_SLAB, LANES), F32) for _, dil in ATTN_PATTERNS for _ in range(2)],
        compiler_params=_params("parallel", "parallel", "arbitrary"),
        name="attn_prompt",
    )(slope_tab, *([qkv] * 9))


def _kv_tail_body(k_ref, v_ref, o_ref):
    o_ref[0, :ATTN_WIDTH, :] = k_ref[...].T
    o_ref[0, ATTN_WIDTH:, :] = v_ref[...].T


def _kv_tail_t(qkv, g, batch, seq, tm=512):
    win = min(ATTN_PATTERNS[g][0], seq)
    tm = min(tm, win)
    first = (seq - win) // tm
    per_b = seq // tm

    def spec(which):
        return pl.BlockSpec((tm, ATTN_WIDTH), lambda b, j: (b * per_b + first + j, 3 * g + which))

    return pl.pallas_call(
        _kv_tail_body,
        out_shape=jax.ShapeDtypeStruct((batch, 2 * ATTN_WIDTH, win), F32),
        grid=(batch, win // tm),
        in_specs=[spec(1), spec(2)],
        out_specs=pl.BlockSpec((1, 2 * ATTN_WIDTH, tm), lambda b, j: (b, 0, j)),
        compiler_params=_params("parallel", "parallel"),
        name=f"kv_tail_g{g}",
    )(qkv, qkv)


SAMPLE_SEQS_PER_STEP = 2


def _attn_sample_body(slope_ref, qkv_ref, c0_ref, c1_ref, c2_ref, o_ref, *, n_new):
    caches = (c0_ref, c1_ref, c2_ref)
    rows = ATTN_HPG * n_new
    row_head = lax.broadcasted_iota(jnp.int32, (rows, ATTN_WIDTH), 0) // n_new
    lane_head = lax.broadcasted_iota(jnp.int32, (rows, ATTN_WIDTH), 1) // ATTN_HEAD_DIM
    diag = row_head == lane_head
    head_col = lax.broadcasted_iota(jnp.int32, (rows, 1), 0) // n_new
    qi_n = lax.broadcasted_iota(jnp.int32, (rows, n_new), 0) % n_new
    gap_n = qi_n - lax.broadcasted_iota(jnp.int32, (rows, n_new), 1)
    geometry = []
    for g, (c_ref, (win, dil)) in enumerate(zip(caches, ATTN_PATTERNS)):
        past_len = c_ref.shape[2]
        slope = jnp.zeros((rows, 1), F32)
        for h in range(ATTN_HPG):
            slope = jnp.where(head_col == h, slope_ref[g * ATTN_HPG + h], slope)
        gap_p = (past_len + lax.broadcasted_iota(jnp.int32, (rows, past_len), 0) % n_new
                 - lax.broadcasted_iota(jnp.int32, (rows, past_len), 1))
        ok_p = (gap_p <= win) & ((gap_p & (dil - 1)) == 0)
        ok_n = (gap_n >= 0) & ((gap_n & (dil - 1)) == 0)
        geometry.append((ok_p, slope * gap_p.astype(F32), ok_n, slope * gap_n.astype(F32)))

    for e in range(SAMPLE_SEQS_PER_STEP):
        qkv = qkv_ref[e * n_new:(e + 1) * n_new, :]
        outs, lses = [], []
        for g, c_ref in enumerate(caches):
            base = g * 3 * ATTN_WIDTH
            ok_p, bias_p, ok_n, bias_n = geometry[g]
            q = qkv[:, base:base + ATTN_WIDTH] * (ATTN_HEAD_DIM ** -0.5)
            q_bd = jnp.where(diag, jnp.concatenate([q] * ATTN_HPG, axis=0), 0.0).astype(BF16)
            k_new = qkv[:, base + ATTN_WIDTH:base + 2 * ATTN_WIDTH].astype(BF16)
            v_new = qkv[:, base + 2 * ATTN_WIDTH:base + 3 * ATTN_WIDTH].astype(BF16)
            k_t = c_ref[e, :ATTN_WIDTH, :].astype(BF16)
            v_t = c_ref[e, ATTN_WIDTH:, :].astype(BF16)
            s_p = jnp.dot(q_bd, k_t, preferred_element_type=F32) - bias_p
            s_p = jnp.where(ok_p, s_p, NEG_INF)
            s_n = lax.dot_general(q_bd, k_new, NT_DIMS, preferred_element_type=F32) - bias_n
            s_n = jnp.where(ok_n, s_n, NEG_INF)
            mx = jnp.maximum(jnp.max(s_p, axis=-1, keepdims=True), jnp.max(s_n, axis=-1, keepdims=True))
            e_p = jnp.exp(s_p - mx)
            e_n = jnp.exp(s_n - mx)
            l = jnp.sum(e_p, axis=-1, keepdims=True) + jnp.sum(e_n, axis=-1, keepdims=True)
            o = (lax.dot_general(e_p.astype(BF16), v_t, NT_DIMS, preferred_element_type=F32)
                 + jnp.dot(e_n.astype(BF16), v_new, preferred_element_type=F32)) / l
            outs.append(o)
            lses.append(mx + jnp.log(l))
        top = jnp.maximum(jnp.maximum(lses[0], lses[1]), lses[2])
        es = [jnp.exp(l - top) for l in lses]
        merged = (es[0] * outs[0] + es[1] * outs[1] + es[2] * outs[2]) / (es[0] + es[1] + es[2])
        merged = jnp.where(diag, merged, 0.0).reshape(ATTN_HPG, n_new, ATTN_WIDTH)
        o_ref[e * n_new:(e + 1) * n_new, :] = jnp.sum(merged, axis=0)


def _attn_sample(qkv, caches_t, slope_tab, batch, n_new):
    n = SAMPLE_SEQS_PER_STEP
    specs = [pl.BlockSpec((n,) + c.shape[1:], lambda b: (b, 0, 0)) for c in caches_t]
    return pl.pallas_call(
        functools.partial(_attn_sample_body, n_new=n_new),
        out_shape=jax.ShapeDtypeStruct((batch * n_new, ATTN_WIDTH), F32),
        grid=(batch // n,),
        in_specs=[pl.BlockSpec(memory_space=pltpu.SMEM),
                  pl.BlockSpec((n * n_new, QKV_WIDTH), lambda b: (b, 0))] + specs,
        out_specs=pl.BlockSpec((n * n_new, ATTN_WIDTH), lambda b: (b, 0)),
        compiler_params=_params("parallel"),
        name="attn_sample",
    )(slope_tab, qkv, *caches_t)


KV_ROWS_SEQS = 16


def _kv_rows_body(qkv_ref, *o_refs, n_new):
    for g, o_ref in enumerate(o_refs):
        lo = (3 * g + 1) * ATTN_WIDTH
        for b in range(KV_ROWS_SEQS):
            rows = qkv_ref[b * n_new:(b + 1) * n_new, lo:lo + 2 * ATTN_WIDTH]
            o_ref[0, b] = rows.reshape(n_new, 2, ATTN_HPG, ATTN_HEAD_DIM)


def _kv_rows(qkv, batch, n_new):
    shape = (1, batch, n_new, 2, ATTN_HPG, ATTN_HEAD_DIM)
    block = (1, KV_ROWS_SEQS) + shape[2:]
    return pl.pallas_call(
        functools.partial(_kv_rows_body, n_new=n_new),
        out_shape=[jax.ShapeDtypeStruct(shape, F32)] * N_ATTN_GROUPS,
        grid=(batch // KV_ROWS_SEQS,),
        in_specs=[pl.BlockSpec((KV_ROWS_SEQS * n_new, QKV_WIDTH), lambda i: (i, 0))],
        out_specs=[pl.BlockSpec(block, lambda i: (0, i, 0, 0, 0, 0))] * N_ATTN_GROUPS,
        compiler_params=_params("parallel"),
        name="kv_rows",
    )(qkv)


def _alibi_slopes():
    n_heads = N_ATTN_GROUPS * ATTN_HPG
    return [2.0 ** (-8.0 * (h + 1) / n_heads) for h in range(n_heads)]


def kernel(x_prompt, x_sample, state_conv, state_ssm, cache_kv_g0, cache_kv_g1, cache_kv_g2, norm_w, w_ffn_in,
           w_ffn_out, ssm_w_in, ssm_conv_w, ssm_conv_b, ssm_dt_bias, ssm_a_log, ssm_d, ssm_norm_w, ssm_w_out,
           attn_w_qkv, attn_w_o, norm_f):
    bp, lp, d = x_prompt.shape
    bs, ls, _ = x_sample.shape
    xs_all = [x_prompt.reshape(bp * lp, d), x_sample.reshape(bs * ls, d)]
    dims = [(bp, lp), (bs, ls)]
    hist = CONV_W - 1
    hp = SSD_HEADS * SSD_HEADDIM

    w_in = w_ffn_in.astype(BF16)
    w_out = w_ffn_out.astype(BF16)
    pad = LANES - SSD_HEADS
    w_proj = jnp.pad(ssm_w_in[0], ((0, 0), (0, pad))).astype(BF16)
    ssd_prm = (ssm_conv_w[0], ssm_conv_b[0].reshape(1, CONV_DIM),
               jnp.pad(ssm_dt_bias[0], (0, pad)).reshape(1, LANES),
               jnp.pad(ssm_a_log[0], (0, pad)).reshape(1, LANES),
               jnp.repeat(ssm_d[0], SSD_HEADDIM).reshape(1, D_INNER),
               ssm_norm_w[0].reshape(1, D_INNER))
    w_ssm_out = ssm_w_out[0].astype(BF16)
    w_qkv = attn_w_qkv[0].astype(BF16)
    w_o = attn_w_o[0].astype(BF16)
    slope_tab = jnp.asarray(_alibi_slopes(), F32)

    conv_states = [jnp.zeros((bp, hist, CONV_DIM), F32), state_conv[0]]
    ssm_states = [jnp.zeros((bp, hp, D_STATE), F32), state_ssm[0].reshape(bs, hp, D_STATE)]
    caches_t = [jnp.transpose(c[0], (0, 2, 3, 4, 1)).reshape(bs, 2 * ATTN_WIDTH, c.shape[2])
                for c in (cache_kv_g0, cache_kv_g1, cache_kv_g2)]

    conv_out, ssm_out = [], []
    for n, (x, (b, l)) in enumerate(zip(xs_all, dims)):
        x = _ffn(x, norm_w[0, 0], w_in, w_out, (0, 0))
        if l % (CHUNKS_PER_STEP * SSD_CHUNK) == 0:
            x, new_conv, new_ssm = _ssd_prompt(x, norm_w[0, 1], w_proj, conv_states[n], ssm_states[n], ssd_prm,
                                               w_ssm_out, b, l)
        else:
            z, xbc, dt_raw = _norm_mm(x, norm_w[0, 1], w_proj, (D_INNER, CONV_DIM, LANES))
            gz, new_conv, new_ssm = _ssd_sample(z, xbc, dt_raw, conv_states[n], ssm_states[n], ssd_prm, b, l)
            x = _mm_res(gz, w_ssm_out, x)
        conv_out.append(new_conv[None])
        ssm_out.append(new_ssm.reshape(1, b, SSD_HEADS, SSD_HEADDIM, D_STATE))
        xs_all[n] = _ffn(x, norm_w[0, 2], w_in, w_out, (0, 1))

    kv_out = []
    for n, (x, (b, l)) in enumerate(zip(xs_all, dims)):
        x = _ffn(x, norm_w[1, 0], w_in, w_out, (1, 0))
        (qkv,) = _norm_mm(x, norm_w[1, 1], w_qkv, (QKV_WIDTH,))
        if n == 0:
            o = _attn_prompt(qkv, slope_tab, b, l)
            kv_t = [_kv_tail_t(qkv, g, b, l) for g in range(N_ATTN_GROUPS)]
            kv_out.append([jnp.transpose(t.reshape(b, 2, ATTN_HPG, ATTN_HEAD_DIM, t.shape[2]), (0, 4, 1, 2, 3))[None]
                           for t in kv_t])
        else:
            o = _attn_sample(qkv, caches_t, slope_tab, b, l)
            kv_out.append(_kv_rows(qkv, b, l))
        x = _mm_res(o, w_o, x)
        xs_all[n] = _ffn(x, norm_w[1, 2], w_in, w_out, (1, 1), g_final=norm_f)

    return (xs_all[0].reshape(bp, lp, d), xs_all[1].reshape(bs, ls, d),
            conv_out[0], conv_out[1], ssm_out[0], ssm_out[1],
            kv_out[0][0], kv_out[1][0], kv_out[0][1], kv_out[1][1], kv_out[0][2], kv_out[1][2])
```

```python
import functools

import jax
import jax.numpy as jnp
from jax import lax
from jax.experimental import pallas as pl
from jax.experimental.pallas import tpu as pltpu

F32 = jnp.float32
BF16 = jnp.bfloat16

EPS = 1e-6
NEG_INF = -1e30

D_MODEL = 1024
D_FF = 2816
D_INNER = 2048
SSD_HEADS = 32
SSD_HEADDIM = 64
SSD_GROUPS = 8
SSD_HPG = SSD_HEADS // SSD_GROUPS
GROUP_W = SSD_HPG * SSD_HEADDIM
D_STATE = 128
CONV_W = 4
CONV_DIM = D_INNER + 2 * SSD_GROUPS * D_STATE
SSD_CHUNK = 128
ATTN_PATTERNS = ((128, 1), (512, 4), (2048, 16))
N_ATTN_GROUPS = 3
ATTN_HPG = 8
ATTN_HEAD_DIM = 64
ATTN_WIDTH = ATTN_HPG * ATTN_HEAD_DIM
QKV_WIDTH = N_ATTN_GROUPS * 3 * ATTN_WIDTH
ATTN_QBLOCK = 128
ATTN_NKEYS = 128

LANES = 128
SUBLANES = 8
VMEM_LIMIT = 56 * 1024 * 1024

NT_DIMS = (((1,), (1,)), ((), ()))
TN_DIMS = (((0,), (0,)), ((), ()))


def _const_spec(shape):
    zeros = (0,) * len(shape)
    return pl.BlockSpec(shape, lambda *_: zeros, pipeline_mode=pl.Buffered(1))


def _params(*semantics):
    return pltpu.CompilerParams(dimension_semantics=semantics, vmem_limit_bytes=VMEM_LIMIT)


def _rms(x, g):
    return x * lax.rsqrt(jnp.mean(x * x, axis=-1, keepdims=True) + EPS) * g


def _silu(x):
    return x * jax.nn.sigmoid(x)


def _softplus(x):
    return jnp.maximum(x, 0.0) + jnp.log1p(jnp.exp(-jnp.abs(x)))


def _ffn_body(x_ref, g_ref, win_ref, wout_ref, *rest, mixer_proj, final_norm):
    rest = list(rest)
    o_ref = rest.pop()
    x = x_ref[...]
    if mixer_proj:
        y_ref, wy_ref = rest[:2]
        x = x + jnp.dot(y_ref[...].astype(BF16), wy_ref[...], preferred_element_type=F32)
    if final_norm:
        gf_ref = rest[-1]
    h = _rms(x, g_ref[...]).astype(BF16)
    a = jnp.dot(h, win_ref[:, :D_FF], preferred_element_type=F32)
    b = jnp.dot(h, win_ref[:, D_FF:], preferred_element_type=F32)
    t = (_silu(a) * b).astype(BF16)
    y = x + 0.5 * jnp.dot(t, wout_ref[...], preferred_element_type=F32)
    if final_norm:
        y = _rms(y, gf_ref[...])
    o_ref[...] = y


def _ffn(x, g, w_in, w_out, which, mixer=None, g_final=None, tm=512):
    m, d = x.shape
    tm = min(tm, m)
    row = pl.BlockSpec((tm, d), lambda i: (i, 0))

    def picked(w):
        return pl.BlockSpec((None, None) + w.shape[2:], lambda i: which + (0, 0), pipeline_mode=pl.Buffered(1))

    in_specs = [row, _const_spec((1, d)), picked(w_in), picked(w_out)]
    args = [x, g.reshape(1, d), w_in, w_out]
    if mixer is not None:
        y, w_y = mixer
        in_specs += [pl.BlockSpec((tm, y.shape[1]), lambda i: (i, 0)), _const_spec(w_y.shape)]
        args += [y, w_y]
    if g_final is not None:
        in_specs.append(_const_spec((1, d)))
        args.append(g_final.reshape(1, d))
    return pl.pallas_call(
        functools.partial(_ffn_body, mixer_proj=mixer is not None, final_norm=g_final is not None),
        out_shape=jax.ShapeDtypeStruct((m, d), F32),
        grid=(m // tm,),
        in_specs=in_specs,
        out_specs=row,
        compiler_params=_params("parallel"),
        name="ffn",
    )(*args)


def _norm_mm_body(x_ref, g_ref, w_ref, *o_refs, splits):
    h = _rms(x_ref[...], g_ref[...]).astype(BF16)
    off = 0
    for o_ref, n in zip(o_refs, splits):
        o_ref[...] = jnp.dot(h, w_ref[:, off:off + n], preferred_element_type=F32)
        off += n


def _norm_mm(x, g, w, splits, tm=256):
    m, d = x.shape
    tm = min(tm, m)
    return pl.pallas_call(
        functools.partial(_norm_mm_body, splits=splits),
        out_shape=[jax.ShapeDtypeStruct((m, n), F32) for n in splits],
        grid=(m // tm,),
        in_specs=[pl.BlockSpec((tm, d), lambda i: (i, 0)), _const_spec((1, d)), _const_spec(w.shape)],
        out_specs=[pl.BlockSpec((tm, n), lambda i: (i, 0)) for n in splits],
        compiler_params=_params("parallel"),
        name="norm_mm",
    )(x, g.reshape(1, d), w)


def _expand_heads(v, first, width):
    rows = v.shape[0]
    return jnp.concatenate(
        [jnp.broadcast_to(v[:, first + j:first + j + 1], (rows, width)) for j in range(SSD_HPG)], axis=1)


def _split_bf16(v):
    parts, rest = [], v
    for _ in range(3):
        parts.append(rest.astype(BF16))
        rest = rest - parts[-1].astype(F32)
    return jnp.concatenate(parts, axis=1)


def _ssd_chunk(q, xpad_ref, row0, z, dt_raw, cw_ref, cb_ref, dtb_ref, alog_ref, drep_ref, ng_ref,
               h_in_ref, h_out_ref, gz_ref, spread_refs=None):
    def conv(slab):
        sl = slice(slab * LANES, (slab + 1) * LANES)
        acc = cb_ref[:, sl]
        for k in range(CONV_W):
            lo = row0 - (CONV_W - 1) + k
            acc = acc + cw_ref[k:k + 1, sl] * xpad_ref[slab, lo:lo + q, :]
        return _silu(acc)

    x_slabs = D_INNER // LANES
    bc_slabs = SSD_GROUPS * D_STATE // LANES

    dt = _softplus(dt_raw + dtb_ref[...])
    a = -jnp.exp(alog_ref[...])
    row = lax.broadcasted_iota(jnp.int32, (q, q), 0)
    col = lax.broadcasted_iota(jnp.int32, (q, q), 1)
    causal = row >= col
    if q == SUBLANES:
        a_cum = dt * a
        row_q = lax.broadcasted_iota(jnp.int32, (q, LANES), 0)
        for shift in (1, 2, 4):
            a_cum = a_cum + jnp.where(row_q >= shift, pltpu.roll(a_cum, shift, 0), 0.0)
    else:
        a_cum = jnp.dot(causal.astype(F32), dt * a, precision=lax.Precision.HIGHEST, preferred_element_type=F32)
    a_last = a_cum[q - 1:q, :]
    w_state = jnp.exp(a_last - a_cum) * dt
    e_cum = jnp.exp(a_cum)
    e_last = jnp.exp(a_last)
    a_cum_t = a_cum.T
    dt_t = dt.T
    if spread_refs is not None:
        spread64_ref, spread128_ref = spread_refs
        lane_head = lax.broadcasted_iota(jnp.int32, (q, GROUP_W), 1) // SSD_HEADDIM
        e_cum_x = jnp.dot(_split_bf16(e_cum), spread64_ref[...], preferred_element_type=F32)
        w_state_x = jnp.dot(_split_bf16(w_state), spread64_ref[...], preferred_element_type=F32)
        a_cum_x = jnp.dot(_split_bf16(a_cum), spread128_ref[...], preferred_element_type=F32)

    for g in range(SSD_GROUPS):
        h0 = g * SSD_HPG
        c0 = g * GROUP_W
        bmb = conv(x_slabs + g).astype(BF16)
        cmb = conv(x_slabs + bc_slabs + g).astype(BF16)
        cb = lax.dot_general(cmb, bmb, NT_DIMS, preferred_element_type=F32)
        x_g = jnp.concatenate([conv(c0 // LANES + i) for i in range(GROUP_W // LANES)], axis=1)
        wts = []
        for j in range(SSD_HPG):
            h = h0 + j
            a_col = a_cum[:, h:h + 1] if spread_refs is None else a_cum_x[:, h * LANES:(h + 1) * LANES]
            seg = a_col - a_cum_t[h:h + 1, :]
            decay = jnp.exp(jnp.where(causal, seg, -jnp.inf))
            wts.append((cb * decay * dt_t[h:h + 1, :]).astype(BF16))
        if spread_refs is None:
            y_diag = jnp.concatenate(
                [jnp.dot(w, x_g[:, j * SSD_HEADDIM:(j + 1) * SSD_HEADDIM].astype(BF16), preferred_element_type=F32)
                 for j, w in enumerate(wts)], axis=1)
        else:
            x_heads = jnp.concatenate([jnp.where(lane_head == j, x_g, 0.0) for j in range(SSD_HPG)], axis=0)
            y_diag = jnp.dot(jnp.concatenate(wts, axis=1), x_heads.astype(BF16), preferred_element_type=F32)

        h_prev = h_in_ref[c0:c0 + GROUP_W, :]
        y_off = lax.dot_general(cmb, h_prev.astype(BF16), NT_DIMS, preferred_element_type=F32)
        if spread_refs is None:
            y_off = y_off * _expand_heads(e_cum, h0, SSD_HEADDIM)
            xw = (x_g * _expand_heads(w_state, h0, SSD_HEADDIM)).astype(BF16)
        else:
            y_off = y_off * e_cum_x[:, c0:c0 + GROUP_W]
            xw = (x_g * w_state_x[:, c0:c0 + GROUP_W]).astype(BF16)
        states = lax.dot_general(xw, bmb, TN_DIMS, preferred_element_type=F32)
        carry = jnp.concatenate(
            [jnp.broadcast_to(e_last[:, h0 + j:h0 + j + 1], (SSD_HEADDIM, D_STATE)) for j in range(SSD_HPG)], axis=0)
        h_out_ref[c0:c0 + GROUP_W, :] = carry * h_prev + states

        y = y_diag + y_off + drep_ref[:, c0:c0 + GROUP_W] * x_g
        gz = y * _silu(z[:, c0:c0 + GROUP_W])
        gz = gz * lax.rsqrt(jnp.mean(gz * gz, axis=-1, keepdims=True) + EPS) * ng_ref[:, c0:c0 + GROUP_W]
        gz_ref[:, c0:c0 + GROUP_W] = gz.astype(gz_ref.dtype)


PAD_ROWS = 8


CONV_SLABS = CONV_DIM // LANES
CONV_HIST = CONV_W - 1
CHUNKS_PER_STEP = 2


def _ssd_prompt_body(x_ref, g_ref, wp_ref, cs_ref, h0_ref, cw_ref, cb_ref, dtb_ref, alog_ref, drep_ref, ng_ref,
                     wo_ref, spread64_ref, spread128_ref, o_ref, nc_ref, hout_ref, xpad_ref, gz_ref):
    q = SSD_CHUNK
    rows = CHUNKS_PER_STEP * q

    @pl.when(pl.program_id(1) == 0)
    def _():
        for j in range(CONV_SLABS):
            xpad_ref[j, PAD_ROWS - CONV_HIST:PAD_ROWS, :] = cs_ref[0, :, j * LANES:(j + 1) * LANES]
        hout_ref[0] = h0_ref[0]

    x = x_ref[...]
    h = _rms(x, g_ref[...]).astype(BF16)
    z = jnp.dot(h, wp_ref[:, :D_INNER], preferred_element_type=F32)
    xbc = jnp.dot(h, wp_ref[:, D_INNER:D_INNER + CONV_DIM], preferred_element_type=F32)
    dt_raw = jnp.dot(h, wp_ref[:, D_INNER + CONV_DIM:], preferred_element_type=F32)
    for j in range(CONV_SLABS):
        xpad_ref[j, PAD_ROWS:PAD_ROWS + rows, :] = xbc[:, j * LANES:(j + 1) * LANES]
    state = hout_ref.at[0]
    for c in range(CHUNKS_PER_STEP):
        sl = slice(c * q, (c + 1) * q)
        _ssd_chunk(q, xpad_ref, PAD_ROWS + c * q, z[sl], dt_raw[sl], cw_ref, cb_ref, dtb_ref, alog_ref, drep_ref,
                   ng_ref, state, state, gz_ref.at[pl.ds(c * q, q)], (spread64_ref, spread128_ref))
    o_ref[...] = x + jnp.dot(gz_ref[...], wo_ref[...], preferred_element_type=F32)
    for j in range(CONV_SLABS):
        tail = xpad_ref[j, PAD_ROWS + rows - CONV_HIST:PAD_ROWS + rows, :]
        nc_ref[0, :, j * LANES:(j + 1) * LANES] = tail
        xpad_ref[j, PAD_ROWS - CONV_HIST:PAD_ROWS, :] = tail


def _ssd_prompt(x, g, w_proj, conv_state, ssm_state, prm, w_out, batch, seq):
    rows = CHUNKS_PER_STEP * SSD_CHUNK
    steps = seq // rows
    d = x.shape[1]
    hp = SSD_HEADS * SSD_HEADDIM
    row = lambda b, c: (b * steps + c, 0)
    per_b = lambda b, c: (b, 0, 0)
    head_of_row = jnp.arange(3 * LANES)[:, None] % LANES
    spreads = [(head_of_row == jnp.arange(SSD_HEADS * w)[None, :] // w).astype(BF16) for w in (SSD_HEADDIM, LANES)]
    consts = list(prm) + [w_out] + spreads
    return pl.pallas_call(
        _ssd_prompt_body,
        out_shape=[jax.ShapeDtypeStruct(x.shape, F32),
                   jax.ShapeDtypeStruct((batch, CONV_HIST, CONV_DIM), F32),
                   jax.ShapeDtypeStruct((batch, hp, D_STATE), F32)],
        grid=(batch, steps),
        in_specs=[pl.BlockSpec((rows, d), row), _const_spec((1, d)), _const_spec(w_proj.shape),
                  pl.BlockSpec((1, CONV_HIST, CONV_DIM), per_b), pl.BlockSpec((1, hp, D_STATE), per_b)]
        + [_const_spec(c.shape) for c in consts],
        out_specs=[pl.BlockSpec((rows, d), row), pl.BlockSpec((1, CONV_HIST, CONV_DIM), per_b),
                   pl.BlockSpec((1, hp, D_STATE), per_b)],
        scratch_shapes=[pltpu.VMEM((CONV_SLABS, PAD_ROWS + rows, LANES), F32), pltpu.VMEM((rows, D_INNER), BF16)],
        compiler_params=_params("parallel", "arbitrary"),
        name="ssd_prompt",
    )(x, g.reshape(1, d), w_proj, conv_state, ssm_state, *consts)


SEQS_PER_STEP = 2


def _ssd_sample_body(z_ref, xbc_ref, dt_ref, cs_ref, h0_ref, cw_ref, cb_ref, dtb_ref, alog_ref, drep_ref, ng_ref,
                     gz_ref, nc_ref, hout_ref, xpad_ref, *, q):
    span = PAD_ROWS + q
    for e in range(SEQS_PER_STEP):
        row0 = e * span + PAD_ROWS
        sl = slice(e * q, (e + 1) * q)
        for j in range(CONV_SLABS):
            lanes = slice(j * LANES, (j + 1) * LANES)
            xpad_ref[j, row0 - CONV_HIST:row0, :] = cs_ref[e, :, lanes]
            xpad_ref[j, row0:row0 + q, :] = xbc_ref[sl, lanes]
        _ssd_chunk(q, xpad_ref, row0, z_ref[sl, :], dt_ref[sl, :], cw_ref, cb_ref, dtb_ref, alog_ref, drep_ref,
                   ng_ref, h0_ref.at[e], hout_ref.at[e], gz_ref.at[pl.ds(e * q, q)])
        for j in range(CONV_SLABS):
            nc_ref[e, :, j * LANES:(j + 1) * LANES] = xpad_ref[j, row0 + q - CONV_HIST:row0 + q, :]


def _ssd_sample(z, xbc, dt_raw, conv_state, ssm_state, prm, batch, seq):
    n = SEQS_PER_STEP
    hp = SSD_HEADS * SSD_HEADDIM
    row = lambda b: (b, 0)
    per_b = lambda b: (b, 0, 0)
    return pl.pallas_call(
        functools.partial(_ssd_sample_body, q=seq),
        out_shape=[jax.ShapeDtypeStruct((batch * seq, D_INNER), F32),
                   jax.ShapeDtypeStruct((batch, CONV_HIST, CONV_DIM), F32),
                   jax.ShapeDtypeStruct((batch, hp, D_STATE), F32)],
        grid=(batch // n,),
        in_specs=[pl.BlockSpec((n * seq, D_INNER), row), pl.BlockSpec((n * seq, CONV_DIM), row),
                  pl.BlockSpec((n * seq, LANES), row),
                  pl.BlockSpec((n, CONV_HIST, CONV_DIM), per_b), pl.BlockSpec((n, hp, D_STATE), per_b)]
        + [_const_spec(c.shape) for c in prm],
        out_specs=[pl.BlockSpec((n * seq, D_INNER), row), pl.BlockSpec((n, CONV_HIST, CONV_DIM), per_b),
                   pl.BlockSpec((n, hp, D_STATE), per_b)],
        scratch_shapes=[pltpu.VMEM((CONV_SLABS, n * (PAD_ROWS + seq), LANES), F32)],
        compiler_params=_params("parallel"),
        name="ssd_sample",
    )(z, xbc, dt_raw, conv_state, ssm_state, *prm)


ATTN_SLAB = ATTN_QBLOCK * 16
HEAD_PAIRS = ATTN_HPG // 2


def _attn_prompt_body(slope_ref, *refs):
    qkv_refs = refs[:9]
    o_ref = refs[9]
    m_sc, l_sc, acc_sc = refs[10:13]
    kv_ext = refs[13:]
    pair = pl.program_id(1)
    s_idx = pl.program_id(2)
    qb = ATTN_QBLOCK

    for g, (_, dil) in enumerate(ATTN_PATTERNS):
        hist = qb * dil
        for ext, src in zip(kv_ext[2 * g:2 * g + 2], qkv_refs[3 * g + 1:3 * g + 3]):
            @pl.when(s_idx == 0)
            def _(ext=ext, hist=hist):
                ext[0:hist, :] = jnp.zeros((hist, LANES), F32)

            ext[hist:hist + ATTN_SLAB, :] = src[...]

    row = lax.broadcasted_iota(jnp.int32, (qb, 2 * qb), 0)
    col = lax.broadcasted_iota(jnp.int32, (qb, 2 * qb), 1)
    dist = qb + row - col
    in_window = (dist >= 0) & (dist <= ATTN_NKEYS)
    valid_first = in_window & (col >= jnp.where(s_idx > 0, 0, qb))
    dist_f = dist.astype(F32)
    lane = lax.broadcasted_iota(jnp.int32, (qb, LANES), 1)
    low_half = lane < ATTN_HEAD_DIM

    for g, (_, dil) in enumerate(ATTN_PATTERNS):
        q_ref = qkv_refs[3 * g]
        k_ext, v_ext = kv_ext[2 * g:2 * g + 2]
        n_blk = ATTN_SLAB // (qb * dil)
        slopes = [slope_ref[g * ATTN_HPG + 2 * pair + e] * float(dil) for e in range(2)]
        bias_in = [jnp.where(in_window, -slopes[e] * dist_f, NEG_INF) for e in range(2)]
        bias_first = [jnp.where(valid_first, -slopes[e] * dist_f, NEG_INF) for e in range(2)]

        for r, blk in [(r, blk) for r in range(dil) for blk in range(n_blk)]:
            start = r + (dil * qb) * blk
            rows = pl.ds(start, qb) if dil == 1 else pl.ds(start, qb, stride=dil)
            key_rows = pl.ds(start, 2 * qb) if dil == 1 else pl.ds(start, 2 * qb, stride=dil)
            k2 = k_ext[key_rows, :].astype(BF16)
            v2 = v_ext[key_rows, :].astype(BF16)
            q2 = q_ref[rows, :] * (ATTN_HEAD_DIM ** -0.5)
            bias = bias_in if blk > 0 else bias_first
            ms, ls, pvs = [], [], []
            for e in range(2):
                qm = jnp.where(low_half if e == 0 else ~low_half, q2, 0.0).astype(BF16)
                s = lax.dot_general(qm, k2, NT_DIMS, preferred_element_type=F32) + bias[e]
                m = jnp.max(s, axis=-1, keepdims=True)
                p = jnp.exp(s - m)
                ms.append(m)
                ls.append(jnp.sum(p, axis=-1, keepdims=True))
                pvs.append(jnp.dot(p.astype(BF16), v2, preferred_element_type=F32))
            m_new = jnp.where(low_half, ms[0], ms[1])
            l_new = jnp.where(low_half, ls[0], ls[1])
            pv_new = jnp.where(low_half, pvs[0], pvs[1])
            if g > 0:
                m_old = m_sc[rows, :]
                m_tot = jnp.maximum(m_old, m_new)
                a_old = jnp.exp(m_old - m_tot)
                a_new = jnp.exp(m_new - m_tot)
                l_new = a_old * l_sc[rows, :] + a_new * l_new
                pv_new = a_old * acc_sc[rows, :] + a_new * pv_new
                m_new = m_tot
            if g == N_ATTN_GROUPS - 1:
                o_ref[rows, :] = pv_new / l_new
            else:
                m_sc[rows, :] = m_new
                l_sc[rows, :] = l_new
                acc_sc[rows, :] = pv_new

    for g, (_, dil) in enumerate(ATTN_PATTERNS):
        hist = qb * dil
        for ext in kv_ext[2 * g:2 * g + 2]:
            ext[0:hist, :] = ext[ATTN_SLAB:ATTN_SLAB + hist, :]


def _attn_prompt(qkv, slope_tab, batch, seq):
    n_slab = seq // ATTN_SLAB
    col_blocks = ATTN_WIDTH // LANES

    def spec(g, which):
        base = (3 * g + which) * col_blocks
        return pl.BlockSpec((ATTN_SLAB, LANES), lambda b, p, s: (b * n_slab + s, base + p))

    in_specs = [pl.BlockSpec(memory_space=pltpu.SMEM)]
    in_specs += [spec(g, which) for g in range(N_ATTN_GROUPS) for which in range(3)]
    return pl.pallas_call(
        _attn_prompt_body,
        out_shape=jax.ShapeDtypeStruct((batch * seq, ATTN_WIDTH), F32),
        grid=(batch, HEAD_PAIRS, n_slab),
        in_specs=in_specs,
        out_specs=pl.BlockSpec((ATTN_SLAB, LANES), lambda b, p, s: (b * n_slab + s, p)),
        scratch_shapes=[pltpu.VMEM((ATTN_SLAB, LANES), F32)] * 3 + [
            pltpu.VMEM((ATTN_QBLOCK * dil + ATTN_SLAB, LANES), F32) for _, dil in ATTN_PATTERNS for _ in range(2)],
        compiler_params=_params("parallel", "parallel", "arbitrary"),
        name="attn_prompt",
    )(slope_tab, *([qkv] * 9))


def _kv_tail_body(k_ref, v_ref, o_ref):
    o_ref[0, :ATTN_WIDTH, :] = k_ref[...].T
    o_ref[0, ATTN_WIDTH:, :] = v_ref[...].T


def _kv_tail_t(qkv, g, batch, seq, tm=512):
    win = min(ATTN_PATTERNS[g][0], seq)
    tm = min(tm, win)
    first = (seq - win) // tm
    per_b = seq // tm

    def spec(which):
        return pl.BlockSpec((tm, ATTN_WIDTH), lambda b, j: (b * per_b + first + j, 3 * g + which))

    return pl.pallas_call(
        _kv_tail_body,
        out_shape=jax.ShapeDtypeStruct((batch, 2 * ATTN_WIDTH, win), F32),
        grid=(batch, win // tm),
        in_specs=[spec(1), spec(2)],
        out_specs=pl.BlockSpec((1, 2 * ATTN_WIDTH, tm), lambda b, j: (b, 0, j)),
        compiler_params=_params("parallel", "parallel"),
        name=f"kv_tail_g{g}",
    )(qkv, qkv)


SAMPLE_SEQS_PER_STEP = 2


def _attn_sample_body(slope_ref, qkv_ref, c0_ref, c1_ref, c2_ref, o_ref, *, n_new):
    caches = (c0_ref, c1_ref, c2_ref)
    rows = ATTN_HPG * n_new
    row_head = lax.broadcasted_iota(jnp.int32, (rows, ATTN_WIDTH), 0) // n_new
    lane_head = lax.broadcasted_iota(jnp.int32, (rows, ATTN_WIDTH), 1) // ATTN_HEAD_DIM
    diag = row_head == lane_head
    head_col = lax.broadcasted_iota(jnp.int32, (rows, 1), 0) // n_new
    qi_n = lax.broadcasted_iota(jnp.int32, (rows, n_new), 0) % n_new
    gap_n = qi_n - lax.broadcasted_iota(jnp.int32, (rows, n_new), 1)
    geometry = []
    for g, (c_ref, (win, dil)) in enumerate(zip(caches, ATTN_PATTERNS)):
        past_len = c_ref.shape[2]
        slope = jnp.zeros((rows, 1), F32)
        for h in range(ATTN_HPG):
            slope = jnp.where(head_col == h, slope_ref[g * ATTN_HPG + h], slope)
        gap_p = (past_len + lax.broadcasted_iota(jnp.int32, (rows, past_len), 0) % n_new
                 - lax.broadcasted_iota(jnp.int32, (rows, past_len), 1))
        ok_p = (gap_p <= win) & ((gap_p & (dil - 1)) == 0)
        ok_n = (gap_n >= 0) & ((gap_n & (dil - 1)) == 0)
        geometry.append((ok_p, slope * gap_p.astype(F32), ok_n, slope * gap_n.astype(F32)))

    for e in range(SAMPLE_SEQS_PER_STEP):
        qkv = qkv_ref[e * n_new:(e + 1) * n_new, :]
        outs, lses = [], []
        for g, c_ref in enumerate(caches):
            base = g * 3 * ATTN_WIDTH
            ok_p, bias_p, ok_n, bias_n = geometry[g]
            q = qkv[:, base:base + ATTN_WIDTH] * (ATTN_HEAD_DIM ** -0.5)
            q_bd = jnp.where(diag, jnp.concatenate([q] * ATTN_HPG, axis=0), 0.0).astype(BF16)
            k_new = qkv[:, base + ATTN_WIDTH:base + 2 * ATTN_WIDTH].astype(BF16)
            v_new = qkv[:, base + 2 * ATTN_WIDTH:base + 3 * ATTN_WIDTH].astype(BF16)
            k_t = c_ref[e, :ATTN_WIDTH, :].astype(BF16)
            v_t = c_ref[e, ATTN_WIDTH:, :].astype(BF16)
            s_p = jnp.dot(q_bd, k_t, preferred_element_type=F32) - bias_p
            s_p = jnp.where(ok_p, s_p, NEG_INF)
            s_n = lax.dot_general(q_bd, k_new, NT_DIMS, preferred_element_type=F32) - bias_n
            s_n = jnp.where(ok_n, s_n, NEG_INF)
            mx = jnp.maximum(jnp.max(s_p, axis=-1, keepdims=True), jnp.max(s_n, axis=-1, keepdims=True))
            e_p = jnp.exp(s_p - mx)
            e_n = jnp.exp(s_n - mx)
            l = jnp.sum(e_p, axis=-1, keepdims=True) + jnp.sum(e_n, axis=-1, keepdims=True)
            o = (lax.dot_general(e_p.astype(BF16), v_t, NT_DIMS, preferred_element_type=F32)
                 + jnp.dot(e_n.astype(BF16), v_new, preferred_element_type=F32)) / l
            outs.append(o)
            lses.append(mx + jnp.log(l))
        top = jnp.maximum(jnp.maximum(lses[0], lses[1]), lses[2])
        es = [jnp.exp(l - top) for l in lses]
        merged = (es[0] * outs[0] + es[1] * outs[1] + es[2] * outs[2]) / (es[0] + es[1] + es[2])
        merged = jnp.where(diag, merged, 0.0).reshape(ATTN_HPG, n_new, ATTN_WIDTH)
        o_ref[e * n_new:(e + 1) * n_new, :] = jnp.sum(merged, axis=0)


def _attn_sample(qkv, caches_t, slope_tab, batch, n_new):
    n = SAMPLE_SEQS_PER_STEP
    specs = [pl.BlockSpec((n,) + c.shape[1:], lambda b: (b, 0, 0)) for c in caches_t]
    return pl.pallas_call(
        functools.partial(_attn_sample_body, n_new=n_new),
        out_shape=jax.ShapeDtypeStruct((batch * n_new, ATTN_WIDTH), F32),
        grid=(batch // n,),
        in_specs=[pl.BlockSpec(memory_space=pltpu.SMEM),
                  pl.BlockSpec((n * n_new, QKV_WIDTH), lambda b: (b, 0))] + specs,
        out_specs=pl.BlockSpec((n * n_new, ATTN_WIDTH), lambda b: (b, 0)),
        compiler_params=_params("parallel"),
        name="attn_sample",
    )(slope_tab, qkv, *caches_t)


KV_ROWS_SEQS = 16


def _kv_rows_body(qkv_ref, *o_refs, n_new):
    for g, o_ref in enumerate(o_refs):
        lo = (3 * g + 1) * ATTN_WIDTH
        for b in range(KV_ROWS_SEQS):
            rows = qkv_ref[b * n_new:(b + 1) * n_new, lo:lo + 2 * ATTN_WIDTH]
            o_ref[0, b] = rows.reshape(n_new, 2, ATTN_HPG, ATTN_HEAD_DIM)


def _kv_rows(qkv, batch, n_new):
    shape = (1, batch, n_new, 2, ATTN_HPG, ATTN_HEAD_DIM)
    block = (1, KV_ROWS_SEQS) + shape[2:]
    return pl.pallas_call(
        functools.partial(_kv_rows_body, n_new=n_new),
        out_shape=[jax.ShapeDtypeStruct(shape, F32)] * N_ATTN_GROUPS,
        grid=(batch // KV_ROWS_SEQS,),
        in_specs=[pl.BlockSpec((KV_ROWS_SEQS * n_new, QKV_WIDTH), lambda i: (i, 0))],
        out_specs=[pl.BlockSpec(block, lambda i: (0, i, 0, 0, 0, 0))] * N_ATTN_GROUPS,
        compiler_params=_params("parallel"),
        name="kv_rows",
    )(qkv)


def _alibi_slopes():
    n_heads = N_ATTN_GROUPS * ATTN_HPG
    return [2.0 ** (-8.0 * (h + 1) / n_heads) for h in range(n_heads)]


def kernel(x_prompt, x_sample, state_conv, state_ssm, cache_kv_g0, cache_kv_g1, cache_kv_g2, norm_w, w_ffn_in,
           w_ffn_out, ssm_w_in, ssm_conv_w, ssm_conv_b, ssm_dt_bias, ssm_a_log, ssm_d, ssm_norm_w, ssm_w_out,
           attn_w_qkv, attn_w_o, norm_f):
    bp, lp, d = x_prompt.shape
    bs, ls, _ = x_sample.shape
    xs_all = [x_prompt.reshape(bp * lp, d), x_sample.reshape(bs * ls, d)]
    dims = [(bp, lp), (bs, ls)]
    hist = CONV_W - 1
    hp = SSD_HEADS * SSD_HEADDIM

    w_in = w_ffn_in.astype(BF16)
    w_out = w_ffn_out.astype(BF16)
    pad = LANES - SSD_HEADS
    w_proj = jnp.pad(ssm_w_in[0], ((0, 0), (0, pad))).astype(BF16)
    ssd_prm = (ssm_conv_w[0], ssm_conv_b[0].reshape(1, CONV_DIM),
               jnp.pad(ssm_dt_bias[0], (0, pad)).reshape(1, LANES),
               jnp.pad(ssm_a_log[0], (0, pad)).reshape(1, LANES),
               jnp.repeat(ssm_d[0], SSD_HEADDIM).reshape(1, D_INNER),
               ssm_norm_w[0].reshape(1, D_INNER))
    w_ssm_out = ssm_w_out[0].astype(BF16)
    w_qkv = attn_w_qkv[0].astype(BF16)
    w_o = attn_w_o[0].astype(BF16)
    slope_tab = jnp.asarray(_alibi_slopes(), F32)

    conv_states = [jnp.zeros((bp, hist, CONV_DIM), F32), state_conv[0]]
    ssm_states = [jnp.zeros((bp, hp, D_STATE), F32), state_ssm[0].reshape(bs, hp, D_STATE)]
    caches_t = [jnp.transpose(c[0], (0, 2, 3, 4, 1)).reshape(bs, 2 * ATTN_WIDTH, c.shape[2])
                for c in (cache_kv_g0, cache_kv_g1, cache_kv_g2)]

    conv_out, ssm_out = [], []
    for n, (x, (b, l)) in enumerate(zip(xs_all, dims)):
        x = _ffn(x, norm_w[0, 0], w_in, w_out, (0, 0))
        mixer = None
        if l % (CHUNKS_PER_STEP * SSD_CHUNK) == 0:
            x, new_conv, new_ssm = _ssd_prompt(x, norm_w[0, 1], w_proj, conv_states[n], ssm_states[n], ssd_prm,
                                               w_ssm_out, b, l)
        else:
            z, xbc, dt_raw = _norm_mm(x, norm_w[0, 1], w_proj, (D_INNER, CONV_DIM, LANES))
            gz, new_conv, new_ssm = _ssd_sample(z, xbc, dt_raw, conv_states[n], ssm_states[n], ssd_prm, b, l)
            mixer = (gz, w_ssm_out)
        conv_out.append(new_conv[None])
        ssm_out.append(new_ssm.reshape(1, b, SSD_HEADS, SSD_HEADDIM, D_STATE))
        xs_all[n] = _ffn(x, norm_w[0, 2], w_in, w_out, (0, 1), mixer=mixer)

    kv_out = []
    for n, (x, (b, l)) in enumerate(zip(xs_all, dims)):
        x = _ffn(x, norm_w[1, 0], w_in, w_out, (1, 0))
        (qkv,) = _norm_mm(x, norm_w[1, 1], w_qkv, (QKV_WIDTH,), tm=512)
        if n == 0:
            o = _attn_prompt(qkv, slope_tab, b, l)
            kv_t = [_kv_tail_t(qkv, g, b, l) for g in range(N_ATTN_GROUPS)]
            kv_out.append([jnp.transpose(t.reshape(b, 2, ATTN_HPG, ATTN_HEAD_DIM, t.shape[2]), (0, 4, 1, 2, 3))[None]
                           for t in kv_t])
        else:
            o = _attn_sample(qkv, caches_t, slope_tab, b, l)
            kv_out.append(_kv_rows(qkv, b, l))
        xs_all[n] = _ffn(x, norm_w[1, 2], w_in, w_out, (1, 1), mixer=(o, w_o), g_final=norm_f)

    return (xs_all[0].reshape(bp, lp, d), xs_all[1].reshape(bs, ls, d),
            conv_out[0], conv_out[1], ssm_out[0], ssm_out[1],
            kv_out[0][0], kv_out[1][0], kv_out[0][1], kv_out[1][1], kv_out[0][2], kv_out[1][2])
```

```python
import functools

import jax
import jax.numpy as jnp
from jax import lax
from jax.experimental import pallas as pl
from jax.experimental.pallas import tpu as pltpu

F32 = jnp.float32
BF16 = jnp.bfloat16

EPS = 1e-6
NEG_INF = -1e30

D_FF = 2816
D_INNER = 2048
SSD_HEADS = 32
SSD_HEADDIM = 64
SSD_GROUPS = 8
SSD_HPG = SSD_HEADS // SSD_GROUPS
GROUP_W = SSD_HPG * SSD_HEADDIM
D_STATE = 128
CONV_W = 4
CONV_DIM = D_INNER + 2 * SSD_GROUPS * D_STATE
SSD_CHUNK = 128
ATTN_PATTERNS = ((128, 1), (512, 4), (2048, 16))
N_ATTN_GROUPS = 3
ATTN_HPG = 8
ATTN_HEAD_DIM = 64
ATTN_WIDTH = ATTN_HPG * ATTN_HEAD_DIM
QKV_WIDTH = N_ATTN_GROUPS * 3 * ATTN_WIDTH
ATTN_QBLOCK = 128
ATTN_NKEYS = 128

LANES = 128
SUBLANES = 8
VMEM_LIMIT = 56 * 1024 * 1024

FFN_ROWS = 512
PROJ_ROWS = 256
QKV_ROWS = 512
KV_TAIL_TOKENS = 512

NT_DIMS = (((1,), (1,)), ((), ()))
TN_DIMS = (((0,), (0,)), ((), ()))


def _const_spec(shape):
    zeros = (0,) * len(shape)
    return pl.BlockSpec(shape, lambda *_: zeros, pipeline_mode=pl.Buffered(1))


def _params(*semantics):
    return pltpu.CompilerParams(dimension_semantics=semantics, vmem_limit_bytes=VMEM_LIMIT)


def _rms(x, g):
    return x * lax.rsqrt(jnp.mean(x * x, axis=-1, keepdims=True) + EPS) * g


def _silu(x):
    return x * jax.nn.sigmoid(x)


def _softplus(x):
    return jnp.maximum(x, 0.0) + jnp.log1p(jnp.exp(-jnp.abs(x)))


def _ffn_body(x_ref, g_ref, win_ref, wout_ref, *rest, mixer_proj, final_norm):
    rest = list(rest)
    o_ref = rest.pop()
    x = x_ref[...]
    if mixer_proj:
        y_ref, wy_ref = rest[:2]
        x = x + jnp.dot(y_ref[...].astype(BF16), wy_ref[...], preferred_element_type=F32)
    if final_norm:
        gf_ref = rest[-1]
    h = _rms(x, g_ref[...]).astype(BF16)
    a = jnp.dot(h, win_ref[:, :D_FF], preferred_element_type=F32)
    b = jnp.dot(h, win_ref[:, D_FF:], preferred_element_type=F32)
    t = (_silu(a) * b).astype(BF16)
    y = x + 0.5 * jnp.dot(t, wout_ref[...], preferred_element_type=F32)
    if final_norm:
        y = _rms(y, gf_ref[...])
    o_ref[...] = y


def _ffn(x, g, w_in, w_out, which, mixer=None, g_final=None, tm=FFN_ROWS):
    m, d = x.shape
    tm = min(tm, m)
    row = pl.BlockSpec((tm, d), lambda i: (i, 0))

    def picked(w):
        return pl.BlockSpec((None, None) + w.shape[2:], lambda i: which + (0, 0), pipeline_mode=pl.Buffered(1))

    in_specs = [row, _const_spec((1, d)), picked(w_in), picked(w_out)]
    args = [x, g.reshape(1, d), w_in, w_out]
    if mixer is not None:
        y, w_y = mixer
        in_specs += [pl.BlockSpec((tm, y.shape[1]), lambda i: (i, 0)), _const_spec(w_y.shape)]
        args += [y, w_y]
    if g_final is not None:
        in_specs.append(_const_spec((1, d)))
        args.append(g_final.reshape(1, d))
    return pl.pallas_call(
        functools.partial(_ffn_body, mixer_proj=mixer is not None, final_norm=g_final is not None),
        out_shape=jax.ShapeDtypeStruct((m, d), F32),
        grid=(m // tm,),
        in_specs=in_specs,
        out_specs=row,
        compiler_params=_params("parallel"),
        name="ffn",
    )(*args)


def _norm_mm_body(x_ref, g_ref, w_ref, *o_refs, splits):
    h = _rms(x_ref[...], g_ref[...]).astype(BF16)
    off = 0
    for o_ref, n in zip(o_refs, splits):
        o_ref[...] = jnp.dot(h, w_ref[:, off:off + n], preferred_element_type=F32)
        off += n


def _norm_mm(x, g, w, splits, tm=PROJ_ROWS):
    m, d = x.shape
    tm = min(tm, m)
    return pl.pallas_call(
        functools.partial(_norm_mm_body, splits=splits),
        out_shape=[jax.ShapeDtypeStruct((m, n), F32) for n in splits],
        grid=(m // tm,),
        in_specs=[pl.BlockSpec((tm, d), lambda i: (i, 0)), _const_spec((1, d)), _const_spec(w.shape)],
        out_specs=[pl.BlockSpec((tm, n), lambda i: (i, 0)) for n in splits],
        compiler_params=_params("parallel"),
        name="norm_mm",
    )(x, g.reshape(1, d), w)


def _expand_heads(v, first, width):
    rows = v.shape[0]
    return jnp.concatenate(
        [jnp.broadcast_to(v[:, first + j:first + j + 1], (rows, width)) for j in range(SSD_HPG)], axis=1)


def _split_bf16(v):
    parts, rest = [], v
    for _ in range(3):
        parts.append(rest.astype(BF16))
        rest = rest - parts[-1].astype(F32)
    return jnp.concatenate(parts, axis=1)


def _ssd_chunk(q, xpad_ref, row0, z, dt_raw, cw_ref, cb_ref, dtb_ref, alog_ref, drep_ref, ng_ref,
               h_in_ref, h_out_ref, gz_ref, spread_refs=None):
    def conv(slab):
        sl = slice(slab * LANES, (slab + 1) * LANES)
        acc = cb_ref[:, sl]
        for k in range(CONV_W):
            lo = row0 - (CONV_W - 1) + k
            acc = acc + cw_ref[k:k + 1, sl] * xpad_ref[slab, lo:lo + q, :]
        return _silu(acc)

    x_slabs = D_INNER // LANES
    bc_slabs = SSD_GROUPS * D_STATE // LANES

    dt = _softplus(dt_raw + dtb_ref[...])
    a = -jnp.exp(alog_ref[...])
    row = lax.broadcasted_iota(jnp.int32, (q, q), 0)
    col = lax.broadcasted_iota(jnp.int32, (q, q), 1)
    causal = row >= col
    if q == SUBLANES:
        a_cum = dt * a
        row_q = lax.broadcasted_iota(jnp.int32, (q, LANES), 0)
        for shift in (1, 2, 4):
            a_cum = a_cum + jnp.where(row_q >= shift, pltpu.roll(a_cum, shift, 0), 0.0)
    else:
        a_cum = jnp.dot(causal.astype(F32), dt * a, precision=lax.Precision.HIGHEST, preferred_element_type=F32)
    a_last = a_cum[q - 1:q, :]
    w_state = jnp.exp(a_last - a_cum) * dt
    e_cum = jnp.exp(a_cum)
    e_last = jnp.exp(a_last)
    a_cum_t = a_cum.T
    dt_t = dt.T
    if spread_refs is not None:
        spread64_ref, spread128_ref = spread_refs
        lane_head = lax.broadcasted_iota(jnp.int32, (q, GROUP_W), 1) // SSD_HEADDIM
        e_cum_x = jnp.dot(_split_bf16(e_cum), spread64_ref[...], preferred_element_type=F32)
        w_state_x = jnp.dot(_split_bf16(w_state), spread64_ref[...], preferred_element_type=F32)
        a_cum_x = jnp.dot(_split_bf16(a_cum), spread128_ref[...], preferred_element_type=F32)

    for g in range(SSD_GROUPS):
        h0 = g * SSD_HPG
        c0 = g * GROUP_W
        bmb = conv(x_slabs + g).astype(BF16)
        cmb = conv(x_slabs + bc_slabs + g).astype(BF16)
        cb = lax.dot_general(cmb, bmb, NT_DIMS, preferred_element_type=F32)
        x_g = jnp.concatenate([conv(c0 // LANES + i) for i in range(GROUP_W // LANES)], axis=1)
        wts = []
        for j in range(SSD_HPG):
            h = h0 + j
            a_col = a_cum[:, h:h + 1] if spread_refs is None else a_cum_x[:, h * LANES:(h + 1) * LANES]
            seg = a_col - a_cum_t[h:h + 1, :]
            decay = jnp.exp(jnp.where(causal, seg, -jnp.inf))
            wts.append((cb * decay * dt_t[h:h + 1, :]).astype(BF16))
        if spread_refs is None:
            y_diag = jnp.concatenate(
                [jnp.dot(w, x_g[:, j * SSD_HEADDIM:(j + 1) * SSD_HEADDIM].astype(BF16), preferred_element_type=F32)
                 for j, w in enumerate(wts)], axis=1)
        else:
            x_gb = x_g.astype(BF16)
            x_heads = jnp.concatenate(
                [jnp.where(lane_head == j, x_gb, jnp.zeros_like(x_gb)) for j in range(SSD_HPG)], axis=0)
            y_diag = jnp.dot(jnp.concatenate(wts, axis=1), x_heads, preferred_element_type=F32)

        h_prev = h_in_ref[c0:c0 + GROUP_W, :]
        y_off = lax.dot_general(cmb, h_prev.astype(BF16), NT_DIMS, preferred_element_type=F32)
        if spread_refs is None:
            y_off = y_off * _expand_heads(e_cum, h0, SSD_HEADDIM)
            xw = (x_g * _expand_heads(w_state, h0, SSD_HEADDIM)).astype(BF16)
        else:
            y_off = y_off * e_cum_x[:, c0:c0 + GROUP_W]
            xw = (x_g * w_state_x[:, c0:c0 + GROUP_W]).astype(BF16)
        states = lax.dot_general(xw, bmb, TN_DIMS, preferred_element_type=F32)
        carry = jnp.concatenate(
            [jnp.broadcast_to(e_last[:, h0 + j:h0 + j + 1], (SSD_HEADDIM, D_STATE)) for j in range(SSD_HPG)], axis=0)
        h_out_ref[c0:c0 + GROUP_W, :] = carry * h_prev + states

        y = y_diag + y_off + drep_ref[:, c0:c0 + GROUP_W] * x_g
        gz = y * _silu(z[:, c0:c0 + GROUP_W])
        gz = gz * lax.rsqrt(jnp.mean(gz * gz, axis=-1, keepdims=True) + EPS) * ng_ref[:, c0:c0 + GROUP_W]
        gz_ref[:, c0:c0 + GROUP_W] = gz.astype(gz_ref.dtype)


PAD_ROWS = 8


CONV_SLABS = CONV_DIM // LANES
CONV_HIST = CONV_W - 1
CHUNKS_PER_STEP = 2


def _ssd_prompt_body(x_ref, g_ref, wp_ref, cs_ref, h0_ref, cw_ref, cb_ref, dtb_ref, alog_ref, drep_ref, ng_ref,
                     wo_ref, spread64_ref, spread128_ref, o_ref, nc_ref, hout_ref, xpad_ref, gz_ref):
    q = SSD_CHUNK
    rows = CHUNKS_PER_STEP * q

    @pl.when(pl.program_id(1) == 0)
    def _():
        for j in range(CONV_SLABS):
            xpad_ref[j, PAD_ROWS - CONV_HIST:PAD_ROWS, :] = cs_ref[0, :, j * LANES:(j + 1) * LANES]
        hout_ref[0] = h0_ref[0]

    x = x_ref[...]
    h = _rms(x, g_ref[...]).astype(BF16)
    z = jnp.dot(h, wp_ref[:, :D_INNER], preferred_element_type=F32)
    xbc = jnp.dot(h, wp_ref[:, D_INNER:D_INNER + CONV_DIM], preferred_element_type=F32)
    dt_raw = jnp.dot(h, wp_ref[:, D_INNER + CONV_DIM:], preferred_element_type=F32)
    for j in range(CONV_SLABS):
        xpad_ref[j, PAD_ROWS:PAD_ROWS + rows, :] = xbc[:, j * LANES:(j + 1) * LANES]
    state = hout_ref.at[0]
    for c in range(CHUNKS_PER_STEP):
        sl = slice(c * q, (c + 1) * q)
        _ssd_chunk(q, xpad_ref, PAD_ROWS + c * q, z[sl], dt_raw[sl], cw_ref, cb_ref, dtb_ref, alog_ref, drep_ref,
                   ng_ref, state, state, gz_ref.at[pl.ds(c * q, q)], (spread64_ref, spread128_ref))
    o_ref[...] = x + jnp.dot(gz_ref[...], wo_ref[...], preferred_element_type=F32)
    for j in range(CONV_SLABS):
        tail = xpad_ref[j, PAD_ROWS + rows - CONV_HIST:PAD_ROWS + rows, :]
        nc_ref[0, :, j * LANES:(j + 1) * LANES] = tail
        xpad_ref[j, PAD_ROWS - CONV_HIST:PAD_ROWS, :] = tail


def _ssd_prompt(x, g, w_proj, conv_state, ssm_state, prm, w_out, batch, seq):
    rows = CHUNKS_PER_STEP * SSD_CHUNK
    steps = seq // rows
    d = x.shape[1]
    hp = SSD_HEADS * SSD_HEADDIM
    row = lambda b, c: (b * steps + c, 0)
    per_b = lambda b, c: (b, 0, 0)
    head_of_row = jnp.arange(3 * LANES)[:, None] % LANES
    spreads = [(head_of_row == jnp.arange(SSD_HEADS * w)[None, :] // w).astype(BF16) for w in (SSD_HEADDIM, LANES)]
    consts = list(prm) + [w_out] + spreads
    return pl.pallas_call(
        _ssd_prompt_body,
        out_shape=[jax.ShapeDtypeStruct(x.shape, F32),
                   jax.ShapeDtypeStruct((batch, CONV_HIST, CONV_DIM), F32),
                   jax.ShapeDtypeStruct((batch, hp, D_STATE), F32)],
        grid=(batch, steps),
        in_specs=[pl.BlockSpec((rows, d), row), _const_spec((1, d)), _const_spec(w_proj.shape),
                  pl.BlockSpec((1, CONV_HIST, CONV_DIM), per_b), pl.BlockSpec((1, hp, D_STATE), per_b)]
        + [_const_spec(c.shape) for c in consts],
        out_specs=[pl.BlockSpec((rows, d), row), pl.BlockSpec((1, CONV_HIST, CONV_DIM), per_b),
                   pl.BlockSpec((1, hp, D_STATE), per_b)],
        scratch_shapes=[pltpu.VMEM((CONV_SLABS, PAD_ROWS + rows, LANES), F32), pltpu.VMEM((rows, D_INNER), BF16)],
        compiler_params=_params("parallel", "arbitrary"),
        name="ssd_prompt",
    )(x, g.reshape(1, d), w_proj, conv_state, ssm_state, *consts)


SEQS_PER_STEP = 2


def _ssd_sample_body(z_ref, xbc_ref, dt_ref, cs_ref, h0_ref, cw_ref, cb_ref, dtb_ref, alog_ref, drep_ref, ng_ref,
                     gz_ref, nc_ref, hout_ref, xpad_ref, *, q):
    span = PAD_ROWS + q
    for e in range(SEQS_PER_STEP):
        row0 = e * span + PAD_ROWS
        sl = slice(e * q, (e + 1) * q)
        for j in range(CONV_SLABS):
            lanes = slice(j * LANES, (j + 1) * LANES)
            xpad_ref[j, row0 - CONV_HIST:row0, :] = cs_ref[e, :, lanes]
            xpad_ref[j, row0:row0 + q, :] = xbc_ref[sl, lanes]
        _ssd_chunk(q, xpad_ref, row0, z_ref[sl, :], dt_ref[sl, :], cw_ref, cb_ref, dtb_ref, alog_ref, drep_ref,
                   ng_ref, h0_ref.at[e], hout_ref.at[e], gz_ref.at[pl.ds(e * q, q)])
        for j in range(CONV_SLABS):
            nc_ref[e, :, j * LANES:(j + 1) * LANES] = xpad_ref[j, row0 + q - CONV_HIST:row0 + q, :]


def _ssd_sample(z, xbc, dt_raw, conv_state, ssm_state, prm, batch, seq):
    n = SEQS_PER_STEP
    hp = SSD_HEADS * SSD_HEADDIM
    row = lambda b: (b, 0)
    per_b = lambda b: (b, 0, 0)
    return pl.pallas_call(
        functools.partial(_ssd_sample_body, q=seq),
        out_shape=[jax.ShapeDtypeStruct((batch * seq, D_INNER), F32),
                   jax.ShapeDtypeStruct((batch, CONV_HIST, CONV_DIM), F32),
                   jax.ShapeDtypeStruct((batch, hp, D_STATE), F32)],
        grid=(batch // n,),
        in_specs=[pl.BlockSpec((n * seq, D_INNER), row), pl.BlockSpec((n * seq, CONV_DIM), row),
                  pl.BlockSpec((n * seq, LANES), row),
                  pl.BlockSpec((n, CONV_HIST, CONV_DIM), per_b), pl.BlockSpec((n, hp, D_STATE), per_b)]
        + [_const_spec(c.shape) for c in prm],
        out_specs=[pl.BlockSpec((n * seq, D_INNER), row), pl.BlockSpec((n, CONV_HIST, CONV_DIM), per_b),
                   pl.BlockSpec((n, hp, D_STATE), per_b)],
        scratch_shapes=[pltpu.VMEM((CONV_SLABS, n * (PAD_ROWS + seq), LANES), F32)],
        compiler_params=_params("parallel"),
        name="ssd_sample",
    )(z, xbc, dt_raw, conv_state, ssm_state, *prm)


ATTN_SLAB = ATTN_QBLOCK * 16
HEAD_PAIRS = ATTN_HPG // 2


def _attn_prompt_body(slope_ref, *refs):
    qkv_refs = refs[:9]
    o_ref = refs[9]
    m_sc, l_sc, acc_sc = refs[10:13]
    kv_ext = refs[13:]
    pair = pl.program_id(1)
    s_idx = pl.program_id(2)
    qb = ATTN_QBLOCK

    for g, (_, dil) in enumerate(ATTN_PATTERNS):
        sub = ATTN_SLAB // dil
        for ext, src in zip(kv_ext[2 * g:2 * g + 2], qkv_refs[3 * g + 1:3 * g + 3]):
            @pl.when(s_idx == 0)
            def _(ext=ext, dil=dil):
                ext[:, 0:qb, :] = jnp.zeros((dil, qb, LANES), BF16)

            for r in range(dil):
                picked = src[...] if dil == 1 else src[pl.ds(r, sub, stride=dil), :]
                ext[r, qb:qb + sub, :] = picked.astype(BF16)

    row = lax.broadcasted_iota(jnp.int32, (qb, 2 * qb), 0)
    col = lax.broadcasted_iota(jnp.int32, (qb, 2 * qb), 1)
    dist = qb + row - col
    in_window = (dist >= 0) & (dist <= ATTN_NKEYS)
    valid_first = in_window & (col >= jnp.where(s_idx > 0, 0, qb))
    dist_f = dist.astype(F32)
    lane = lax.broadcasted_iota(jnp.int32, (qb, LANES), 1)
    low_half = lane < ATTN_HEAD_DIM

    for g, (_, dil) in enumerate(ATTN_PATTERNS):
        q_ref = qkv_refs[3 * g]
        k_ext, v_ext = kv_ext[2 * g:2 * g + 2]
        n_blk = ATTN_SLAB // (qb * dil)
        slopes = [slope_ref[g * ATTN_HPG + 2 * pair + e] * float(dil) for e in range(2)]
        bias_in = [jnp.where(in_window, -slopes[e] * dist_f, NEG_INF) for e in range(2)]
        bias_first = [jnp.where(valid_first, -slopes[e] * dist_f, NEG_INF) for e in range(2)]

        for r, blk in [(r, blk) for r in range(dil) for blk in range(n_blk)]:
            start = r + (dil * qb) * blk
            rows = pl.ds(start, qb) if dil == 1 else pl.ds(start, qb, stride=dil)
            k2 = k_ext[r, blk * qb:(blk + 2) * qb, :]
            v2 = v_ext[r, blk * qb:(blk + 2) * qb, :]
            q2 = q_ref[rows, :] * (ATTN_HEAD_DIM ** -0.5)
            bias = bias_in if blk > 0 else bias_first
            ms, ls, pvs = [], [], []
            for e in range(2):
                qm = jnp.where(low_half if e == 0 else ~low_half, q2, 0.0).astype(BF16)
                s = lax.dot_general(qm, k2, NT_DIMS, preferred_element_type=F32) + bias[e]
                m = jnp.max(s, axis=-1, keepdims=True)
                p = jnp.exp(s - m)
                ms.append(m)
                ls.append(jnp.sum(p, axis=-1, keepdims=True))
                pvs.append(jnp.dot(p.astype(BF16), v2, preferred_element_type=F32))
            m_new = jnp.where(low_half, ms[0], ms[1])
            l_new = jnp.where(low_half, ls[0], ls[1])
            pv_new = jnp.where(low_half, pvs[0], pvs[1])
            if g > 0:
                m_old = m_sc[rows, :]
                m_tot = jnp.maximum(m_old, m_new)
                a_old = jnp.exp(m_old - m_tot)
                a_new = jnp.exp(m_new - m_tot)
                l_new = a_old * l_sc[rows, :] + a_new * l_new
                pv_new = a_old * acc_sc[rows, :] + a_new * pv_new
                m_new = m_tot
            if g == N_ATTN_GROUPS - 1:
                o_ref[rows, :] = pv_new / l_new
            else:
                m_sc[rows, :] = m_new
                l_sc[rows, :] = l_new
                acc_sc[rows, :] = pv_new

    for g, (_, dil) in enumerate(ATTN_PATTERNS):
        sub = ATTN_SLAB // dil
        for ext in kv_ext[2 * g:2 * g + 2]:
            ext[:, 0:qb, :] = ext[:, sub:sub + qb, :]


def _attn_prompt(qkv, slope_tab, batch, seq):
    n_slab = seq // ATTN_SLAB
    col_blocks = ATTN_WIDTH // LANES

    def spec(g, which):
        base = (3 * g + which) * col_blocks
        return pl.BlockSpec((ATTN_SLAB, LANES), lambda b, p, s: (b * n_slab + s, base + p))

    in_specs = [pl.BlockSpec(memory_space=pltpu.SMEM)]
    in_specs += [spec(g, which) for g in range(N_ATTN_GROUPS) for which in range(3)]
    return pl.pallas_call(
        _attn_prompt_body,
        out_shape=jax.ShapeDtypeStruct((batch * seq, ATTN_WIDTH), F32),
        grid=(batch, HEAD_PAIRS, n_slab),
        in_specs=in_specs,
        out_specs=pl.BlockSpec((ATTN_SLAB, LANES), lambda b, p, s: (b * n_slab + s, p)),
        scratch_shapes=[pltpu.VMEM((ATTN_SLAB, LANES), F32)] * 3 + [
            pltpu.VMEM((dil, ATTN_QBLOCK + ATTN_SLAB // dil, LANES), BF16)
            for _, dil in ATTN_PATTERNS for _ in range(2)],
        compiler_params=_params("parallel", "parallel", "arbitrary"),
        name="attn_prompt",
    )(slope_tab, *([qkv] * 9))


def _kv_tail_body(k_ref, v_ref, o_ref):
    o_ref[0, :ATTN_WIDTH, :] = k_ref[...].T
    o_ref[0, ATTN_WIDTH:, :] = v_ref[...].T


def _kv_tail_t(qkv, g, batch, seq, tm=KV_TAIL_TOKENS):
    win = min(ATTN_PATTERNS[g][0], seq)
    tm = min(tm, win)
    first = (seq - win) // tm
    per_b = seq // tm

    def spec(which):
        return pl.BlockSpec((tm, ATTN_WIDTH), lambda b, j: (b * per_b + first + j, 3 * g + which))

    return pl.pallas_call(
        _kv_tail_body,
        out_shape=jax.ShapeDtypeStruct((batch, 2 * ATTN_WIDTH, win), F32),
        grid=(batch, win // tm),
        in_specs=[spec(1), spec(2)],
        out_specs=pl.BlockSpec((1, 2 * ATTN_WIDTH, tm), lambda b, j: (b, 0, j)),
        compiler_params=_params("parallel", "parallel"),
        name=f"kv_tail_g{g}",
    )(qkv, qkv)


SAMPLE_SEQS_PER_STEP = 2


def _attn_sample_body(slope_ref, qkv_ref, c0_ref, c1_ref, c2_ref, o_ref, *, n_new):
    caches = (c0_ref, c1_ref, c2_ref)
    rows = ATTN_HPG * n_new
    row_head = lax.broadcasted_iota(jnp.int32, (rows, ATTN_WIDTH), 0) // n_new
    lane_head = lax.broadcasted_iota(jnp.int32, (rows, ATTN_WIDTH), 1) // ATTN_HEAD_DIM
    diag = row_head == lane_head
    head_col = lax.broadcasted_iota(jnp.int32, (rows, 1), 0) // n_new
    qi_n = lax.broadcasted_iota(jnp.int32, (rows, n_new), 0) % n_new
    gap_n = qi_n - lax.broadcasted_iota(jnp.int32, (rows, n_new), 1)
    geometry = []
    for g, (c_ref, (win, dil)) in enumerate(zip(caches, ATTN_PATTERNS)):
        past_len = c_ref.shape[2]
        slope = jnp.zeros((rows, 1), F32)
        for h in range(ATTN_HPG):
            slope = jnp.where(head_col == h, slope_ref[g * ATTN_HPG + h], slope)
        gap_p = (past_len + lax.broadcasted_iota(jnp.int32, (rows, past_len), 0) % n_new
                 - lax.broadcasted_iota(jnp.int32, (rows, past_len), 1))
        ok_p = (gap_p <= win) & ((gap_p & (dil - 1)) == 0)
        ok_n = (gap_n >= 0) & ((gap_n & (dil - 1)) == 0)
        geometry.append((ok_p, slope * gap_p.astype(F32), ok_n, slope * gap_n.astype(F32)))

    for e in range(SAMPLE_SEQS_PER_STEP):
        qkv = qkv_ref[e * n_new:(e + 1) * n_new, :]
        outs, lses = [], []
        for g, c_ref in enumerate(caches):
            base = g * 3 * ATTN_WIDTH
            ok_p, bias_p, ok_n, bias_n = geometry[g]
            q = qkv[:, base:base + ATTN_WIDTH] * (ATTN_HEAD_DIM ** -0.5)
            q_bd = jnp.where(diag, jnp.concatenate([q] * ATTN_HPG, axis=0), 0.0).astype(BF16)
            k_new = qkv[:, base + ATTN_WIDTH:base + 2 * ATTN_WIDTH].astype(BF16)
            v_new = qkv[:, base + 2 * ATTN_WIDTH:base + 3 * ATTN_WIDTH].astype(BF16)
            k_t = c_ref[e, :ATTN_WIDTH, :].astype(BF16)
            v_t = c_ref[e, ATTN_WIDTH:, :].astype(BF16)
            s_p = jnp.dot(q_bd, k_t, preferred_element_type=F32) - bias_p
            s_p = jnp.where(ok_p, s_p, NEG_INF)
            s_n = lax.dot_general(q_bd, k_new, NT_DIMS, preferred_element_type=F32) - bias_n
            s_n = jnp.where(ok_n, s_n, NEG_INF)
            mx = jnp.maximum(jnp.max(s_p, axis=-1, keepdims=True), jnp.max(s_n, axis=-1, keepdims=True))
            e_p = jnp.exp(s_p - mx)
            e_n = jnp.exp(s_n - mx)
            l = jnp.sum(e_p, axis=-1, keepdims=True) + jnp.sum(e_n, axis=-1, keepdims=True)
            o = (lax.dot_general(e_p.astype(BF16), v_t, NT_DIMS, preferred_element_type=F32)
                 + jnp.dot(e_n.astype(BF16), v_new, preferred_element_type=F32)) / l
            outs.append(o)
            lses.append(mx + jnp.log(l))
        top = jnp.maximum(jnp.maximum(lses[0], lses[1]), lses[2])
        es = [jnp.exp(l - top) for l in lses]
        merged = (es[0] * outs[0] + es[1] * outs[1] + es[2] * outs[2]) / (es[0] + es[1] + es[2])
        merged = jnp.where(diag, merged, 0.0).reshape(ATTN_HPG, n_new, ATTN_WIDTH)
        o_ref[e * n_new:(e + 1) * n_new, :] = jnp.sum(merged, axis=0)


def _attn_sample(qkv, caches_t, slope_tab, batch, n_new):
    n = SAMPLE_SEQS_PER_STEP
    specs = [pl.BlockSpec((n,) + c.shape[1:], lambda b: (b, 0, 0)) for c in caches_t]
    return pl.pallas_call(
        functools.partial(_attn_sample_body, n_new=n_new),
        out_shape=jax.ShapeDtypeStruct((batch * n_new, ATTN_WIDTH), F32),
        grid=(batch // n,),
        in_specs=[pl.BlockSpec(memory_space=pltpu.SMEM),
                  pl.BlockSpec((n * n_new, QKV_WIDTH), lambda b: (b, 0))] + specs,
        out_specs=pl.BlockSpec((n * n_new, ATTN_WIDTH), lambda b: (b, 0)),
        compiler_params=_params("parallel"),
        name="attn_sample",
    )(slope_tab, qkv, *caches_t)


KV_ROWS_SEQS = 16


def _kv_rows_body(qkv_ref, *o_refs, n_new):
    for g, o_ref in enumerate(o_refs):
        lo = (3 * g + 1) * ATTN_WIDTH
        for b in range(KV_ROWS_SEQS):
            rows = qkv_ref[b * n_new:(b + 1) * n_new, lo:lo + 2 * ATTN_WIDTH]
            o_ref[0, b] = rows.reshape(n_new, 2, ATTN_HPG, ATTN_HEAD_DIM)


def _kv_rows(qkv, batch, n_new):
    shape = (1, batch, n_new, 2, ATTN_HPG, ATTN_HEAD_DIM)
    block = (1, KV_ROWS_SEQS) + shape[2:]
    return pl.pallas_call(
        functools.partial(_kv_rows_body, n_new=n_new),
        out_shape=[jax.ShapeDtypeStruct(shape, F32)] * N_ATTN_GROUPS,
        grid=(batch // KV_ROWS_SEQS,),
        in_specs=[pl.BlockSpec((KV_ROWS_SEQS * n_new, QKV_WIDTH), lambda i: (i, 0))],
        out_specs=[pl.BlockSpec(block, lambda i: (0, i, 0, 0, 0, 0))] * N_ATTN_GROUPS,
        compiler_params=_params("parallel"),
        name="kv_rows",
    )(qkv)


def _alibi_slopes():
    n_heads = N_ATTN_GROUPS * ATTN_HPG
    return [2.0 ** (-8.0 * (h + 1) / n_heads) for h in range(n_heads)]


def kernel(x_prompt, x_sample, state_conv, state_ssm, cache_kv_g0, cache_kv_g1, cache_kv_g2, norm_w, w_ffn_in,
           w_ffn_out, ssm_w_in, ssm_conv_w, ssm_conv_b, ssm_dt_bias, ssm_a_log, ssm_d, ssm_norm_w, ssm_w_out,
           attn_w_qkv, attn_w_o, norm_f):
    bp, lp, d = x_prompt.shape
    bs, ls, _ = x_sample.shape
    xs_all = [x_prompt.reshape(bp * lp, d), x_sample.reshape(bs * ls, d)]
    dims = [(bp, lp), (bs, ls)]
    hist = CONV_W - 1
    hp = SSD_HEADS * SSD_HEADDIM

    w_in = w_ffn_in.astype(BF16)
    w_out = w_ffn_out.astype(BF16)
    pad = LANES - SSD_HEADS
    w_proj = jnp.pad(ssm_w_in[0], ((0, 0), (0, pad))).astype(BF16)
    ssd_prm = (ssm_conv_w[0], ssm_conv_b[0].reshape(1, CONV_DIM),
               jnp.pad(ssm_dt_bias[0], (0, pad)).reshape(1, LANES),
               jnp.pad(ssm_a_log[0], (0, pad)).reshape(1, LANES),
               jnp.repeat(ssm_d[0], SSD_HEADDIM).reshape(1, D_INNER),
               ssm_norm_w[0].reshape(1, D_INNER))
    w_ssm_out = ssm_w_out[0].astype(BF16)
    w_qkv = attn_w_qkv[0].astype(BF16)
    w_o = attn_w_o[0].astype(BF16)
    slope_tab = jnp.asarray(_alibi_slopes(), F32)

    conv_states = [jnp.zeros((bp, hist, CONV_DIM), F32), state_conv[0]]
    ssm_states = [jnp.zeros((bp, hp, D_STATE), F32), state_ssm[0].reshape(bs, hp, D_STATE)]
    caches_t = [jnp.transpose(c[0], (0, 2, 3, 4, 1)).reshape(bs, 2 * ATTN_WIDTH, c.shape[2])
                for c in (cache_kv_g0, cache_kv_g1, cache_kv_g2)]

    conv_out, ssm_out = [], []
    for n, (x, (b, l)) in enumerate(zip(xs_all, dims)):
        x = _ffn(x, norm_w[0, 0], w_in, w_out, (0, 0))
        mixer = None
        if l % (CHUNKS_PER_STEP * SSD_CHUNK) == 0:
            x, new_conv, new_ssm = _ssd_prompt(x, norm_w[0, 1], w_proj, conv_states[n], ssm_states[n], ssd_prm,
                                               w_ssm_out, b, l)
        else:
            z, xbc, dt_raw = _norm_mm(x, norm_w[0, 1], w_proj, (D_INNER, CONV_DIM, LANES))
            gz, new_conv, new_ssm = _ssd_sample(z, xbc, dt_raw, conv_states[n], ssm_states[n], ssd_prm, b, l)
            mixer = (gz, w_ssm_out)
        conv_out.append(new_conv[None])
        ssm_out.append(new_ssm.reshape(1, b, SSD_HEADS, SSD_HEADDIM, D_STATE))
        xs_all[n] = _ffn(x, norm_w[0, 2], w_in, w_out, (0, 1), mixer=mixer)

    kv_out = []
    for n, (x, (b, l)) in enumerate(zip(xs_all, dims)):
        x = _ffn(x, norm_w[1, 0], w_in, w_out, (1, 0))
        (qkv,) = _norm_mm(x, norm_w[1, 1], w_qkv, (QKV_WIDTH,), tm=QKV_ROWS)
        if n == 0:
            o = _attn_prompt(qkv, slope_tab, b, l)
            kv_t = [_kv_tail_t(qkv, g, b, l) for g in range(N_ATTN_GROUPS)]
            kv_out.append([jnp.transpose(t.reshape(b, 2, ATTN_HPG, ATTN_HEAD_DIM, t.shape[2]), (0, 4, 1, 2, 3))[None]
                           for t in kv_t])
        else:
            o = _attn_sample(qkv, caches_t, slope_tab, b, l)
            kv_out.append(_kv_rows(qkv, b, l))
        xs_all[n] = _ffn(x, norm_w[1, 2], w_in, w_out, (1, 1), mixer=(o, w_o), g_final=norm_f)

    return (xs_all[0].reshape(bp, lp, d), xs_all[1].reshape(bs, ls, d),
            conv_out[0], conv_out[1], ssm_out[0], ssm_out[1],
            kv_out[0][0], kv_out[1][0], kv_out[0][1], kv_out[1][1], kv_out[0][2], kv_out[1][2])
```

```python
import functools

import jax
import jax.numpy as jnp
from jax import lax
from jax.experimental import pallas as pl
from jax.experimental.pallas import tpu as pltpu

F32 = jnp.float32
BF16 = jnp.bfloat16

EPS = 1e-6
NEG_INF = -1e30

D_FF = 2816
D_INNER = 2048
SSD_HEADS = 32
SSD_HEADDIM = 64
SSD_GROUPS = 8
SSD_HPG = SSD_HEADS // SSD_GROUPS
GROUP_W = SSD_HPG * SSD_HEADDIM
D_STATE = 128
CONV_W = 4
CONV_DIM = D_INNER + 2 * SSD_GROUPS * D_STATE
SSD_CHUNK = 128
ATTN_PATTERNS = ((128, 1), (512, 4), (2048, 16))
N_ATTN_GROUPS = 3
ATTN_HPG = 8
ATTN_HEAD_DIM = 64
ATTN_WIDTH = ATTN_HPG * ATTN_HEAD_DIM
QKV_WIDTH = N_ATTN_GROUPS * 3 * ATTN_WIDTH
ATTN_QBLOCK = 128
ATTN_NKEYS = 128

LANES = 128
SUBLANES = 8
VMEM_LIMIT = 56 * 1024 * 1024

FFN_ROWS = 512
PROJ_ROWS = 256
QKV_ROWS = 512
KV_TAIL_TOKENS = 512

NT_DIMS = (((1,), (1,)), ((), ()))
TN_DIMS = (((0,), (0,)), ((), ()))


def _const_spec(shape):
    zeros = (0,) * len(shape)
    return pl.BlockSpec(shape, lambda *_: zeros, pipeline_mode=pl.Buffered(1))


def _params(*semantics):
    return pltpu.CompilerParams(dimension_semantics=semantics, vmem_limit_bytes=VMEM_LIMIT)


def _rms(x, g):
    return x * lax.rsqrt(jnp.mean(x * x, axis=-1, keepdims=True) + EPS) * g


def _silu(x):
    return x * jax.nn.sigmoid(x)


def _softplus(x):
    return jnp.maximum(x, 0.0) + jnp.log1p(jnp.exp(-jnp.abs(x)))


def _ffn_rows(x_ref, mixer_refs, g_ref, win_ref, wout_ref, gf_ref, o_ref):
    x = x_ref[...]
    if mixer_refs is not None:
        y_ref, wy_ref = mixer_refs
        x = x + jnp.dot(y_ref[...].astype(BF16), wy_ref[...], preferred_element_type=F32)
    h = _rms(x, g_ref[...]).astype(BF16)
    a = jnp.dot(h, win_ref[:, :D_FF], preferred_element_type=F32)
    b = jnp.dot(h, win_ref[:, D_FF:], preferred_element_type=F32)
    t = (_silu(a) * b).astype(BF16)
    y = x + 0.5 * jnp.dot(t, wout_ref[...], preferred_element_type=F32)
    if gf_ref is not None:
        y = _rms(y, gf_ref[...])
    o_ref[...] = y


def _ffn_body(g_ref, win_ref, wout_ref, *rest, mixers, final_norm, first_steps):
    rest = list(rest)
    gf_ref = rest.pop(0) if final_norm else None
    n = len(mixers)
    ins, outs = rest[:len(rest) - n], rest[len(rest) - n:]
    step = pl.program_id(0)
    for k, has_mixer in enumerate(mixers):
        x_ref = ins.pop(0)
        mixer_refs = (ins.pop(0), ins.pop(0)) if has_mixer else None
        run = functools.partial(_ffn_rows, x_ref, mixer_refs, g_ref, win_ref, wout_ref, gf_ref, outs[k])
        if n == 1:
            run()
        else:
            pl.when((step >= first_steps[k]) & (step < first_steps[k + 1]))(run)


def _ffn(streams, g, w_in, w_out, which, g_final=None, tm=FFN_ROWS):
    d = streams[0][0].shape[1]

    def picked(w):
        return pl.BlockSpec((None, None) + w.shape[2:], lambda i: which + (0, 0), pipeline_mode=pl.Buffered(1))

    in_specs = [_const_spec((1, d)), picked(w_in), picked(w_out)]
    args = [g.reshape(1, d), w_in, w_out]
    if g_final is not None:
        in_specs.append(_const_spec((1, d)))
        args.append(g_final.reshape(1, d))
    first_steps, out_specs, out_shapes = [0], [], []
    for x, mixer in streams:
        m = x.shape[0]
        rows = min(tm, m)
        tiles = m // rows
        first = first_steps[-1]
        first_steps.append(first + tiles)

        def row_block(i, first=first, tiles=tiles):
            return (jnp.clip(i - first, 0, tiles - 1), 0)

        in_specs.append(pl.BlockSpec((rows, d), row_block))
        args.append(x)
        if mixer is not None:
            y, w_y = mixer
            in_specs += [pl.BlockSpec((rows, y.shape[1]), row_block), _const_spec(w_y.shape)]
            args += [y, w_y]
        out_specs.append(pl.BlockSpec((rows, d), row_block))
        out_shapes.append(jax.ShapeDtypeStruct((m, d), F32))
    return pl.pallas_call(
        functools.partial(_ffn_body, mixers=tuple(mx is not None for _, mx in streams),
                          final_norm=g_final is not None, first_steps=tuple(first_steps)),
        out_shape=out_shapes,
        grid=(first_steps[-1],),
        in_specs=in_specs,
        out_specs=out_specs,
        compiler_params=_params("parallel" if len(streams) == 1 else "arbitrary"),
        name="ffn",
    )(*args)


def _norm_mm_body(x_ref, g_ref, w_ref, *o_refs, splits):
    h = _rms(x_ref[...], g_ref[...]).astype(BF16)
    off = 0
    for o_ref, n in zip(o_refs, splits):
        o_ref[...] = jnp.dot(h, w_ref[:, off:off + n], preferred_element_type=F32)
        off += n


def _norm_mm(x, g, w, splits, tm=PROJ_ROWS):
    m, d = x.shape
    tm = min(tm, m)
    return pl.pallas_call(
        functools.partial(_norm_mm_body, splits=splits),
        out_shape=[jax.ShapeDtypeStruct((m, n), F32) for n in splits],
        grid=(m // tm,),
        in_specs=[pl.BlockSpec((tm, d), lambda i: (i, 0)), _const_spec((1, d)), _const_spec(w.shape)],
        out_specs=[pl.BlockSpec((tm, n), lambda i: (i, 0)) for n in splits],
        compiler_params=_params("parallel"),
        name="norm_mm",
    )(x, g.reshape(1, d), w)


def _expand_heads(v, first, width):
    rows = v.shape[0]
    return jnp.concatenate(
        [jnp.broadcast_to(v[:, first + j:first + j + 1], (rows, width)) for j in range(SSD_HPG)], axis=1)


def _split_bf16(v):
    parts, rest = [], v
    for _ in range(3):
        parts.append(rest.astype(BF16))
        rest = rest - parts[-1].astype(F32)
    return jnp.concatenate(parts, axis=1)


def _ssd_chunk(q, xpad_ref, row0, z, dt_raw, cw_ref, cb_ref, dtb_ref, alog_ref, drep_ref, ng_ref,
               h_in_ref, h_out_ref, gz_ref, spread_refs=None):
    def conv(slab):
        sl = slice(slab * LANES, (slab + 1) * LANES)
        acc = cb_ref[:, sl]
        for k in range(CONV_W):
            lo = row0 - (CONV_W - 1) + k
            acc = acc + cw_ref[k:k + 1, sl] * xpad_ref[slab, lo:lo + q, :]
        return _silu(acc)

    x_slabs = D_INNER // LANES
    bc_slabs = SSD_GROUPS * D_STATE // LANES

    dt = _softplus(dt_raw + dtb_ref[...])
    a = -jnp.exp(alog_ref[...])
    row = lax.broadcasted_iota(jnp.int32, (q, q), 0)
    col = lax.broadcasted_iota(jnp.int32, (q, q), 1)
    causal = row >= col
    if q == SUBLANES:
        a_cum = dt * a
        row_q = lax.broadcasted_iota(jnp.int32, (q, LANES), 0)
        for shift in (1, 2, 4):
            a_cum = a_cum + jnp.where(row_q >= shift, pltpu.roll(a_cum, shift, 0), 0.0)
    else:
        a_cum = jnp.dot(causal.astype(F32), dt * a, precision=lax.Precision.HIGHEST, preferred_element_type=F32)
    a_last = a_cum[q - 1:q, :]
    w_state = jnp.exp(a_last - a_cum) * dt
    e_cum = jnp.exp(a_cum)
    e_last = jnp.exp(a_last)
    a_cum_t = a_cum.T
    dt_t = dt.T
    if spread_refs is not None:
        spread64_ref, spread128_ref = spread_refs
        lane_head = lax.broadcasted_iota(jnp.int32, (q, GROUP_W), 1) // SSD_HEADDIM
        e_cum_x = jnp.dot(_split_bf16(e_cum), spread64_ref[...], preferred_element_type=F32)
        w_state_x = jnp.dot(_split_bf16(w_state), spread64_ref[...], preferred_element_type=F32)
        a_cum_x = jnp.dot(_split_bf16(a_cum), spread128_ref[...], preferred_element_type=F32)

    for g in range(SSD_GROUPS):
        h0 = g * SSD_HPG
        c0 = g * GROUP_W
        bmb = conv(x_slabs + g).astype(BF16)
        cmb = conv(x_slabs + bc_slabs + g).astype(BF16)
        cb = lax.dot_general(cmb, bmb, NT_DIMS, preferred_element_type=F32)
        x_g = jnp.concatenate([conv(c0 // LANES + i) for i in range(GROUP_W // LANES)], axis=1)
        wts = []
        for j in range(SSD_HPG):
            h = h0 + j
            a_col = a_cum[:, h:h + 1] if spread_refs is None else a_cum_x[:, h * LANES:(h + 1) * LANES]
            seg = a_col - a_cum_t[h:h + 1, :]
            decay = jnp.exp(jnp.where(causal, seg, -jnp.inf))
            wts.append((cb * decay * dt_t[h:h + 1, :]).astype(BF16))
        if spread_refs is None:
            y_diag = jnp.concatenate(
                [jnp.dot(w, x_g[:, j * SSD_HEADDIM:(j + 1) * SSD_HEADDIM].astype(BF16), preferred_element_type=F32)
                 for j, w in enumerate(wts)], axis=1)
        else:
            x_gb = x_g.astype(BF16)
            x_heads = jnp.concatenate(
                [jnp.where(lane_head == j, x_gb, jnp.zeros_like(x_gb)) for j in range(SSD_HPG)], axis=0)
            y_diag = jnp.dot(jnp.concatenate(wts, axis=1), x_heads, preferred_element_type=F32)

        h_prev = h_in_ref[c0:c0 + GROUP_W, :]
        y_off = lax.dot_general(cmb, h_prev.astype(BF16), NT_DIMS, preferred_element_type=F32)
        if spread_refs is None:
            y_off = y_off * _expand_heads(e_cum, h0, SSD_HEADDIM)
            xw = (x_g * _expand_heads(w_state, h0, SSD_HEADDIM)).astype(BF16)
        else:
            y_off = y_off * e_cum_x[:, c0:c0 + GROUP_W]
            xw = (x_g * w_state_x[:, c0:c0 + GROUP_W]).astype(BF16)
        states = lax.dot_general(xw, bmb, TN_DIMS, preferred_element_type=F32)
        carry = jnp.concatenate(
            [jnp.broadcast_to(e_last[:, h0 + j:h0 + j + 1], (SSD_HEADDIM, D_STATE)) for j in range(SSD_HPG)], axis=0)
        h_out_ref[c0:c0 + GROUP_W, :] = carry * h_prev + states

        y = y_diag + y_off + drep_ref[:, c0:c0 + GROUP_W] * x_g
        gz = y * _silu(z[:, c0:c0 + GROUP_W])
        gz = gz * lax.rsqrt(jnp.mean(gz * gz, axis=-1, keepdims=True) + EPS) * ng_ref[:, c0:c0 + GROUP_W]
        gz_ref[:, c0:c0 + GROUP_W] = gz.astype(gz_ref.dtype)


PAD_ROWS = 8


CONV_SLABS = CONV_DIM // LANES
CONV_HIST = CONV_W - 1
CHUNKS_PER_STEP = 2


def _ssd_prompt_body(x_ref, g_ref, wp_ref, cs_ref, h0_ref, cw_ref, cb_ref, dtb_ref, alog_ref, drep_ref, ng_ref,
                     wo_ref, spread64_ref, spread128_ref, o_ref, nc_ref, hout_ref, xpad_ref, gz_ref):
    q = SSD_CHUNK
    rows = CHUNKS_PER_STEP * q

    @pl.when(pl.program_id(1) == 0)
    def _():
        for j in range(CONV_SLABS):
            xpad_ref[j, PAD_ROWS - CONV_HIST:PAD_ROWS, :] = cs_ref[0, :, j * LANES:(j + 1) * LANES]
        hout_ref[0] = h0_ref[0]

    x = x_ref[...]
    h = _rms(x, g_ref[...]).astype(BF16)
    z = jnp.dot(h, wp_ref[:, :D_INNER], preferred_element_type=F32)
    xbc = jnp.dot(h, wp_ref[:, D_INNER:D_INNER + CONV_DIM], preferred_element_type=F32)
    dt_raw = jnp.dot(h, wp_ref[:, D_INNER + CONV_DIM:], preferred_element_type=F32)
    for j in range(CONV_SLABS):
        xpad_ref[j, PAD_ROWS:PAD_ROWS + rows, :] = xbc[:, j * LANES:(j + 1) * LANES]
    state = hout_ref.at[0]
    for c in range(CHUNKS_PER_STEP):
        sl = slice(c * q, (c + 1) * q)
        _ssd_chunk(q, xpad_ref, PAD_ROWS + c * q, z[sl], dt_raw[sl], cw_ref, cb_ref, dtb_ref, alog_ref, drep_ref,
                   ng_ref, state, state, gz_ref.at[pl.ds(c * q, q)], (spread64_ref, spread128_ref))
    o_ref[...] = x + jnp.dot(gz_ref[...], wo_ref[...], preferred_element_type=F32)
    for j in range(CONV_SLABS):
        tail = xpad_ref[j, PAD_ROWS + rows - CONV_HIST:PAD_ROWS + rows, :]
        nc_ref[0, :, j * LANES:(j + 1) * LANES] = tail
        xpad_ref[j, PAD_ROWS - CONV_HIST:PAD_ROWS, :] = tail


def _ssd_prompt(x, g, w_proj, conv_state, ssm_state, prm, w_out, batch, seq):
    rows = CHUNKS_PER_STEP * SSD_CHUNK
    steps = seq // rows
    d = x.shape[1]
    hp = SSD_HEADS * SSD_HEADDIM
    row = lambda b, c: (b * steps + c, 0)
    per_b = lambda b, c: (b, 0, 0)
    head_of_row = jnp.arange(3 * LANES)[:, None] % LANES
    spreads = [(head_of_row == jnp.arange(SSD_HEADS * w)[None, :] // w).astype(BF16) for w in (SSD_HEADDIM, LANES)]
    consts = list(prm) + [w_out] + spreads
    return pl.pallas_call(
        _ssd_prompt_body,
        out_shape=[jax.ShapeDtypeStruct(x.shape, F32),
                   jax.ShapeDtypeStruct((batch, CONV_HIST, CONV_DIM), F32),
                   jax.ShapeDtypeStruct((batch, hp, D_STATE), F32)],
        grid=(batch, steps),
        in_specs=[pl.BlockSpec((rows, d), row), _const_spec((1, d)), _const_spec(w_proj.shape),
                  pl.BlockSpec((1, CONV_HIST, CONV_DIM), per_b), pl.BlockSpec((1, hp, D_STATE), per_b)]
        + [_const_spec(c.shape) for c in consts],
        out_specs=[pl.BlockSpec((rows, d), row), pl.BlockSpec((1, CONV_HIST, CONV_DIM), per_b),
                   pl.BlockSpec((1, hp, D_STATE), per_b)],
        scratch_shapes=[pltpu.VMEM((CONV_SLABS, PAD_ROWS + rows, LANES), F32), pltpu.VMEM((rows, D_INNER), BF16)],
        compiler_params=_params("parallel", "arbitrary"),
        name="ssd_prompt",
    )(x, g.reshape(1, d), w_proj, conv_state, ssm_state, *consts)


SEQS_PER_STEP = 2


def _ssd_sample_body(z_ref, xbc_ref, dt_ref, cs_ref, h0_ref, cw_ref, cb_ref, dtb_ref, alog_ref, drep_ref, ng_ref,
                     gz_ref, nc_ref, hout_ref, xpad_ref, *, q):
    span = PAD_ROWS + q
    for e in range(SEQS_PER_STEP):
        row0 = e * span + PAD_ROWS
        sl = slice(e * q, (e + 1) * q)
        for j in range(CONV_SLABS):
            lanes = slice(j * LANES, (j + 1) * LANES)
            xpad_ref[j, row0 - CONV_HIST:row0, :] = cs_ref[e, :, lanes]
            xpad_ref[j, row0:row0 + q, :] = xbc_ref[sl, lanes]
        _ssd_chunk(q, xpad_ref, row0, z_ref[sl, :], dt_ref[sl, :], cw_ref, cb_ref, dtb_ref, alog_ref, drep_ref,
                   ng_ref, h0_ref.at[e], hout_ref.at[e], gz_ref.at[pl.ds(e * q, q)])
        for j in range(CONV_SLABS):
            nc_ref[e, :, j * LANES:(j + 1) * LANES] = xpad_ref[j, row0 + q - CONV_HIST:row0 + q, :]


def _ssd_sample(z, xbc, dt_raw, conv_state, ssm_state, prm, batch, seq):
    n = SEQS_PER_STEP
    hp = SSD_HEADS * SSD_HEADDIM
    row = lambda b: (b, 0)
    per_b = lambda b: (b, 0, 0)
    return pl.pallas_call(
        functools.partial(_ssd_sample_body, q=seq),
        out_shape=[jax.ShapeDtypeStruct((batch * seq, D_INNER), F32),
                   jax.ShapeDtypeStruct((batch, CONV_HIST, CONV_DIM), F32),
                   jax.ShapeDtypeStruct((batch, hp, D_STATE), F32)],
        grid=(batch // n,),
        in_specs=[pl.BlockSpec((n * seq, D_INNER), row), pl.BlockSpec((n * seq, CONV_DIM), row),
                  pl.BlockSpec((n * seq, LANES), row),
                  pl.BlockSpec((n, CONV_HIST, CONV_DIM), per_b), pl.BlockSpec((n, hp, D_STATE), per_b)]
        + [_const_spec(c.shape) for c in prm],
        out_specs=[pl.BlockSpec((n * seq, D_INNER), row), pl.BlockSpec((n, CONV_HIST, CONV_DIM), per_b),
                   pl.BlockSpec((n, hp, D_STATE), per_b)],
        scratch_shapes=[pltpu.VMEM((CONV_SLABS, n * (PAD_ROWS + seq), LANES), F32)],
        compiler_params=_params("parallel"),
        name="ssd_sample",
    )(z, xbc, dt_raw, conv_state, ssm_state, *prm)


ATTN_SLAB = ATTN_QBLOCK * 16
HEAD_PAIRS = ATTN_HPG // 2


def _attn_prompt_body(slope_ref, *refs):
    qkv_refs = refs[:9]
    o_ref = refs[9]
    m_sc, l_sc, acc_sc = refs[10:13]
    kv_ext = refs[13:]
    pair = pl.program_id(1)
    s_idx = pl.program_id(2)
    qb = ATTN_QBLOCK

    for g, (_, dil) in enumerate(ATTN_PATTERNS):
        sub = ATTN_SLAB // dil
        for ext, src in zip(kv_ext[2 * g:2 * g + 2], qkv_refs[3 * g + 1:3 * g + 3]):
            @pl.when(s_idx == 0)
            def _(ext=ext, dil=dil):
                ext[:, 0:qb, :] = jnp.zeros((dil, qb, LANES), BF16)

            for r in range(dil):
                picked = src[...] if dil == 1 else src[pl.ds(r, sub, stride=dil), :]
                ext[r, qb:qb + sub, :] = picked.astype(BF16)

    row = lax.broadcasted_iota(jnp.int32, (qb, 2 * qb), 0)
    col = lax.broadcasted_iota(jnp.int32, (qb, 2 * qb), 1)
    dist = qb + row - col
    in_window = (dist >= 0) & (dist <= ATTN_NKEYS)
    valid_first = in_window & (col >= jnp.where(s_idx > 0, 0, qb))
    dist_f = dist.astype(F32)
    lane = lax.broadcasted_iota(jnp.int32, (qb, LANES), 1)
    low_half = lane < ATTN_HEAD_DIM

    for g, (_, dil) in enumerate(ATTN_PATTERNS):
        q_ref = qkv_refs[3 * g]
        k_ext, v_ext = kv_ext[2 * g:2 * g + 2]
        n_blk = ATTN_SLAB // (qb * dil)
        slopes = [slope_ref[g * ATTN_HPG + 2 * pair + e] * float(dil) for e in range(2)]
        bias_in = [jnp.where(in_window, -slopes[e] * dist_f, NEG_INF) for e in range(2)]
        bias_first = [jnp.where(valid_first, -slopes[e] * dist_f, NEG_INF) for e in range(2)]

        for r, blk in [(r, blk) for r in range(dil) for blk in range(n_blk)]:
            start = r + (dil * qb) * blk
            rows = pl.ds(start, qb) if dil == 1 else pl.ds(start, qb, stride=dil)
            k2 = k_ext[r, blk * qb:(blk + 2) * qb, :]
            v2 = v_ext[r, blk * qb:(blk + 2) * qb, :]
            q2 = q_ref[rows, :] * (ATTN_HEAD_DIM ** -0.5)
            bias = bias_in if blk > 0 else bias_first
            ms, ls, pvs = [], [], []
            for e in range(2):
                qm = jnp.where(low_half if e == 0 else ~low_half, q2, 0.0).astype(BF16)
                s = lax.dot_general(qm, k2, NT_DIMS, preferred_element_type=F32) + bias[e]
                m = jnp.max(s, axis=-1, keepdims=True)
                p = jnp.exp(s - m)
                ms.append(m)
                ls.append(jnp.sum(p, axis=-1, keepdims=True))
                pvs.append(jnp.dot(p.astype(BF16), v2, preferred_element_type=F32))
            m_new = jnp.where(low_half, ms[0], ms[1])
            l_new = jnp.where(low_half, ls[0], ls[1])
            pv_new = jnp.where(low_half, pvs[0], pvs[1])
            if g > 0:
                m_old = m_sc[rows, :]
                m_tot = jnp.maximum(m_old, m_new)
                a_old = jnp.exp(m_old - m_tot)
                a_new = jnp.exp(m_new - m_tot)
                l_new = a_old * l_sc[rows, :] + a_new * l_new
                pv_new = a_old * acc_sc[rows, :] + a_new * pv_new
                m_new = m_tot
            if g == N_ATTN_GROUPS - 1:
                o_ref[rows, :] = pv_new / l_new
            else:
                m_sc[rows, :] = m_new
                l_sc[rows, :] = l_new
                acc_sc[rows, :] = pv_new

    for g, (_, dil) in enumerate(ATTN_PATTERNS):
        sub = ATTN_SLAB // dil
        for ext in kv_ext[2 * g:2 * g + 2]:
            ext[:, 0:qb, :] = ext[:, sub:sub + qb, :]


def _attn_prompt(qkv, slope_tab, batch, seq):
    n_slab = seq // ATTN_SLAB
    col_blocks = ATTN_WIDTH // LANES

    def spec(g, which):
        base = (3 * g + which) * col_blocks
        return pl.BlockSpec((ATTN_SLAB, LANES), lambda b, p, s: (b * n_slab + s, base + p))

    in_specs = [pl.BlockSpec(memory_space=pltpu.SMEM)]
    in_specs += [spec(g, which) for g in range(N_ATTN_GROUPS) for which in range(3)]
    return pl.pallas_call(
        _attn_prompt_body,
        out_shape=jax.ShapeDtypeStruct((batch * seq, ATTN_WIDTH), F32),
        grid=(batch, HEAD_PAIRS, n_slab),
        in_specs=in_specs,
        out_specs=pl.BlockSpec((ATTN_SLAB, LANES), lambda b, p, s: (b * n_slab + s, p)),
        scratch_shapes=[pltpu.VMEM((ATTN_SLAB, LANES), F32)] * 3 + [
            pltpu.VMEM((dil, ATTN_QBLOCK + ATTN_SLAB // dil, LANES), BF16)
            for _, dil in ATTN_PATTERNS for _ in range(2)],
        compiler_params=_params("parallel", "parallel", "arbitrary"),
        name="attn_prompt",
    )(slope_tab, *([qkv] * 9))


def _kv_tail_body(k_ref, v_ref, o_ref):
    o_ref[0, :ATTN_WIDTH, :] = k_ref[...].T
    o_ref[0, ATTN_WIDTH:, :] = v_ref[...].T


def _kv_tail_t(qkv, g, batch, seq, tm=KV_TAIL_TOKENS):
    win = min(ATTN_PATTERNS[g][0], seq)
    tm = min(tm, win)
    first = (seq - win) // tm
    per_b = seq // tm

    def spec(which):
        return pl.BlockSpec((tm, ATTN_WIDTH), lambda b, j: (b * per_b + first + j, 3 * g + which))

    return pl.pallas_call(
        _kv_tail_body,
        out_shape=jax.ShapeDtypeStruct((batch, 2 * ATTN_WIDTH, win), F32),
        grid=(batch, win // tm),
        in_specs=[spec(1), spec(2)],
        out_specs=pl.BlockSpec((1, 2 * ATTN_WIDTH, tm), lambda b, j: (b, 0, j)),
        compiler_params=_params("parallel", "parallel"),
        name=f"kv_tail_g{g}",
    )(qkv, qkv)


SAMPLE_SEQS_PER_STEP = 2


def _attn_sample_body(slope_ref, qkv_ref, c0_ref, c1_ref, c2_ref, o_ref, *, n_new):
    caches = (c0_ref, c1_ref, c2_ref)
    rows = ATTN_HPG * n_new
    row_head = lax.broadcasted_iota(jnp.int32, (rows, ATTN_WIDTH), 0) // n_new
    lane_head = lax.broadcasted_iota(jnp.int32, (rows, ATTN_WIDTH), 1) // ATTN_HEAD_DIM
    diag = row_head == lane_head
    head_col = lax.broadcasted_iota(jnp.int32, (rows, 1), 0) // n_new
    qi_n = lax.broadcasted_iota(jnp.int32, (rows, n_new), 0) % n_new
    gap_n = qi_n - lax.broadcasted_iota(jnp.int32, (rows, n_new), 1)
    geometry = []
    for g, (c_ref, (win, dil)) in enumerate(zip(caches, ATTN_PATTERNS)):
        past_len = c_ref.shape[2]
        slope = jnp.zeros((rows, 1), F32)
        for h in range(ATTN_HPG):
            slope = jnp.where(head_col == h, slope_ref[g * ATTN_HPG + h], slope)
        gap_p = (past_len + lax.broadcasted_iota(jnp.int32, (rows, past_len), 0) % n_new
                 - lax.broadcasted_iota(jnp.int32, (rows, past_len), 1))
        ok_p = (gap_p <= win) & ((gap_p & (dil - 1)) == 0)
        ok_n = (gap_n >= 0) & ((gap_n & (dil - 1)) == 0)
        geometry.append((ok_p, slope * gap_p.astype(F32), ok_n, slope * gap_n.astype(F32)))

    for e in range(SAMPLE_SEQS_PER_STEP):
        qkv = qkv_ref[e * n_new:(e + 1) * n_new, :]
        outs, lses = [], []
        for g, c_ref in enumerate(caches):
            base = g * 3 * ATTN_WIDTH
            ok_p, bias_p, ok_n, bias_n = geometry[g]
            q = qkv[:, base:base + ATTN_WIDTH] * (ATTN_HEAD_DIM ** -0.5)
            q_bd = jnp.where(diag, jnp.concatenate([q] * ATTN_HPG, axis=0), 0.0).astype(BF16)
            k_new = qkv[:, base + ATTN_WIDTH:base + 2 * ATTN_WIDTH].astype(BF16)
            v_new = qkv[:, base + 2 * ATTN_WIDTH:base + 3 * ATTN_WIDTH].astype(BF16)
            k_t = c_ref[e, :ATTN_WIDTH, :].astype(BF16)
            v_t = c_ref[e, ATTN_WIDTH:, :].astype(BF16)
            s_p = jnp.dot(q_bd, k_t, preferred_element_type=F32) - bias_p
            s_p = jnp.where(ok_p, s_p, NEG_INF)
            s_n = lax.dot_general(q_bd, k_new, NT_DIMS, preferred_element_type=F32) - bias_n
            s_n = jnp.where(ok_n, s_n, NEG_INF)
            mx = jnp.maximum(jnp.max(s_p, axis=-1, keepdims=True), jnp.max(s_n, axis=-1, keepdims=True))
            e_p = jnp.exp(s_p - mx)
            e_n = jnp.exp(s_n - mx)
            l = jnp.sum(e_p, axis=-1, keepdims=True) + jnp.sum(e_n, axis=-1, keepdims=True)
            o = (lax.dot_general(e_p.astype(BF16), v_t, NT_DIMS, preferred_element_type=F32)
                 + jnp.dot(e_n.astype(BF16), v_new, preferred_element_type=F32)) / l
            outs.append(o)
            lses.append(mx + jnp.log(l))
        top = jnp.maximum(jnp.maximum(lses[0], lses[1]), lses[2])
        es = [jnp.exp(l - top) for l in lses]
        merged = (es[0] * outs[0] + es[1] * outs[1] + es[2] * outs[2]) / (es[0] + es[1] + es[2])
        merged = jnp.where(diag, merged, 0.0).reshape(ATTN_HPG, n_new, ATTN_WIDTH)
        o_ref[e * n_new:(e + 1) * n_new, :] = jnp.sum(merged, axis=0)


def _attn_sample(qkv, caches_t, slope_tab, batch, n_new):
    n = SAMPLE_SEQS_PER_STEP
    specs = [pl.BlockSpec((n,) + c.shape[1:], lambda b: (b, 0, 0)) for c in caches_t]
    return pl.pallas_call(
        functools.partial(_attn_sample_body, n_new=n_new),
        out_shape=jax.ShapeDtypeStruct((batch * n_new, ATTN_WIDTH), F32),
        grid=(batch // n,),
        in_specs=[pl.BlockSpec(memory_space=pltpu.SMEM),
                  pl.BlockSpec((n * n_new, QKV_WIDTH), lambda b: (b, 0))] + specs,
        out_specs=pl.BlockSpec((n * n_new, ATTN_WIDTH), lambda b: (b, 0)),
        compiler_params=_params("parallel"),
        name="attn_sample",
    )(slope_tab, qkv, *caches_t)


KV_ROWS_SEQS = 16


def _kv_rows_body(qkv_ref, *o_refs, n_new):
    for g, o_ref in enumerate(o_refs):
        lo = (3 * g + 1) * ATTN_WIDTH
        for b in range(KV_ROWS_SEQS):
            rows = qkv_ref[b * n_new:(b + 1) * n_new, lo:lo + 2 * ATTN_WIDTH]
            o_ref[0, b] = rows.reshape(n_new, 2, ATTN_HPG, ATTN_HEAD_DIM)


def _kv_rows(qkv, batch, n_new):
    shape = (1, batch, n_new, 2, ATTN_HPG, ATTN_HEAD_DIM)
    block = (1, KV_ROWS_SEQS) + shape[2:]
    return pl.pallas_call(
        functools.partial(_kv_rows_body, n_new=n_new),
        out_shape=[jax.ShapeDtypeStruct(shape, F32)] * N_ATTN_GROUPS,
        grid=(batch // KV_ROWS_SEQS,),
        in_specs=[pl.BlockSpec((KV_ROWS_SEQS * n_new, QKV_WIDTH), lambda i: (i, 0))],
        out_specs=[pl.BlockSpec(block, lambda i: (0, i, 0, 0, 0, 0))] * N_ATTN_GROUPS,
        compiler_params=_params("parallel"),
        name="kv_rows",
    )(qkv)


def _alibi_slopes():
    n_heads = N_ATTN_GROUPS * ATTN_HPG
    return [2.0 ** (-8.0 * (h + 1) / n_heads) for h in range(n_heads)]


def kernel(x_prompt, x_sample, state_conv, state_ssm, cache_kv_g0, cache_kv_g1, cache_kv_g2, norm_w, w_ffn_in,
           w_ffn_out, ssm_w_in, ssm_conv_w, ssm_conv_b, ssm_dt_bias, ssm_a_log, ssm_d, ssm_norm_w, ssm_w_out,
           attn_w_qkv, attn_w_o, norm_f):
    bp, lp, d = x_prompt.shape
    bs, ls, _ = x_sample.shape
    xp = x_prompt.reshape(bp * lp, d)
    xs = x_sample.reshape(bs * ls, d)
    hp = SSD_HEADS * SSD_HEADDIM

    w_in = w_ffn_in.astype(BF16)
    w_out = w_ffn_out.astype(BF16)
    pad = LANES - SSD_HEADS
    w_proj = jnp.pad(ssm_w_in[0], ((0, 0), (0, pad))).astype(BF16)
    ssd_prm = (ssm_conv_w[0], ssm_conv_b[0].reshape(1, CONV_DIM),
               jnp.pad(ssm_dt_bias[0], (0, pad)).reshape(1, LANES),
               jnp.pad(ssm_a_log[0], (0, pad)).reshape(1, LANES),
               jnp.repeat(ssm_d[0], SSD_HEADDIM).reshape(1, D_INNER),
               ssm_norm_w[0].reshape(1, D_INNER))
    w_ssm_out = ssm_w_out[0].astype(BF16)
    w_qkv = attn_w_qkv[0].astype(BF16)
    w_o = attn_w_o[0].astype(BF16)
    slope_tab = jnp.asarray(_alibi_slopes(), F32)

    caches_t = [jnp.transpose(c[0], (0, 2, 3, 4, 1)).reshape(bs, 2 * ATTN_WIDTH, c.shape[2])
                for c in (cache_kv_g0, cache_kv_g1, cache_kv_g2)]

    xp, xs = _ffn([(xp, None), (xs, None)], norm_w[0, 0], w_in, w_out, (0, 0))
    xp, conv_p, ssm_p = _ssd_prompt(xp, norm_w[0, 1], w_proj, jnp.zeros((bp, CONV_HIST, CONV_DIM), F32),
                                    jnp.zeros((bp, hp, D_STATE), F32), ssd_prm, w_ssm_out, bp, lp)
    z, xbc, dt_raw = _norm_mm(xs, norm_w[0, 1], w_proj, (D_INNER, CONV_DIM, LANES))
    gz, conv_s, ssm_s = _ssd_sample(z, xbc, dt_raw, state_conv[0], state_ssm[0].reshape(bs, hp, D_STATE), ssd_prm,
                                    bs, ls)
    xp, xs = _ffn([(xp, None), (xs, (gz, w_ssm_out))], norm_w[0, 2], w_in, w_out, (0, 1))

    xp, xs = _ffn([(xp, None), (xs, None)], norm_w[1, 0], w_in, w_out, (1, 0))
    (qkv_p,) = _norm_mm(xp, norm_w[1, 1], w_qkv, (QKV_WIDTH,), tm=QKV_ROWS)
    (qkv_s,) = _norm_mm(xs, norm_w[1, 1], w_qkv, (QKV_WIDTH,), tm=QKV_ROWS)
    o_p = _attn_prompt(qkv_p, slope_tab, bp, lp)
    o_s = _attn_sample(qkv_s, caches_t, slope_tab, bs, ls)
    kv_p = [jnp.transpose(t.reshape(bp, 2, ATTN_HPG, ATTN_HEAD_DIM, t.shape[2]), (0, 4, 1, 2, 3))[None]
            for t in (_kv_tail_t(qkv_p, g, bp, lp) for g in range(N_ATTN_GROUPS))]
    kv_s = _kv_rows(qkv_s, bs, ls)
    yp, ys = _ffn([(xp, (o_p, w_o)), (xs, (o_s, w_o))], norm_w[1, 2], w_in, w_out, (1, 1), g_final=norm_f)

    return (yp.reshape(bp, lp, d), ys.reshape(bs, ls, d),
            conv_p[None], conv_s[None],
            ssm_p.reshape(1, bp, SSD_HEADS, SSD_HEADDIM, D_STATE), ssm_s.reshape(1, bs, SSD_HEADS, SSD_HEADDIM, D_STATE),
            kv_p[0], kv_s[0], kv_p[1], kv_s[1], kv_p[2], kv_s[2])
```

```python
import functools

import jax
import jax.numpy as jnp
from jax import lax
from jax.experimental import pallas as pl
from jax.experimental.pallas import tpu as pltpu

F32 = jnp.float32
BF16 = jnp.bfloat16

EPS = 1e-6
NEG_INF = -1e30

D_FF = 2816
D_INNER = 2048
SSD_HEADS = 32
SSD_HEADDIM = 64
SSD_GROUPS = 8
SSD_HPG = SSD_HEADS // SSD_GROUPS
GROUP_W = SSD_HPG * SSD_HEADDIM
D_STATE = 128
CONV_W = 4
CONV_DIM = D_INNER + 2 * SSD_GROUPS * D_STATE
SSD_CHUNK = 128
ATTN_PATTERNS = ((128, 1), (512, 4), (2048, 16))
N_ATTN_GROUPS = 3
ATTN_HPG = 8
ATTN_HEAD_DIM = 64
ATTN_WIDTH = ATTN_HPG * ATTN_HEAD_DIM
QKV_WIDTH = N_ATTN_GROUPS * 3 * ATTN_WIDTH
ATTN_QBLOCK = 128
ATTN_NKEYS = 128

LANES = 128
SUBLANES = 8
VMEM_LIMIT = 56 * 1024 * 1024

FFN_ROWS = 512
PROJ_ROWS = 256
QKV_ROWS = 512
KV_TAIL_TOKENS = 512

NT_DIMS = (((1,), (1,)), ((), ()))
TN_DIMS = (((0,), (0,)), ((), ()))


def _const_spec(shape):
    zeros = (0,) * len(shape)
    return pl.BlockSpec(shape, lambda *_: zeros, pipeline_mode=pl.Buffered(1))


def _params(*semantics):
    return pltpu.CompilerParams(dimension_semantics=semantics, vmem_limit_bytes=VMEM_LIMIT)


def _rms(x, g):
    return x * lax.rsqrt(jnp.mean(x * x, axis=-1, keepdims=True) + EPS) * g


def _silu(x):
    return x * jax.nn.sigmoid(x)


def _softplus(x):
    return jnp.maximum(x, 0.0) + jnp.log1p(jnp.exp(-jnp.abs(x)))


def _ffn_body(x_ref, g_ref, win_ref, wout_ref, *rest, mixer_proj, final_norm):
    rest = list(rest)
    o_ref = rest.pop()
    x = x_ref[...]
    if mixer_proj:
        y_ref, wy_ref = rest[:2]
        x = x + jnp.dot(y_ref[...].astype(BF16), wy_ref[...], preferred_element_type=F32)
    if final_norm:
        gf_ref = rest[-1]
    h = _rms(x, g_ref[...]).astype(BF16)
    a = jnp.dot(h, win_ref[:, :D_FF], preferred_element_type=F32)
    b = jnp.dot(h, win_ref[:, D_FF:], preferred_element_type=F32)
    t = (_silu(a) * b).astype(BF16)
    y = x + 0.5 * jnp.dot(t, wout_ref[...], preferred_element_type=F32)
    if final_norm:
        y = _rms(y, gf_ref[...])
    o_ref[...] = y


def _ffn(x, g, w_in, w_out, which, mixer=None, g_final=None, tm=FFN_ROWS):
    m, d = x.shape
    tm = min(tm, m)
    row = pl.BlockSpec((tm, d), lambda i: (i, 0))

    def picked(w):
        return pl.BlockSpec((None, None) + w.shape[2:], lambda i: which + (0, 0), pipeline_mode=pl.Buffered(1))

    in_specs = [row, _const_spec((1, d)), picked(w_in), picked(w_out)]
    args = [x, g.reshape(1, d), w_in, w_out]
    if mixer is not None:
        y, w_y = mixer
        in_specs += [pl.BlockSpec((tm, y.shape[1]), lambda i: (i, 0)), _const_spec(w_y.shape)]
        args += [y, w_y]
    if g_final is not None:
        in_specs.append(_const_spec((1, d)))
        args.append(g_final.reshape(1, d))
    return pl.pallas_call(
        functools.partial(_ffn_body, mixer_proj=mixer is not None, final_norm=g_final is not None),
        out_shape=jax.ShapeDtypeStruct((m, d), F32),
        grid=(m // tm,),
        in_specs=in_specs,
        out_specs=row,
        compiler_params=_params("parallel"),
        name="ffn",
    )(*args)


def _norm_mm_body(x_ref, g_ref, w_ref, *o_refs, splits):
    h = _rms(x_ref[...], g_ref[...]).astype(BF16)
    off = 0
    for o_ref, n in zip(o_refs, splits):
        o_ref[...] = jnp.dot(h, w_ref[:, off:off + n], preferred_element_type=F32)
        off += n


def _norm_mm(x, g, w, splits, tm=PROJ_ROWS):
    m, d = x.shape
    tm = min(tm, m)
    return pl.pallas_call(
        functools.partial(_norm_mm_body, splits=splits),
        out_shape=[jax.ShapeDtypeStruct((m, n), F32) for n in splits],
        grid=(m // tm,),
        in_specs=[pl.BlockSpec((tm, d), lambda i: (i, 0)), _const_spec((1, d)), _const_spec(w.shape)],
        out_specs=[pl.BlockSpec((tm, n), lambda i: (i, 0)) for n in splits],
        compiler_params=_params("parallel"),
        name="norm_mm",
    )(x, g.reshape(1, d), w)


def _expand_heads(v, first, width):
    rows = v.shape[0]
    return jnp.concatenate(
        [jnp.broadcast_to(v[:, first + j:first + j + 1], (rows, width)) for j in range(SSD_HPG)], axis=1)


def _split_bf16(v):
    parts, rest = [], v
    for _ in range(3):
        parts.append(rest.astype(BF16))
        rest = rest - parts[-1].astype(F32)
    return jnp.concatenate(parts, axis=1)


def _ssd_chunk(q, xpad_ref, row0, z, dt_raw, cw_ref, cb_ref, dtb_ref, alog_ref, drep_ref, ng_ref,
               h_in_ref, h_out_ref, gz_ref, spread_ref=None):
    def conv(slab):
        sl = slice(slab * LANES, (slab + 1) * LANES)
        acc = cb_ref[:, sl]
        for k in range(CONV_W):
            lo = row0 - (CONV_W - 1) + k
            acc = acc + cw_ref[k:k + 1, sl] * xpad_ref[slab, lo:lo + q, :]
        return _silu(acc)

    x_slabs = D_INNER // LANES
    bc_slabs = SSD_GROUPS * D_STATE // LANES

    dt = _softplus(dt_raw + dtb_ref[...])
    a = -jnp.exp(alog_ref[...])
    row = lax.broadcasted_iota(jnp.int32, (q, q), 0)
    col = lax.broadcasted_iota(jnp.int32, (q, q), 1)
    causal = row >= col
    if q == SUBLANES:
        a_cum = dt * a
        row_q = lax.broadcasted_iota(jnp.int32, (q, LANES), 0)
        for shift in (1, 2, 4):
            a_cum = a_cum + jnp.where(row_q >= shift, pltpu.roll(a_cum, shift, 0), 0.0)
    else:
        a_cum = jnp.dot(causal.astype(F32), dt * a, precision=lax.Precision.HIGHEST, preferred_element_type=F32)
    a_last = a_cum[q - 1:q, :]
    w_state = jnp.exp(a_last - a_cum) * dt
    e_cum = jnp.exp(a_cum)
    e_last = jnp.exp(a_last)
    a_cum_t = a_cum.T
    dt_t = dt.T
    if spread_ref is not None:
        lane_head = lax.broadcasted_iota(jnp.int32, (q, GROUP_W), 1) // SSD_HEADDIM
        e_cum_x = jnp.dot(_split_bf16(e_cum), spread_ref[...], preferred_element_type=F32)
        w_state_x = jnp.dot(_split_bf16(w_state), spread_ref[...], preferred_element_type=F32)

    for g in range(SSD_GROUPS):
        h0 = g * SSD_HPG
        c0 = g * GROUP_W
        bmb = conv(x_slabs + g).astype(BF16)
        cmb = conv(x_slabs + bc_slabs + g).astype(BF16)
        cb = lax.dot_general(cmb, bmb, NT_DIMS, preferred_element_type=F32)
        x_g = jnp.concatenate([conv(c0 // LANES + i) for i in range(GROUP_W // LANES)], axis=1)
        wts = []
        for j in range(SSD_HPG):
            h = h0 + j
            seg = a_cum[:, h:h + 1] - a_cum_t[h:h + 1, :]
            decay = jnp.exp(jnp.where(causal, seg, -jnp.inf))
            wts.append((cb * decay * dt_t[h:h + 1, :]).astype(BF16))
        if spread_ref is None:
            y_diag = jnp.concatenate(
                [jnp.dot(w, x_g[:, j * SSD_HEADDIM:(j + 1) * SSD_HEADDIM].astype(BF16), preferred_element_type=F32)
                 for j, w in enumerate(wts)], axis=1)
        else:
            x_gb = x_g.astype(BF16)
            x_heads = jnp.concatenate(
                [jnp.where(lane_head == j, x_gb, jnp.zeros_like(x_gb)) for j in range(SSD_HPG)], axis=0)
            y_diag = jnp.dot(jnp.concatenate(wts, axis=1), x_heads, preferred_element_type=F32)

        h_prev = h_in_ref[c0:c0 + GROUP_W, :]
        y_off = lax.dot_general(cmb, h_prev.astype(BF16), NT_DIMS, preferred_element_type=F32)
        if spread_ref is None:
            y_off = y_off * _expand_heads(e_cum, h0, SSD_HEADDIM)
            xw = (x_g * _expand_heads(w_state, h0, SSD_HEADDIM)).astype(BF16)
        else:
            y_off = y_off * e_cum_x[:, c0:c0 + GROUP_W]
            xw = (x_g * w_state_x[:, c0:c0 + GROUP_W]).astype(BF16)
        states = lax.dot_general(xw, bmb, TN_DIMS, preferred_element_type=F32)
        carry = jnp.concatenate(
            [jnp.broadcast_to(e_last[:, h0 + j:h0 + j + 1], (SSD_HEADDIM, D_STATE)) for j in range(SSD_HPG)], axis=0)
        h_out_ref[c0:c0 + GROUP_W, :] = carry * h_prev + states

        y = y_diag + y_off + drep_ref[:, c0:c0 + GROUP_W] * x_g
        gz = y * _silu(z[:, c0:c0 + GROUP_W])
        gz = gz * lax.rsqrt(jnp.mean(gz * gz, axis=-1, keepdims=True) + EPS) * ng_ref[:, c0:c0 + GROUP_W]
        gz_ref[:, c0:c0 + GROUP_W] = gz.astype(gz_ref.dtype)


PAD_ROWS = 8


CONV_SLABS = CONV_DIM // LANES
CONV_HIST = CONV_W - 1
CHUNKS_PER_STEP = 2


def _ssd_prompt_body(x_ref, g_ref, wp_ref, cs_ref, h0_ref, cw_ref, cb_ref, dtb_ref, alog_ref, drep_ref, ng_ref,
                     wo_ref, spread_ref, o_ref, nc_ref, hout_ref, xpad_ref, gz_ref):
    q = SSD_CHUNK
    rows = CHUNKS_PER_STEP * q

    @pl.when(pl.program_id(1) == 0)
    def _():
        for j in range(CONV_SLABS):
            xpad_ref[j, PAD_ROWS - CONV_HIST:PAD_ROWS, :] = cs_ref[0, :, j * LANES:(j + 1) * LANES]
        hout_ref[0] = h0_ref[0]

    x = x_ref[...]
    h = _rms(x, g_ref[...]).astype(BF16)
    z = jnp.dot(h, wp_ref[:, :D_INNER], preferred_element_type=F32)
    xbc = jnp.dot(h, wp_ref[:, D_INNER:D_INNER + CONV_DIM], preferred_element_type=F32)
    dt_raw = jnp.dot(h, wp_ref[:, D_INNER + CONV_DIM:], preferred_element_type=F32)
    for j in range(CONV_SLABS):
        xpad_ref[j, PAD_ROWS:PAD_ROWS + rows, :] = xbc[:, j * LANES:(j + 1) * LANES]
    state = hout_ref.at[0]
    for c in range(CHUNKS_PER_STEP):
        sl = slice(c * q, (c + 1) * q)
        _ssd_chunk(q, xpad_ref, PAD_ROWS + c * q, z[sl], dt_raw[sl], cw_ref, cb_ref, dtb_ref, alog_ref, drep_ref,
                   ng_ref, state, state, gz_ref.at[pl.ds(c * q, q)], spread_ref)
    o_ref[...] = x + jnp.dot(gz_ref[...], wo_ref[...], preferred_element_type=F32)
    for j in range(CONV_SLABS):
        tail = xpad_ref[j, PAD_ROWS + rows - CONV_HIST:PAD_ROWS + rows, :]
        nc_ref[0, :, j * LANES:(j + 1) * LANES] = tail
        xpad_ref[j, PAD_ROWS - CONV_HIST:PAD_ROWS, :] = tail


def _ssd_prompt(x, g, w_proj, conv_state, ssm_state, prm, w_out, batch, seq):
    rows = CHUNKS_PER_STEP * SSD_CHUNK
    steps = seq // rows
    d = x.shape[1]
    hp = SSD_HEADS * SSD_HEADDIM
    row = lambda b, c: (b * steps + c, 0)
    per_b = lambda b, c: (b, 0, 0)
    head_of_row = jnp.arange(3 * LANES)[:, None] % LANES
    spread = (head_of_row == jnp.arange(SSD_HEADS * SSD_HEADDIM)[None, :] // SSD_HEADDIM).astype(BF16)
    consts = list(prm) + [w_out, spread]
    return pl.pallas_call(
        _ssd_prompt_body,
        out_shape=[jax.ShapeDtypeStruct(x.shape, F32),
                   jax.ShapeDtypeStruct((batch, CONV_HIST, CONV_DIM), F32),
                   jax.ShapeDtypeStruct((batch, hp, D_STATE), F32)],
        grid=(batch, steps),
        in_specs=[pl.BlockSpec((rows, d), row), _const_spec((1, d)), _const_spec(w_proj.shape),
                  pl.BlockSpec((1, CONV_HIST, CONV_DIM), per_b), pl.BlockSpec((1, hp, D_STATE), per_b)]
        + [_const_spec(c.shape) for c in consts],
        out_specs=[pl.BlockSpec((rows, d), row), pl.BlockSpec((1, CONV_HIST, CONV_DIM), per_b),
                   pl.BlockSpec((1, hp, D_STATE), per_b)],
        scratch_shapes=[pltpu.VMEM((CONV_SLABS, PAD_ROWS + rows, LANES), F32), pltpu.VMEM((rows, D_INNER), BF16)],
        compiler_params=_params("parallel", "arbitrary"),
        name="ssd_prompt",
    )(x, g.reshape(1, d), w_proj, conv_state, ssm_state, *consts)


SEQS_PER_STEP = 2


def _ssd_sample_body(z_ref, xbc_ref, dt_ref, cs_ref, h0_ref, cw_ref, cb_ref, dtb_ref, alog_ref, drep_ref, ng_ref,
                     gz_ref, nc_ref, hout_ref, xpad_ref, *, q):
    span = PAD_ROWS + q
    for e in range(SEQS_PER_STEP):
        row0 = e * span + PAD_ROWS
        sl = slice(e * q, (e + 1) * q)
        for j in range(CONV_SLABS):
            lanes = slice(j * LANES, (j + 1) * LANES)
            xpad_ref[j, row0 - CONV_HIST:row0, :] = cs_ref[e, :, lanes]
            xpad_ref[j, row0:row0 + q, :] = xbc_ref[sl, lanes]
        _ssd_chunk(q, xpad_ref, row0, z_ref[sl, :], dt_ref[sl, :], cw_ref, cb_ref, dtb_ref, alog_ref, drep_ref,
                   ng_ref, h0_ref.at[e], hout_ref.at[e], gz_ref.at[pl.ds(e * q, q)])
        for j in range(CONV_SLABS):
            nc_ref[e, :, j * LANES:(j + 1) * LANES] = xpad_ref[j, row0 + q - CONV_HIST:row0 + q, :]


def _ssd_sample(z, xbc, dt_raw, conv_state, ssm_state, prm, batch, seq):
    n = SEQS_PER_STEP
    hp = SSD_HEADS * SSD_HEADDIM
    row = lambda b: (b, 0)
    per_b = lambda b: (b, 0, 0)
    return pl.pallas_call(
        functools.partial(_ssd_sample_body, q=seq),
        out_shape=[jax.ShapeDtypeStruct((batch * seq, D_INNER), F32),
                   jax.ShapeDtypeStruct((batch, CONV_HIST, CONV_DIM), F32),
                   jax.ShapeDtypeStruct((batch, hp, D_STATE), F32)],
        grid=(batch // n,),
        in_specs=[pl.BlockSpec((n * seq, D_INNER), row), pl.BlockSpec((n * seq, CONV_DIM), row),
                  pl.BlockSpec((n * seq, LANES), row),
                  pl.BlockSpec((n, CONV_HIST, CONV_DIM), per_b), pl.BlockSpec((n, hp, D_STATE), per_b)]
        + [_const_spec(c.shape) for c in prm],
        out_specs=[pl.BlockSpec((n * seq, D_INNER), row), pl.BlockSpec((n, CONV_HIST, CONV_DIM), per_b),
                   pl.BlockSpec((n, hp, D_STATE), per_b)],
        scratch_shapes=[pltpu.VMEM((CONV_SLABS, n * (PAD_ROWS + seq), LANES), F32)],
        compiler_params=_params("parallel"),
        name="ssd_sample",
    )(z, xbc, dt_raw, conv_state, ssm_state, *prm)


ATTN_SLAB = ATTN_QBLOCK * 16
HEAD_PAIRS = ATTN_HPG // 2


def _attn_prompt_body(slope_ref, *refs):
    qkv_refs = refs[:9]
    o_ref = refs[9]
    m_sc, l_sc, acc_sc = refs[10:13]
    kv_ext = refs[13:]
    pair = pl.program_id(1)
    s_idx = pl.program_id(2)
    qb = ATTN_QBLOCK

    for g, (_, dil) in enumerate(ATTN_PATTERNS):
        sub = ATTN_SLAB // dil
        for ext, src in zip(kv_ext[2 * g:2 * g + 2], qkv_refs[3 * g + 1:3 * g + 3]):
            @pl.when(s_idx == 0)
            def _(ext=ext, dil=dil):
                ext[:, 0:qb, :] = jnp.zeros((dil, qb, LANES), BF16)

            for r in range(dil):
                picked = src[...] if dil == 1 else src[pl.ds(r, sub, stride=dil), :]
                ext[r, qb:qb + sub, :] = picked.astype(BF16)

    row = lax.broadcasted_iota(jnp.int32, (qb, 2 * qb), 0)
    col = lax.broadcasted_iota(jnp.int32, (qb, 2 * qb), 1)
    dist = qb + row - col
    in_window = (dist >= 0) & (dist <= ATTN_NKEYS)
    valid_first = in_window & (col >= jnp.where(s_idx > 0, 0, qb))
    dist_f = dist.astype(F32)
    lane = lax.broadcasted_iota(jnp.int32, (qb, LANES), 1)
    low_half = lane < ATTN_HEAD_DIM

    for g, (_, dil) in enumerate(ATTN_PATTERNS):
        q_ref = qkv_refs[3 * g]
        k_ext, v_ext = kv_ext[2 * g:2 * g + 2]
        n_blk = ATTN_SLAB // (qb * dil)
        slopes = [slope_ref[g * ATTN_HPG + 2 * pair + e] * float(dil) for e in range(2)]
        bias_in = [jnp.where(in_window, -slopes[e] * dist_f, NEG_INF) for e in range(2)]
        bias_first = [jnp.where(valid_first, -slopes[e] * dist_f, NEG_INF) for e in range(2)]

        for r, blk in [(r, blk) for r in range(dil) for blk in range(n_blk)]:
            start = r + (dil * qb) * blk
            rows = pl.ds(start, qb) if dil == 1 else pl.ds(start, qb, stride=dil)
            k2 = k_ext[r, blk * qb:(blk + 2) * qb, :]
            v2 = v_ext[r, blk * qb:(blk + 2) * qb, :]
            q2 = q_ref[rows, :] * (ATTN_HEAD_DIM ** -0.5)
            bias = bias_in if blk > 0 else bias_first
            ms, ls, pvs = [], [], []
            for e in range(2):
                qm = jnp.where(low_half if e == 0 else ~low_half, q2, 0.0).astype(BF16)
                s = lax.dot_general(qm, k2, NT_DIMS, preferred_element_type=F32) + bias[e]
                m = jnp.max(s, axis=-1, keepdims=True)
                p = jnp.exp(s - m)
                ms.append(m)
                ls.append(jnp.sum(p, axis=-1, keepdims=True))
                pvs.append(jnp.dot(p.astype(BF16), v2, preferred_element_type=F32))
            m_new = jnp.where(low_half, ms[0], ms[1])
            l_new = jnp.where(low_half, ls[0], ls[1])
            pv_new = jnp.where(low_half, pvs[0], pvs[1])
            if g > 0:
                m_old = m_sc[rows, :]
                m_tot = jnp.maximum(m_old, m_new)
                a_old = jnp.exp(m_old - m_tot)
                a_new = jnp.exp(m_new - m_tot)
                l_new = a_old * l_sc[rows, :] + a_new * l_new
                pv_new = a_old * acc_sc[rows, :] + a_new * pv_new
                m_new = m_tot
            if g == N_ATTN_GROUPS - 1:
                o_ref[rows, :] = pv_new / l_new
            else:
                m_sc[rows, :] = m_new
                l_sc[rows, :] = l_new
                acc_sc[rows, :] = pv_new

    for g, (_, dil) in enumerate(ATTN_PATTERNS):
        sub = ATTN_SLAB // dil
        for ext in kv_ext[2 * g:2 * g + 2]:
            ext[:, 0:qb, :] = ext[:, sub:sub + qb, :]


def _attn_prompt(qkv, slope_tab, batch, seq):
    n_slab = seq // ATTN_SLAB
    col_blocks = ATTN_WIDTH // LANES

    def spec(g, which):
        base = (3 * g + which) * col_blocks
        return pl.BlockSpec((ATTN_SLAB, LANES), lambda b, p, s: (b * n_slab + s, base + p))

    in_specs = [pl.BlockSpec(memory_space=pltpu.SMEM)]
    in_specs += [spec(g, which) for g in range(N_ATTN_GROUPS) for which in range(3)]
    return pl.pallas_call(
        _attn_prompt_body,
        out_shape=jax.ShapeDtypeStruct((batch * seq, ATTN_WIDTH), F32),
        grid=(batch, HEAD_PAIRS, n_slab),
        in_specs=in_specs,
        out_specs=pl.BlockSpec((ATTN_SLAB, LANES), lambda b, p, s: (b * n_slab + s, p)),
        scratch_shapes=[pltpu.VMEM((ATTN_SLAB, LANES), F32)] * 3 + [
            pltpu.VMEM((dil, ATTN_QBLOCK + ATTN_SLAB // dil, LANES), BF16)
            for _, dil in ATTN_PATTERNS for _ in range(2)],
        compiler_params=_params("parallel", "parallel", "arbitrary"),
        name="attn_prompt",
    )(slope_tab, *([qkv] * 9))


def _kv_tail_body(k_ref, v_ref, o_ref):
    o_ref[0, :ATTN_WIDTH, :] = k_ref[...].T
    o_ref[0, ATTN_WIDTH:, :] = v_ref[...].T


def _kv_tail_t(qkv, g, batch, seq, tm=KV_TAIL_TOKENS):
    win = min(ATTN_PATTERNS[g][0], seq)
    tm = min(tm, win)
    first = (seq - win) // tm
    per_b = seq // tm

    def spec(which):
        return pl.BlockSpec((tm, ATTN_WIDTH), lambda b, j: (b * per_b + first + j, 3 * g + which))

    return pl.pallas_call(
        _kv_tail_body,
        out_shape=jax.ShapeDtypeStruct((batch, 2 * ATTN_WIDTH, win), F32),
        grid=(batch, win // tm),
        in_specs=[spec(1), spec(2)],
        out_specs=pl.BlockSpec((1, 2 * ATTN_WIDTH, tm), lambda b, j: (b, 0, j)),
        compiler_params=_params("parallel", "parallel"),
        name=f"kv_tail_g{g}",
    )(qkv, qkv)


SAMPLE_SEQS_PER_STEP = 2


def _attn_sample_body(slope_ref, qkv_ref, c0_ref, c1_ref, c2_ref, o_ref, *, n_new):
    caches = (c0_ref, c1_ref, c2_ref)
    rows = ATTN_HPG * n_new
    row_head = lax.broadcasted_iota(jnp.int32, (rows, ATTN_WIDTH), 0) // n_new
    lane_head = lax.broadcasted_iota(jnp.int32, (rows, ATTN_WIDTH), 1) // ATTN_HEAD_DIM
    diag = row_head == lane_head
    head_col = lax.broadcasted_iota(jnp.int32, (rows, 1), 0) // n_new
    qi_n = lax.broadcasted_iota(jnp.int32, (rows, n_new), 0) % n_new
    gap_n = qi_n - lax.broadcasted_iota(jnp.int32, (rows, n_new), 1)
    geometry = []
    for g, (c_ref, (win, dil)) in enumerate(zip(caches, ATTN_PATTERNS)):
        past_len = c_ref.shape[2]
        slope = jnp.zeros((rows, 1), F32)
        for h in range(ATTN_HPG):
            slope = jnp.where(head_col == h, slope_ref[g * ATTN_HPG + h], slope)
        gap_p = (past_len + lax.broadcasted_iota(jnp.int32, (rows, past_len), 0) % n_new
                 - lax.broadcasted_iota(jnp.int32, (rows, past_len), 1))
        ok_p = (gap_p <= win) & ((gap_p & (dil - 1)) == 0)
        ok_n = (gap_n >= 0) & ((gap_n & (dil - 1)) == 0)
        geometry.append((ok_p, slope * gap_p.astype(F32), ok_n, slope * gap_n.astype(F32)))

    for e in range(SAMPLE_SEQS_PER_STEP):
        qkv = qkv_ref[e * n_new:(e + 1) * n_new, :]
        outs, lses = [], []
        for g, c_ref in enumerate(caches):
            base = g * 3 * ATTN_WIDTH
            ok_p, bias_p, ok_n, bias_n = geometry[g]
            q = qkv[:, base:base + ATTN_WIDTH] * (ATTN_HEAD_DIM ** -0.5)
            q_bd = jnp.where(diag, jnp.concatenate([q] * ATTN_HPG, axis=0), 0.0).astype(BF16)
            k_new = qkv[:, base + ATTN_WIDTH:base + 2 * ATTN_WIDTH].astype(BF16)
            v_new = qkv[:, base + 2 * ATTN_WIDTH:base + 3 * ATTN_WIDTH].astype(BF16)
            k_t = c_ref[e, :ATTN_WIDTH, :].astype(BF16)
            v_t = c_ref[e, ATTN_WIDTH:, :].astype(BF16)
            s_p = jnp.dot(q_bd, k_t, preferred_element_type=F32) - bias_p
            s_p = jnp.where(ok_p, s_p, NEG_INF)
            s_n = lax.dot_general(q_bd, k_new, NT_DIMS, preferred_element_type=F32) - bias_n
            s_n = jnp.where(ok_n, s_n, NEG_INF)
            mx = jnp.maximum(jnp.max(s_p, axis=-1, keepdims=True), jnp.max(s_n, axis=-1, keepdims=True))
            e_p = jnp.exp(s_p - mx)
            e_n = jnp.exp(s_n - mx)
            l = jnp.sum(e_p, axis=-1, keepdims=True) + jnp.sum(e_n, axis=-1, keepdims=True)
            o = (lax.dot_general(e_p.astype(BF16), v_t, NT_DIMS, preferred_element_type=F32)
                 + jnp.dot(e_n.astype(BF16), v_new, preferred_element_type=F32)) / l
            outs.append(o)
            lses.append(mx + jnp.log(l))
        top = jnp.maximum(jnp.maximum(lses[0], lses[1]), lses[2])
        es = [jnp.exp(l - top) for l in lses]
        merged = (es[0] * outs[0] + es[1] * outs[1] + es[2] * outs[2]) / (es[0] + es[1] + es[2])
        merged = jnp.where(diag, merged, 0.0).reshape(ATTN_HPG, n_new, ATTN_WIDTH)
        o_ref[e * n_new:(e + 1) * n_new, :] = jnp.sum(merged, axis=0)


def _attn_sample(qkv, caches_t, slope_tab, batch, n_new):
    n = SAMPLE_SEQS_PER_STEP
    specs = [pl.BlockSpec((n,) + c.shape[1:], lambda b: (b, 0, 0)) for c in caches_t]
    return pl.pallas_call(
        functools.partial(_attn_sample_body, n_new=n_new),
        out_shape=jax.ShapeDtypeStruct((batch * n_new, ATTN_WIDTH), F32),
        grid=(batch // n,),
        in_specs=[pl.BlockSpec(memory_space=pltpu.SMEM),
                  pl.BlockSpec((n * n_new, QKV_WIDTH), lambda b: (b, 0))] + specs,
        out_specs=pl.BlockSpec((n * n_new, ATTN_WIDTH), lambda b: (b, 0)),
        compiler_params=_params("parallel"),
        name="attn_sample",
    )(slope_tab, qkv, *caches_t)


KV_ROWS_SEQS = 16


def _kv_rows_body(qkv_ref, *o_refs, n_new):
    for g, o_ref in enumerate(o_refs):
        lo = (3 * g + 1) * ATTN_WIDTH
        for b in range(KV_ROWS_SEQS):
            rows = qkv_ref[b * n_new:(b + 1) * n_new, lo:lo + 2 * ATTN_WIDTH]
            o_ref[0, b] = rows.reshape(n_new, 2, ATTN_HPG, ATTN_HEAD_DIM)


def _kv_rows(qkv, batch, n_new):
    shape = (1, batch, n_new, 2, ATTN_HPG, ATTN_HEAD_DIM)
    block = (1, KV_ROWS_SEQS) + shape[2:]
    return pl.pallas_call(
        functools.partial(_kv_rows_body, n_new=n_new),
        out_shape=[jax.ShapeDtypeStruct(shape, F32)] * N_ATTN_GROUPS,
        grid=(batch // KV_ROWS_SEQS,),
        in_specs=[pl.BlockSpec((KV_ROWS_SEQS * n_new, QKV_WIDTH), lambda i: (i, 0))],
        out_specs=[pl.BlockSpec(block, lambda i: (0, i, 0, 0, 0, 0))] * N_ATTN_GROUPS,
        compiler_params=_params("parallel"),
        name="kv_rows",
    )(qkv)


def _alibi_slopes():
    n_heads = N_ATTN_GROUPS * ATTN_HPG
    return [2.0 ** (-8.0 * (h + 1) / n_heads) for h in range(n_heads)]


def kernel(x_prompt, x_sample, state_conv, state_ssm, cache_kv_g0, cache_kv_g1, cache_kv_g2, norm_w, w_ffn_in,
           w_ffn_out, ssm_w_in, ssm_conv_w, ssm_conv_b, ssm_dt_bias, ssm_a_log, ssm_d, ssm_norm_w, ssm_w_out,
           attn_w_qkv, attn_w_o, norm_f):
    bp, lp, d = x_prompt.shape
    bs, ls, _ = x_sample.shape
    xs_all = [x_prompt.reshape(bp * lp, d), x_sample.reshape(bs * ls, d)]
    dims = [(bp, lp), (bs, ls)]
    hist = CONV_W - 1
    hp = SSD_HEADS * SSD_HEADDIM

    w_in = w_ffn_in.astype(BF16)
    w_out = w_ffn_out.astype(BF16)
    pad = LANES - SSD_HEADS
    w_proj = jnp.pad(ssm_w_in[0], ((0, 0), (0, pad))).astype(BF16)
    ssd_prm = (ssm_conv_w[0], ssm_conv_b[0].reshape(1, CONV_DIM),
               jnp.pad(ssm_dt_bias[0], (0, pad)).reshape(1, LANES),
               jnp.pad(ssm_a_log[0], (0, pad)).reshape(1, LANES),
               jnp.repeat(ssm_d[0], SSD_HEADDIM).reshape(1, D_INNER),
               ssm_norm_w[0].reshape(1, D_INNER))
    w_ssm_out = ssm_w_out[0].astype(BF16)
    w_qkv = attn_w_qkv[0].astype(BF16)
    w_o = attn_w_o[0].astype(BF16)
    slope_tab = jnp.asarray(_alibi_slopes(), F32)

    conv_states = [jnp.zeros((bp, hist, CONV_DIM), F32), state_conv[0]]
    ssm_states = [jnp.zeros((bp, hp, D_STATE), F32), state_ssm[0].reshape(bs, hp, D_STATE)]
    caches_t = [jnp.transpose(c[0], (0, 2, 3, 4, 1)).reshape(bs, 2 * ATTN_WIDTH, c.shape[2])
                for c in (cache_kv_g0, cache_kv_g1, cache_kv_g2)]

    conv_out, ssm_out = [], []
    for n, (x, (b, l)) in enumerate(zip(xs_all, dims)):
        x = _ffn(x, norm_w[0, 0], w_in, w_out, (0, 0))
        mixer = None
        if l % (CHUNKS_PER_STEP * SSD_CHUNK) == 0:
            x, new_conv, new_ssm = _ssd_prompt(x, norm_w[0, 1], w_proj, conv_states[n], ssm_states[n], ssd_prm,
                                               w_ssm_out, b, l)
        else:
            z, xbc, dt_raw = _norm_mm(x, norm_w[0, 1], w_proj, (D_INNER, CONV_DIM, LANES))
            gz, new_conv, new_ssm = _ssd_sample(z, xbc, dt_raw, conv_states[n], ssm_states[n], ssd_prm, b, l)
            mixer = (gz, w_ssm_out)
        conv_out.append(new_conv[None])
        ssm_out.append(new_ssm.reshape(1, b, SSD_HEADS, SSD_HEADDIM, D_STATE))
        xs_all[n] = _ffn(x, norm_w[0, 2], w_in, w_out, (0, 1), mixer=mixer)

    kv_out = []
    for n, (x, (b, l)) in enumerate(zip(xs_all, dims)):
        x = _ffn(x, norm_w[1, 0], w_in, w_out, (1, 0))
        (qkv,) = _norm_mm(x, norm_w[1, 1], w_qkv, (QKV_WIDTH,), tm=QKV_ROWS)
        if n == 0:
            o = _attn_prompt(qkv, slope_tab, b, l)
            kv_t = [_kv_tail_t(qkv, g, b, l) for g in range(N_ATTN_GROUPS)]
            kv_out.append([jnp.transpose(t.reshape(b, 2, ATTN_HPG, ATTN_HEAD_DIM, t.shape[2]), (0, 4, 1, 2, 3))[None]
                           for t in kv_t])
        else:
            o = _attn_sample(qkv, caches_t, slope_tab, b, l)
            kv_out.append(_kv_rows(qkv, b, l))
        xs_all[n] = _ffn(x, norm_w[1, 2], w_in, w_out, (1, 1), mixer=(o, w_o), g_final=norm_f)

    return (xs_all[0].reshape(bp, lp, d), xs_all[1].reshape(bs, ls, d),
            conv_out[0], conv_out[1], ssm_out[0], ssm_out[1],
            kv_out[0][0], kv_out[1][0], kv_out[0][1], kv_out[1][1], kv_out[0][2], kv_out[1][2])
```

```python
import functools

import jax
import jax.numpy as jnp
from jax import lax
from jax.experimental import pallas as pl
from jax.experimental.pallas import tpu as pltpu

F32 = jnp.float32
BF16 = jnp.bfloat16

EPS = 1e-6
NEG_INF = -1e30

D_FF = 2816
D_INNER = 2048
SSD_HEADS = 32
SSD_HEADDIM = 64
SSD_GROUPS = 8
SSD_HPG = SSD_HEADS // SSD_GROUPS
GROUP_W = SSD_HPG * SSD_HEADDIM
D_STATE = 128
CONV_W = 4
CONV_DIM = D_INNER + 2 * SSD_GROUPS * D_STATE
SSD_CHUNK = 128
ATTN_PATTERNS = ((128, 1), (512, 4), (2048, 16))
N_ATTN_GROUPS = 3
ATTN_HPG = 8
ATTN_HEAD_DIM = 64
ATTN_WIDTH = ATTN_HPG * ATTN_HEAD_DIM
QKV_WIDTH = N_ATTN_GROUPS * 3 * ATTN_WIDTH
ATTN_QBLOCK = 128
ATTN_NKEYS = 128

LANES = 128
SUBLANES = 8
VMEM_LIMIT = 56 * 1024 * 1024

FFN_ROWS = 512
PROJ_ROWS = 256
QKV_ROWS = 512
KV_TAIL_TOKENS = 512

NT_DIMS = (((1,), (1,)), ((), ()))
TN_DIMS = (((0,), (0,)), ((), ()))


def _const_spec(shape):
    zeros = (0,) * len(shape)
    return pl.BlockSpec(shape, lambda *_: zeros, pipeline_mode=pl.Buffered(1))


def _params(*semantics):
    return pltpu.CompilerParams(dimension_semantics=semantics, vmem_limit_bytes=VMEM_LIMIT)


def _rms(x, g):
    return x * lax.rsqrt(jnp.mean(x * x, axis=-1, keepdims=True) + EPS) * g


def _silu(x):
    return x * jax.nn.sigmoid(x)


def _softplus(x):
    return jnp.maximum(x, 0.0) + jnp.log1p(jnp.exp(-jnp.abs(x)))


def _ffn_body(x_ref, g_ref, win_ref, wout_ref, *rest, mixer_proj, final_norm):
    rest = list(rest)
    o_ref = rest.pop()
    x = x_ref[...]
    if mixer_proj:
        y_ref, wy_ref = rest[:2]
        x = x + jnp.dot(y_ref[...].astype(BF16), wy_ref[...], preferred_element_type=F32)
    if final_norm:
        gf_ref = rest[-1]
    h = _rms(x, g_ref[...]).astype(BF16)
    a = jnp.dot(h, win_ref[:, :D_FF], preferred_element_type=F32)
    b = jnp.dot(h, win_ref[:, D_FF:], preferred_element_type=F32)
    t = (_silu(a) * b).astype(BF16)
    y = x + 0.5 * jnp.dot(t, wout_ref[...], preferred_element_type=F32)
    if final_norm:
        y = _rms(y, gf_ref[...])
    o_ref[...] = y


def _ffn(x, g, w_in, w_out, which, mixer=None, g_final=None, tm=FFN_ROWS):
    m, d = x.shape
    tm = min(tm, m)
    row = pl.BlockSpec((tm, d), lambda i: (i, 0))

    def picked(w):
        return pl.BlockSpec((None, None) + w.shape[2:], lambda i: which + (0, 0), pipeline_mode=pl.Buffered(1))

    in_specs = [row, _const_spec((1, d)), picked(w_in), picked(w_out)]
    args = [x, g.reshape(1, d), w_in, w_out]
    if mixer is not None:
        y, w_y = mixer
        in_specs += [pl.BlockSpec((tm, y.shape[1]), lambda i: (i, 0)), _const_spec(w_y.shape)]
        args += [y, w_y]
    if g_final is not None:
        in_specs.append(_const_spec((1, d)))
        args.append(g_final.reshape(1, d))
    return pl.pallas_call(
        functools.partial(_ffn_body, mixer_proj=mixer is not None, final_norm=g_final is not None),
        out_shape=jax.ShapeDtypeStruct((m, d), F32),
        grid=(m // tm,),
        in_specs=in_specs,
        out_specs=row,
        compiler_params=_params("parallel"),
        name="ffn",
    )(*args)


def _norm_mm_body(x_ref, g_ref, w_ref, *o_refs, splits):
    h = _rms(x_ref[...], g_ref[...]).astype(BF16)
    off = 0
    for o_ref, n in zip(o_refs, splits):
        o_ref[...] = jnp.dot(h, w_ref[:, off:off + n], preferred_element_type=F32)
        off += n


def _norm_mm(x, g, w, splits, tm=PROJ_ROWS):
    m, d = x.shape
    tm = min(tm, m)
    return pl.pallas_call(
        functools.partial(_norm_mm_body, splits=splits),
        out_shape=[jax.ShapeDtypeStruct((m, n), F32) for n in splits],
        grid=(m // tm,),
        in_specs=[pl.BlockSpec((tm, d), lambda i: (i, 0)), _const_spec((1, d)), _const_spec(w.shape)],
        out_specs=[pl.BlockSpec((tm, n), lambda i: (i, 0)) for n in splits],
        compiler_params=_params("parallel"),
        name="norm_mm",
    )(x, g.reshape(1, d), w)


def _expand_heads(v, first, width):
    rows = v.shape[0]
    return jnp.concatenate(
        [jnp.broadcast_to(v[:, first + j:first + j + 1], (rows, width)) for j in range(SSD_HPG)], axis=1)


def _split_bf16(v):
    parts, rest = [], v
    for _ in range(3):
        parts.append(rest.astype(BF16))
        rest = rest - parts[-1].astype(F32)
    return jnp.concatenate(parts, axis=1)


def _ssd_chunk(q, xpad_ref, row0, z, dt_raw, cw_ref, cb_ref, dtb_ref, alog_ref, drep_ref, ng_ref,
               h_in_ref, h_out_ref, gz_ref, spread_ref=None):
    def conv(slab):
        sl = slice(slab * LANES, (slab + 1) * LANES)
        acc = cb_ref[:, sl]
        for k in range(CONV_W):
            lo = row0 - (CONV_W - 1) + k
            acc = acc + cw_ref[k:k + 1, sl] * xpad_ref[slab, lo:lo + q, :]
        return _silu(acc)

    x_slabs = D_INNER // LANES
    bc_slabs = SSD_GROUPS * D_STATE // LANES

    dt = _softplus(dt_raw + dtb_ref[...])
    a = -jnp.exp(alog_ref[...])
    row = lax.broadcasted_iota(jnp.int32, (q, q), 0)
    col = lax.broadcasted_iota(jnp.int32, (q, q), 1)
    causal = row >= col
    if q == SUBLANES:
        a_cum = dt * a
        row_q = lax.broadcasted_iota(jnp.int32, (q, LANES), 0)
        for shift in (1, 2, 4):
            a_cum = a_cum + jnp.where(row_q >= shift, pltpu.roll(a_cum, shift, 0), 0.0)
    else:
        pieces = _split_bf16(dt * a)
        tri = causal.astype(BF16)
        a_cum = jnp.dot(jnp.concatenate([tri] * 3, axis=1),
                        jnp.concatenate([pieces[:, k * LANES:(k + 1) * LANES] for k in range(3)], axis=0),
                        preferred_element_type=F32)
    a_last = a_cum[q - 1:q, :]
    w_state = jnp.exp(a_last - a_cum) * dt
    e_cum = jnp.exp(a_cum)
    e_last = jnp.exp(a_last)
    a_cum_t = a_cum.T
    dt_t = dt.T
    if spread_ref is not None:
        lane_head = lax.broadcasted_iota(jnp.int32, (q, GROUP_W), 1) // SSD_HEADDIM
        e_cum_x = jnp.dot(_split_bf16(e_cum), spread_ref[...], preferred_element_type=F32)
        w_state_x = jnp.dot(_split_bf16(w_state), spread_ref[...], preferred_element_type=F32)

    for g in range(SSD_GROUPS):
        h0 = g * SSD_HPG
        c0 = g * GROUP_W
        bmb = conv(x_slabs + g).astype(BF16)
        cmb = conv(x_slabs + bc_slabs + g).astype(BF16)
        cb = lax.dot_general(cmb, bmb, NT_DIMS, preferred_element_type=F32)
        x_g = jnp.concatenate([conv(c0 // LANES + i) for i in range(GROUP_W // LANES)], axis=1)
        wts = []
        for j in range(SSD_HPG):
            h = h0 + j
            seg = a_cum[:, h:h + 1] - a_cum_t[h:h + 1, :]
            decay = jnp.exp(jnp.where(causal, seg, -jnp.inf))
            wts.append((cb * decay * dt_t[h:h + 1, :]).astype(BF16))
        if spread_ref is None:
            y_diag = jnp.concatenate(
                [jnp.dot(w, x_g[:, j * SSD_HEADDIM:(j + 1) * SSD_HEADDIM].astype(BF16), preferred_element_type=F32)
                 for j, w in enumerate(wts)], axis=1)
        else:
            x_gb = x_g.astype(BF16)
            x_heads = jnp.concatenate(
                [jnp.where(lane_head == j, x_gb, jnp.zeros_like(x_gb)) for j in range(SSD_HPG)], axis=0)
            y_diag = jnp.dot(jnp.concatenate(wts, axis=1), x_heads, preferred_element_type=F32)

        h_prev = h_in_ref[c0:c0 + GROUP_W, :]
        y_off = lax.dot_general(cmb, h_prev.astype(BF16), NT_DIMS, preferred_element_type=F32)
        if spread_ref is None:
            y_off = y_off * _expand_heads(e_cum, h0, SSD_HEADDIM)
            xw = (x_g * _expand_heads(w_state, h0, SSD_HEADDIM)).astype(BF16)
        else:
            y_off = y_off * e_cum_x[:, c0:c0 + GROUP_W]
            xw = (x_g * w_state_x[:, c0:c0 + GROUP_W]).astype(BF16)
        states = lax.dot_general(xw, bmb, TN_DIMS, preferred_element_type=F32)
        carry = jnp.concatenate(
            [jnp.broadcast_to(e_last[:, h0 + j:h0 + j + 1], (SSD_HEADDIM, D_STATE)) for j in range(SSD_HPG)], axis=0)
        h_out_ref[c0:c0 + GROUP_W, :] = carry * h_prev + states

        y = y_diag + y_off + drep_ref[:, c0:c0 + GROUP_W] * x_g
        gz = y * _silu(z[:, c0:c0 + GROUP_W])
        gz = gz * lax.rsqrt(jnp.mean(gz * gz, axis=-1, keepdims=True) + EPS) * ng_ref[:, c0:c0 + GROUP_W]
        gz_ref[:, c0:c0 + GROUP_W] = gz.astype(gz_ref.dtype)


PAD_ROWS = 8


CONV_SLABS = CONV_DIM // LANES
CONV_HIST = CONV_W - 1
CHUNKS_PER_STEP = 2


def _ssd_prompt_body(x_ref, g_ref, wp_ref, cs_ref, h0_ref, cw_ref, cb_ref, dtb_ref, alog_ref, drep_ref, ng_ref,
                     wo_ref, spread_ref, o_ref, nc_ref, hout_ref, xpad_ref, gz_ref):
    q = SSD_CHUNK
    rows = CHUNKS_PER_STEP * q

    @pl.when(pl.program_id(1) == 0)
    def _():
        for j in range(CONV_SLABS):
            xpad_ref[j, PAD_ROWS - CONV_HIST:PAD_ROWS, :] = cs_ref[0, :, j * LANES:(j + 1) * LANES]
        hout_ref[0] = h0_ref[0]

    x = x_ref[...]
    h = _rms(x, g_ref[...]).astype(BF16)
    z = jnp.dot(h, wp_ref[:, :D_INNER], preferred_element_type=F32)
    dt_raw = jnp.dot(h, wp_ref[:, D_INNER + CONV_DIM:], preferred_element_type=F32)
    for j in range(0, CONV_SLABS, 2):
        lo = D_INNER + j * LANES
        pair = jnp.dot(h, wp_ref[:, lo:lo + 2 * LANES], preferred_element_type=F32)
        xpad_ref[j, PAD_ROWS:PAD_ROWS + rows, :] = pair[:, :LANES]
        xpad_ref[j + 1, PAD_ROWS:PAD_ROWS + rows, :] = pair[:, LANES:]
    state = hout_ref.at[0]
    for c in range(CHUNKS_PER_STEP):
        sl = slice(c * q, (c + 1) * q)
        _ssd_chunk(q, xpad_ref, PAD_ROWS + c * q, z[sl], dt_raw[sl], cw_ref, cb_ref, dtb_ref, alog_ref, drep_ref,
                   ng_ref, state, state, gz_ref.at[pl.ds(c * q, q)], spread_ref)
    o_ref[...] = x + jnp.dot(gz_ref[...], wo_ref[...], preferred_element_type=F32)
    for j in range(CONV_SLABS):
        tail = xpad_ref[j, PAD_ROWS + rows - CONV_HIST:PAD_ROWS + rows, :]
        nc_ref[0, :, j * LANES:(j + 1) * LANES] = tail
        xpad_ref[j, PAD_ROWS - CONV_HIST:PAD_ROWS, :] = tail


def _ssd_prompt(x, g, w_proj, conv_state, ssm_state, prm, w_out, batch, seq):
    rows = CHUNKS_PER_STEP * SSD_CHUNK
    steps = seq // rows
    d = x.shape[1]
    hp = SSD_HEADS * SSD_HEADDIM
    row = lambda b, c: (b * steps + c, 0)
    per_b = lambda b, c: (b, 0, 0)
    head_of_row = jnp.arange(3 * LANES)[:, None] % LANES
    spread = (head_of_row == jnp.arange(SSD_HEADS * SSD_HEADDIM)[None, :] // SSD_HEADDIM).astype(BF16)
    consts = list(prm) + [w_out, spread]
    return pl.pallas_call(
        _ssd_prompt_body,
        out_shape=[jax.ShapeDtypeStruct(x.shape, F32),
                   jax.ShapeDtypeStruct((batch, CONV_HIST, CONV_DIM), F32),
                   jax.ShapeDtypeStruct((batch, hp, D_STATE), F32)],
        grid=(batch, steps),
        in_specs=[pl.BlockSpec((rows, d), row), _const_spec((1, d)), _const_spec(w_proj.shape),
                  pl.BlockSpec((1, CONV_HIST, CONV_DIM), per_b), pl.BlockSpec((1, hp, D_STATE), per_b)]
        + [_const_spec(c.shape) for c in consts],
        out_specs=[pl.BlockSpec((rows, d), row), pl.BlockSpec((1, CONV_HIST, CONV_DIM), per_b),
                   pl.BlockSpec((1, hp, D_STATE), per_b)],
        scratch_shapes=[pltpu.VMEM((CONV_SLABS, PAD_ROWS + rows, LANES), F32), pltpu.VMEM((rows, D_INNER), BF16)],
        compiler_params=_params("parallel", "arbitrary"),
        name="ssd_prompt",
    )(x, g.reshape(1, d), w_proj, conv_state, ssm_state, *consts)


SEQS_PER_STEP = 2


def _ssd_sample_body(z_ref, xbc_ref, dt_ref, cs_ref, h0_ref, cw_ref, cb_ref, dtb_ref, alog_ref, drep_ref, ng_ref,
                     gz_ref, nc_ref, hout_ref, xpad_ref, *, q):
    span = PAD_ROWS + q
    for e in range(SEQS_PER_STEP):
        row0 = e * span + PAD_ROWS
        sl = slice(e * q, (e + 1) * q)
        for j in range(CONV_SLABS):
            lanes = slice(j * LANES, (j + 1) * LANES)
            xpad_ref[j, row0 - CONV_HIST:row0, :] = cs_ref[e, :, lanes]
            xpad_ref[j, row0:row0 + q, :] = xbc_ref[sl, lanes]
        _ssd_chunk(q, xpad_ref, row0, z_ref[sl, :], dt_ref[sl, :], cw_ref, cb_ref, dtb_ref, alog_ref, drep_ref,
                   ng_ref, h0_ref.at[e], hout_ref.at[e], gz_ref.at[pl.ds(e * q, q)])
        for j in range(CONV_SLABS):
            nc_ref[e, :, j * LANES:(j + 1) * LANES] = xpad_ref[j, row0 + q - CONV_HIST:row0 + q, :]


def _ssd_sample(z, xbc, dt_raw, conv_state, ssm_state, prm, batch, seq):
    n = SEQS_PER_STEP
    hp = SSD_HEADS * SSD_HEADDIM
    row = lambda b: (b, 0)
    per_b = lambda b: (b, 0, 0)
    return pl.pallas_call(
        functools.partial(_ssd_sample_body, q=seq),
        out_shape=[jax.ShapeDtypeStruct((batch * seq, D_INNER), F32),
                   jax.ShapeDtypeStruct((batch, CONV_HIST, CONV_DIM), F32),
                   jax.ShapeDtypeStruct((batch, hp, D_STATE), F32)],
        grid=(batch // n,),
        in_specs=[pl.BlockSpec((n * seq, D_INNER), row), pl.BlockSpec((n * seq, CONV_DIM), row),
                  pl.BlockSpec((n * seq, LANES), row),
                  pl.BlockSpec((n, CONV_HIST, CONV_DIM), per_b), pl.BlockSpec((n, hp, D_STATE), per_b)]
        + [_const_spec(c.shape) for c in prm],
        out_specs=[pl.BlockSpec((n * seq, D_INNER), row), pl.BlockSpec((n, CONV_HIST, CONV_DIM), per_b),
                   pl.BlockSpec((n, hp, D_STATE), per_b)],
        scratch_shapes=[pltpu.VMEM((CONV_SLABS, n * (PAD_ROWS + seq), LANES), F32)],
        compiler_params=_params("parallel"),
        name="ssd_sample",
    )(z, xbc, dt_raw, conv_state, ssm_state, *prm)


ATTN_SLAB = ATTN_QBLOCK * 16
HEAD_PAIRS = ATTN_HPG // 2


def _attn_prompt_body(slope_ref, *refs):
    qkv_refs = refs[:9]
    o_ref = refs[9]
    m_sc, l_sc, acc_sc = refs[10:13]
    kv_ext = refs[13:]
    pair = pl.program_id(1)
    s_idx = pl.program_id(2)
    qb = ATTN_QBLOCK

    for g, (_, dil) in enumerate(ATTN_PATTERNS):
        sub = ATTN_SLAB // dil
        for ext, src in zip(kv_ext[2 * g:2 * g + 2], qkv_refs[3 * g + 1:3 * g + 3]):
            @pl.when(s_idx == 0)
            def _(ext=ext, dil=dil):
                ext[:, 0:qb, :] = jnp.zeros((dil, qb, LANES), BF16)

            for r in range(dil):
                picked = src[...] if dil == 1 else src[pl.ds(r, sub, stride=dil), :]
                ext[r, qb:qb + sub, :] = picked.astype(BF16)

    row = lax.broadcasted_iota(jnp.int32, (qb, 2 * qb), 0)
    col = lax.broadcasted_iota(jnp.int32, (qb, 2 * qb), 1)
    dist = qb + row - col
    in_window = (dist >= 0) & (dist <= ATTN_NKEYS)
    valid_first = in_window & (col >= jnp.where(s_idx > 0, 0, qb))
    dist_f = dist.astype(F32)
    lane = lax.broadcasted_iota(jnp.int32, (qb, LANES), 1)
    low_half = lane < ATTN_HEAD_DIM

    for g, (_, dil) in enumerate(ATTN_PATTERNS):
        q_ref = qkv_refs[3 * g]
        k_ext, v_ext = kv_ext[2 * g:2 * g + 2]
        n_blk = ATTN_SLAB // (qb * dil)
        slopes = [slope_ref[g * ATTN_HPG + 2 * pair + e] * float(dil) for e in range(2)]
        bias_in = [jnp.where(in_window, -slopes[e] * dist_f, NEG_INF) for e in range(2)]
        bias_first = [jnp.where(valid_first, -slopes[e] * dist_f, NEG_INF) for e in range(2)]

        for r, blk in [(r, blk) for r in range(dil) for blk in range(n_blk)]:
            start = r + (dil * qb) * blk
            rows = pl.ds(start, qb) if dil == 1 else pl.ds(start, qb, stride=dil)
            k2 = k_ext[r, blk * qb:(blk + 2) * qb, :]
            v2 = v_ext[r, blk * qb:(blk + 2) * qb, :]
            q2 = q_ref[rows, :] * (ATTN_HEAD_DIM ** -0.5)
            bias = bias_in if blk > 0 else bias_first
            ms, ls, pvs = [], [], []
            for e in range(2):
                qm = jnp.where(low_half if e == 0 else ~low_half, q2, 0.0).astype(BF16)
                s = lax.dot_general(qm, k2, NT_DIMS, preferred_element_type=F32) + bias[e]
                m = jnp.max(s, axis=-1, keepdims=True)
                p = jnp.exp(s - m)
                ms.append(m)
                ls.append(jnp.sum(p, axis=-1, keepdims=True))
                pvs.append(jnp.dot(p.astype(BF16), v2, preferred_element_type=F32))
            m_new = jnp.where(low_half, ms[0], ms[1])
            l_new = jnp.where(low_half, ls[0], ls[1])
            pv_new = jnp.where(low_half, pvs[0], pvs[1])
            if g > 0:
                m_old = m_sc[rows, :]
                m_tot = jnp.maximum(m_old, m_new)
                a_old = jnp.exp(m_old - m_tot)
                a_new = jnp.exp(m_new - m_tot)
                l_new = a_old * l_sc[rows, :] + a_new * l_new
                pv_new = a_old * acc_sc[rows, :] + a_new * pv_new
                m_new = m_tot
            if g == N_ATTN_GROUPS - 1:
                o_ref[rows, :] = pv_new / l_new
            else:
                m_sc[rows, :] = m_new
                l_sc[rows, :] = l_new
                acc_sc[rows, :] = pv_new

    for g, (_, dil) in enumerate(ATTN_PATTERNS):
        sub = ATTN_SLAB // dil
        for ext in kv_ext[2 * g:2 * g + 2]:
            ext[:, 0:qb, :] = ext[:, sub:sub + qb, :]


def _attn_prompt(qkv, slope_tab, batch, seq):
    n_slab = seq // ATTN_SLAB
    col_blocks = ATTN_WIDTH // LANES

    def spec(g, which):
        base = (3 * g + which) * col_blocks
        return pl.BlockSpec((ATTN_SLAB, LANES), lambda b, p, s: (b * n_slab + s, base + p))

    in_specs = [pl.BlockSpec(memory_space=pltpu.SMEM)]
    in_specs += [spec(g, which) for g in range(N_ATTN_GROUPS) for which in range(3)]
    return pl.pallas_call(
        _attn_prompt_body,
        out_shape=jax.ShapeDtypeStruct((batch * seq, ATTN_WIDTH), F32),
        grid=(batch, HEAD_PAIRS, n_slab),
        in_specs=in_specs,
        out_specs=pl.BlockSpec((ATTN_SLAB, LANES), lambda b, p, s: (b * n_slab + s, p)),
        scratch_shapes=[pltpu.VMEM((ATTN_SLAB, LANES), F32)] * 3 + [
            pltpu.VMEM((dil, ATTN_QBLOCK + ATTN_SLAB // dil, LANES), BF16)
            for _, dil in ATTN_PATTERNS for _ in range(2)],
        compiler_params=_params("parallel", "parallel", "arbitrary"),
        name="attn_prompt",
    )(slope_tab, *([qkv] * 9))


def _kv_tail_body(k_ref, v_ref, o_ref):
    o_ref[0, :ATTN_WIDTH, :] = k_ref[...].T
    o_ref[0, ATTN_WIDTH:, :] = v_ref[...].T


def _kv_tail_t(qkv, g, batch, seq, tm=KV_TAIL_TOKENS):
    win = min(ATTN_PATTERNS[g][0], seq)
    tm = min(tm, win)
    first = (seq - win) // tm
    per_b = seq // tm

    def spec(which):
        return pl.BlockSpec((tm, ATTN_WIDTH), lambda b, j: (b * per_b + first + j, 3 * g + which))

    return pl.pallas_call(
        _kv_tail_body,
        out_shape=jax.ShapeDtypeStruct((batch, 2 * ATTN_WIDTH, win), F32),
        grid=(batch, win // tm),
        in_specs=[spec(1), spec(2)],
        out_specs=pl.BlockSpec((1, 2 * ATTN_WIDTH, tm), lambda b, j: (b, 0, j)),
        compiler_params=_params("parallel", "parallel"),
        name=f"kv_tail_g{g}",
    )(qkv, qkv)


SAMPLE_SEQS_PER_STEP = 2


def _attn_sample_body(slope_ref, qkv_ref, c0_ref, c1_ref, c2_ref, o_ref, *, n_new):
    caches = (c0_ref, c1_ref, c2_ref)
    rows = ATTN_HPG * n_new
    row_head = lax.broadcasted_iota(jnp.int32, (rows, ATTN_WIDTH), 0) // n_new
    lane_head = lax.broadcasted_iota(jnp.int32, (rows, ATTN_WIDTH), 1) // ATTN_HEAD_DIM
    diag = row_head == lane_head
    head_col = lax.broadcasted_iota(jnp.int32, (rows, 1), 0) // n_new
    qi_n = lax.broadcasted_iota(jnp.int32, (rows, n_new), 0) % n_new
    gap_n = qi_n - lax.broadcasted_iota(jnp.int32, (rows, n_new), 1)
    geometry = []
    for g, (c_ref, (win, dil)) in enumerate(zip(caches, ATTN_PATTERNS)):
        past_len = c_ref.shape[2]
        slope = jnp.zeros((rows, 1), F32)
        for h in range(ATTN_HPG):
            slope = jnp.where(head_col == h, slope_ref[g * ATTN_HPG + h], slope)
        gap_p = (past_len + lax.broadcasted_iota(jnp.int32, (rows, past_len), 0) % n_new
                 - lax.broadcasted_iota(jnp.int32, (rows, past_len), 1))
        ok_p = (gap_p <= win) & ((gap_p & (dil - 1)) == 0)
        ok_n = (gap_n >= 0) & ((gap_n & (dil - 1)) == 0)
        geometry.append((ok_p, slope * gap_p.astype(F32), ok_n, slope * gap_n.astype(F32)))

    for e in range(SAMPLE_SEQS_PER_STEP):
        qkv = qkv_ref[e * n_new:(e + 1) * n_new, :]
        outs, lses = [], []
        for g, c_ref in enumerate(caches):
            base = g * 3 * ATTN_WIDTH
            ok_p, bias_p, ok_n, bias_n = geometry[g]
            q = qkv[:, base:base + ATTN_WIDTH] * (ATTN_HEAD_DIM ** -0.5)
            q_bd = jnp.where(diag, jnp.concatenate([q] * ATTN_HPG, axis=0), 0.0).astype(BF16)
            k_new = qkv[:, base + ATTN_WIDTH:base + 2 * ATTN_WIDTH].astype(BF16)
            v_new = qkv[:, base + 2 * ATTN_WIDTH:base + 3 * ATTN_WIDTH].astype(BF16)
            k_t = c_ref[e, :ATTN_WIDTH, :].astype(BF16)
            v_t = c_ref[e, ATTN_WIDTH:, :].astype(BF16)
            s_p = jnp.dot(q_bd, k_t, preferred_element_type=F32) - bias_p
            s_p = jnp.where(ok_p, s_p, NEG_INF)
            s_n = lax.dot_general(q_bd, k_new, NT_DIMS, preferred_element_type=F32) - bias_n
            s_n = jnp.where(ok_n, s_n, NEG_INF)
            mx = jnp.maximum(jnp.max(s_p, axis=-1, keepdims=True), jnp.max(s_n, axis=-1, keepdims=True))
            e_p = jnp.exp(s_p - mx)
            e_n = jnp.exp(s_n - mx)
            l = jnp.sum(e_p, axis=-1, keepdims=True) + jnp.sum(e_n, axis=-1, keepdims=True)
            o = (lax.dot_general(e_p.astype(BF16), v_t, NT_DIMS, preferred_element_type=F32)
                 + jnp.dot(e_n.astype(BF16), v_new, preferred_element_type=F32)) / l
            outs.append(o)
            lses.append(mx + jnp.log(l))
        top = jnp.maximum(jnp.maximum(lses[0], lses[1]), lses[2])
        es = [jnp.exp(l - top) for l in lses]
        merged = (es[0] * outs[0] + es[1] * outs[1] + es[2] * outs[2]) / (es[0] + es[1] + es[2])
        merged = jnp.where(diag, merged, 0.0).reshape(ATTN_HPG, n_new, ATTN_WIDTH)
        o_ref[e * n_new:(e + 1) * n_new, :] = jnp.sum(merged, axis=0)


def _attn_sample(qkv, caches_t, slope_tab, batch, n_new):
    n = SAMPLE_SEQS_PER_STEP
    specs = [pl.BlockSpec((n,) + c.shape[1:], lambda b: (b, 0, 0)) for c in caches_t]
    return pl.pallas_call(
        functools.partial(_attn_sample_body, n_new=n_new),
        out_shape=jax.ShapeDtypeStruct((batch * n_new, ATTN_WIDTH), F32),
        grid=(batch // n,),
        in_specs=[pl.BlockSpec(memory_space=pltpu.SMEM),
                  pl.BlockSpec((n * n_new, QKV_WIDTH), lambda b: (b, 0))] + specs,
        out_specs=pl.BlockSpec((n * n_new, ATTN_WIDTH), lambda b: (b, 0)),
        compiler_params=_params("parallel"),
        name="attn_sample",
    )(slope_tab, qkv, *caches_t)


KV_ROWS_SEQS = 16


def _kv_rows_body(qkv_ref, *o_refs, n_new):
    for g, o_ref in enumerate(o_refs):
        lo = (3 * g + 1) * ATTN_WIDTH
        for b in range(KV_ROWS_SEQS):
            rows = qkv_ref[b * n_new:(b + 1) * n_new, lo:lo + 2 * ATTN_WIDTH]
            o_ref[0, b] = rows.reshape(n_new, 2, ATTN_HPG, ATTN_HEAD_DIM)


def _kv_rows(qkv, batch, n_new):
    shape = (1, batch, n_new, 2, ATTN_HPG, ATTN_HEAD_DIM)
    block = (1, KV_ROWS_SEQS) + shape[2:]
    return pl.pallas_call(
        functools.partial(_kv_rows_body, n_new=n_new),
        out_shape=[jax.ShapeDtypeStruct(shape, F32)] * N_ATTN_GROUPS,
        grid=(batch // KV_ROWS_SEQS,),
        in_specs=[pl.BlockSpec((KV_ROWS_SEQS * n_new, QKV_WIDTH), lambda i: (i, 0))],
        out_specs=[pl.BlockSpec(block, lambda i: (0, i, 0, 0, 0, 0))] * N_ATTN_GROUPS,
        compiler_params=_params("parallel"),
        name="kv_rows",
    )(qkv)


def _alibi_slopes():
    n_heads = N_ATTN_GROUPS * ATTN_HPG
    return [2.0 ** (-8.0 * (h + 1) / n_heads) for h in range(n_heads)]


def kernel(x_prompt, x_sample, state_conv, state_ssm, cache_kv_g0, cache_kv_g1, cache_kv_g2, norm_w, w_ffn_in,
           w_ffn_out, ssm_w_in, ssm_conv_w, ssm_conv_b, ssm_dt_bias, ssm_a_log, ssm_d, ssm_norm_w, ssm_w_out,
           attn_w_qkv, attn_w_o, norm_f):
    bp, lp, d = x_prompt.shape
    bs, ls, _ = x_sample.shape
    xs_all = [x_prompt.reshape(bp * lp, d), x_sample.reshape(bs * ls, d)]
    dims = [(bp, lp), (bs, ls)]
    hist = CONV_W - 1
    hp = SSD_HEADS * SSD_HEADDIM

    w_in = w_ffn_in.astype(BF16)
    w_out = w_ffn_out.astype(BF16)
    pad = LANES - SSD_HEADS
    w_proj = jnp.pad(ssm_w_in[0], ((0, 0), (0, pad))).astype(BF16)
    ssd_prm = (ssm_conv_w[0], ssm_conv_b[0].reshape(1, CONV_DIM),
               jnp.pad(ssm_dt_bias[0], (0, pad)).reshape(1, LANES),
               jnp.pad(ssm_a_log[0], (0, pad)).reshape(1, LANES),
               jnp.repeat(ssm_d[0], SSD_HEADDIM).reshape(1, D_INNER),
               ssm_norm_w[0].reshape(1, D_INNER))
    w_ssm_out = ssm_w_out[0].astype(BF16)
    w_qkv = attn_w_qkv[0].astype(BF16)
    w_o = attn_w_o[0].astype(BF16)
    slope_tab = jnp.asarray(_alibi_slopes(), F32)

    conv_states = [jnp.zeros((bp, hist, CONV_DIM), F32), state_conv[0]]
    ssm_states = [jnp.zeros((bp, hp, D_STATE), F32), state_ssm[0].reshape(bs, hp, D_STATE)]
    caches_t = [jnp.transpose(c[0], (0, 2, 3, 4, 1)).reshape(bs, 2 * ATTN_WIDTH, c.shape[2])
                for c in (cache_kv_g0, cache_kv_g1, cache_kv_g2)]

    conv_out, ssm_out = [], []
    for n, (x, (b, l)) in enumerate(zip(xs_all, dims)):
        x = _ffn(x, norm_w[0, 0], w_in, w_out, (0, 0))
        mixer = None
        if l % (CHUNKS_PER_STEP * SSD_CHUNK) == 0:
            x, new_conv, new_ssm = _ssd_prompt(x, norm_w[0, 1], w_proj, conv_states[n], ssm_states[n], ssd_prm,
                                               w_ssm_out, b, l)
        else:
            z, xbc, dt_raw = _norm_mm(x, norm_w[0, 1], w_proj, (D_INNER, CONV_DIM, LANES))
            gz, new_conv, new_ssm = _ssd_sample(z, xbc, dt_raw, conv_states[n], ssm_states[n], ssd_prm, b, l)
            mixer = (gz, w_ssm_out)
        conv_out.append(new_conv[None])
        ssm_out.append(new_ssm.reshape(1, b, SSD_HEADS, SSD_HEADDIM, D_STATE))
        xs_all[n] = _ffn(x, norm_w[0, 2], w_in, w_out, (0, 1), mixer=mixer)

    kv_out = []
    for n, (x, (b, l)) in enumerate(zip(xs_all, dims)):
        x = _ffn(x, norm_w[1, 0], w_in, w_out, (1, 0))
        (qkv,) = _norm_mm(x, norm_w[1, 1], w_qkv, (QKV_WIDTH,), tm=QKV_ROWS)
        if n == 0:
            o = _attn_prompt(qkv, slope_tab, b, l)
            kv_t = [_kv_tail_t(qkv, g, b, l) for g in range(N_ATTN_GROUPS)]
            kv_out.append([jnp.transpose(t.reshape(b, 2, ATTN_HPG, ATTN_HEAD_DIM, t.shape[2]), (0, 4, 1, 2, 3))[None]
                           for t in kv_t])
        else:
            o = _attn_sample(qkv, caches_t, slope_tab, b, l)
            kv_out.append(_kv_rows(qkv, b, l))
        xs_all[n] = _ffn(x, norm_w[1, 2], w_in, w_out, (1, 1), mixer=(o, w_o), g_final=norm_f)

    return (xs_all[0].reshape(bp, lp, d), xs_all[1].reshape(bs, ls, d),
            conv_out[0], conv_out[1], ssm_out[0], ssm_out[1],
            kv_out[0][0], kv_out[1][0], kv_out[0][1], kv_out[1][1], kv_out[0][2], kv_out[1][2])
```

```python
import functools

import jax
import jax.numpy as jnp
from jax import lax
from jax.experimental import pallas as pl
from jax.experimental.pallas import tpu as pltpu

F32 = jnp.float32
BF16 = jnp.bfloat16

EPS = 1e-6
NEG_INF = -1e30

D_FF = 2816
D_INNER = 2048
SSD_HEADS = 32
SSD_HEADDIM = 64
SSD_GROUPS = 8
SSD_HPG = SSD_HEADS // SSD_GROUPS
GROUP_W = SSD_HPG * SSD_HEADDIM
D_STATE = 128
CONV_W = 4
CONV_DIM = D_INNER + 2 * SSD_GROUPS * D_STATE
SSD_CHUNK = 128
ATTN_PATTERNS = ((128, 1), (512, 4), (2048, 16))
N_ATTN_GROUPS = 3
ATTN_HPG = 8
ATTN_HEAD_DIM = 64
ATTN_WIDTH = ATTN_HPG * ATTN_HEAD_DIM
QKV_WIDTH = N_ATTN_GROUPS * 3 * ATTN_WIDTH
ATTN_QBLOCK = 128
ATTN_NKEYS = 128

LANES = 128
SUBLANES = 8
VMEM_LIMIT = 56 * 1024 * 1024

FFN_ROWS = 1024
FFN_CHUNKS = (1536, 1280)
PROJ_ROWS = 256
QKV_ROWS = 512
KV_TAIL_TOKENS = 512

NT_DIMS = (((1,), (1,)), ((), ()))
TN_DIMS = (((0,), (0,)), ((), ()))


def _const_spec(shape):
    zeros = (0,) * len(shape)
    return pl.BlockSpec(shape, lambda *_: zeros, pipeline_mode=pl.Buffered(1))


def _params(*semantics):
    return pltpu.CompilerParams(dimension_semantics=semantics, vmem_limit_bytes=VMEM_LIMIT)


def _rms(x, g):
    return x * lax.rsqrt(jnp.mean(x * x, axis=-1, keepdims=True) + EPS) * g


def _silu(x):
    return x * jax.nn.sigmoid(x)


def _softplus(x):
    return jnp.maximum(x, 0.0) + jnp.log1p(jnp.exp(-jnp.abs(x)))


def _ffn_body(x_ref, g_ref, win_ref, wout_ref, *rest, mixer_proj, final_norm):
    rest = list(rest)
    o_ref = rest.pop()
    x = x_ref[...]
    if mixer_proj:
        y_ref, wy_ref = rest[:2]
        x = x + jnp.dot(y_ref[...].astype(BF16), wy_ref[...], preferred_element_type=F32)
    if final_norm:
        gf_ref = rest[-1]
    h = _rms(x, g_ref[...]).astype(BF16)
    acc = jnp.zeros_like(x)
    lo = 0
    for width in FFN_CHUNKS:
        a = jnp.dot(h, win_ref[:, lo:lo + width], preferred_element_type=F32)
        b = jnp.dot(h, win_ref[:, D_FF + lo:D_FF + lo + width], preferred_element_type=F32)
        t = (_silu(a) * b).astype(BF16)
        acc = acc + jnp.dot(t, wout_ref[lo:lo + width, :], preferred_element_type=F32)
        lo += width
    y = x + 0.5 * acc
    if final_norm:
        y = _rms(y, gf_ref[...])
    o_ref[...] = y


def _ffn(x, g, w_in, w_out, which, mixer=None, g_final=None, tm=FFN_ROWS):
    m, d = x.shape
    tm = min(tm, m)
    row = pl.BlockSpec((tm, d), lambda i: (i, 0))

    def picked(w):
        return pl.BlockSpec((None, None) + w.shape[2:], lambda i: which + (0, 0), pipeline_mode=pl.Buffered(1))

    in_specs = [row, _const_spec((1, d)), picked(w_in), picked(w_out)]
    args = [x, g.reshape(1, d), w_in, w_out]
    if mixer is not None:
        y, w_y = mixer
        in_specs += [pl.BlockSpec((tm, y.shape[1]), lambda i: (i, 0)), _const_spec(w_y.shape)]
        args += [y, w_y]
    if g_final is not None:
        in_specs.append(_const_spec((1, d)))
        args.append(g_final.reshape(1, d))
    return pl.pallas_call(
        functools.partial(_ffn_body, mixer_proj=mixer is not None, final_norm=g_final is not None),
        out_shape=jax.ShapeDtypeStruct((m, d), F32),
        grid=(m // tm,),
        in_specs=in_specs,
        out_specs=row,
        compiler_params=_params("parallel"),
        name="ffn",
    )(*args)


def _norm_mm_body(x_ref, g_ref, w_ref, *o_refs, splits):
    h = _rms(x_ref[...], g_ref[...]).astype(BF16)
    off = 0
    for o_ref, n in zip(o_refs, splits):
        o_ref[...] = jnp.dot(h, w_ref[:, off:off + n], preferred_element_type=F32)
        off += n


def _norm_mm(x, g, w, splits, tm=PROJ_ROWS):
    m, d = x.shape
    tm = min(tm, m)
    return pl.pallas_call(
        functools.partial(_norm_mm_body, splits=splits),
        out_shape=[jax.ShapeDtypeStruct((m, n), F32) for n in splits],
        grid=(m // tm,),
        in_specs=[pl.BlockSpec((tm, d), lambda i: (i, 0)), _const_spec((1, d)), _const_spec(w.shape)],
        out_specs=[pl.BlockSpec((tm, n), lambda i: (i, 0)) for n in splits],
        compiler_params=_params("parallel"),
        name="norm_mm",
    )(x, g.reshape(1, d), w)


def _expand_heads(v, first, width):
    rows = v.shape[0]
    return jnp.concatenate(
        [jnp.broadcast_to(v[:, first + j:first + j + 1], (rows, width)) for j in range(SSD_HPG)], axis=1)


def _split_bf16(v):
    parts, rest = [], v
    for _ in range(3):
        parts.append(rest.astype(BF16))
        rest = rest - parts[-1].astype(F32)
    return jnp.concatenate(parts, axis=1)


def _ssd_chunk(q, xpad_ref, row0, z, dt_raw, cw_ref, cb_ref, dtb_ref, alog_ref, drep_ref, ng_ref,
               h_in_ref, h_out_ref, gz_ref, spread_ref=None):
    def conv(slab):
        sl = slice(slab * LANES, (slab + 1) * LANES)
        acc = cb_ref[:, sl]
        for k in range(CONV_W):
            lo = row0 - (CONV_W - 1) + k
            acc = acc + cw_ref[k:k + 1, sl] * xpad_ref[slab, lo:lo + q, :]
        return _silu(acc)

    x_slabs = D_INNER // LANES
    bc_slabs = SSD_GROUPS * D_STATE // LANES

    dt = _softplus(dt_raw + dtb_ref[...])
    a = -jnp.exp(alog_ref[...])
    row = lax.broadcasted_iota(jnp.int32, (q, q), 0)
    col = lax.broadcasted_iota(jnp.int32, (q, q), 1)
    causal = row >= col
    if q == SUBLANES:
        a_cum = dt * a
        row_q = lax.broadcasted_iota(jnp.int32, (q, LANES), 0)
        for shift in (1, 2, 4):
            a_cum = a_cum + jnp.where(row_q >= shift, pltpu.roll(a_cum, shift, 0), 0.0)
    else:
        pieces = _split_bf16(dt * a)
        tri = causal.astype(BF16)
        a_cum = jnp.dot(jnp.concatenate([tri] * 3, axis=1),
                        jnp.concatenate([pieces[:, k * LANES:(k + 1) * LANES] for k in range(3)], axis=0),
                        preferred_element_type=F32)
    a_last = a_cum[q - 1:q, :]
    w_state = jnp.exp(a_last - a_cum) * dt
    e_cum = jnp.exp(a_cum)
    e_last = jnp.exp(a_last)
    a_cum_t = a_cum.T
    dt_t = dt.T
    if spread_ref is not None:
        lane_head = lax.broadcasted_iota(jnp.int32, (q, GROUP_W), 1) // SSD_HEADDIM
        e_cum_x = jnp.dot(_split_bf16(e_cum), spread_ref[...], preferred_element_type=F32)
        w_state_x = jnp.dot(_split_bf16(w_state), spread_ref[...], preferred_element_type=F32)

    for g in range(SSD_GROUPS):
        h0 = g * SSD_HPG
        c0 = g * GROUP_W
        bmb = conv(x_slabs + g).astype(BF16)
        cmb = conv(x_slabs + bc_slabs + g).astype(BF16)
        cb = lax.dot_general(cmb, bmb, NT_DIMS, preferred_element_type=F32)
        x_g = jnp.concatenate([conv(c0 // LANES + i) for i in range(GROUP_W // LANES)], axis=1)
        wts = []
        for j in range(SSD_HPG):
            h = h0 + j
            seg = a_cum[:, h:h + 1] - a_cum_t[h:h + 1, :]
            decay = jnp.exp(jnp.where(causal, seg, -jnp.inf))
            wts.append((cb * decay * dt_t[h:h + 1, :]).astype(BF16))
        if spread_ref is None:
            y_diag = jnp.concatenate(
                [jnp.dot(w, x_g[:, j * SSD_HEADDIM:(j + 1) * SSD_HEADDIM].astype(BF16), preferred_element_type=F32)
                 for j, w in enumerate(wts)], axis=1)
        else:
            x_gb = x_g.astype(BF16)
            x_heads = jnp.concatenate(
                [jnp.where(lane_head == j, x_gb, jnp.zeros_like(x_gb)) for j in range(SSD_HPG)], axis=0)
            y_diag = jnp.dot(jnp.concatenate(wts, axis=1), x_heads, preferred_element_type=F32)

        h_prev = h_in_ref[c0:c0 + GROUP_W, :]
        y_off = lax.dot_general(cmb, h_prev.astype(BF16), NT_DIMS, preferred_element_type=F32)
        if spread_ref is None:
            y_off = y_off * _expand_heads(e_cum, h0, SSD_HEADDIM)
            xw = (x_g * _expand_heads(w_state, h0, SSD_HEADDIM)).astype(BF16)
        else:
            y_off = y_off * e_cum_x[:, c0:c0 + GROUP_W]
            xw = (x_g * w_state_x[:, c0:c0 + GROUP_W]).astype(BF16)
        states = lax.dot_general(xw, bmb, TN_DIMS, preferred_element_type=F32)
        carry = jnp.concatenate(
            [jnp.broadcast_to(e_last[:, h0 + j:h0 + j + 1], (SSD_HEADDIM, D_STATE)) for j in range(SSD_HPG)], axis=0)
        h_out_ref[c0:c0 + GROUP_W, :] = carry * h_prev + states

        y = y_diag + y_off + drep_ref[:, c0:c0 + GROUP_W] * x_g
        gz = y * _silu(z[:, c0:c0 + GROUP_W])
        gz = gz * lax.rsqrt(jnp.mean(gz * gz, axis=-1, keepdims=True) + EPS) * ng_ref[:, c0:c0 + GROUP_W]
        gz_ref[:, c0:c0 + GROUP_W] = gz.astype(gz_ref.dtype)


PAD_ROWS = 8


CONV_SLABS = CONV_DIM // LANES
CONV_HIST = CONV_W - 1
CHUNKS_PER_STEP = 2


def _ssd_prompt_body(x_ref, g_ref, wp_ref, cs_ref, h0_ref, cw_ref, cb_ref, dtb_ref, alog_ref, drep_ref, ng_ref,
                     wo_ref, spread_ref, o_ref, nc_ref, hout_ref, xpad_ref, gz_ref):
    q = SSD_CHUNK
    rows = CHUNKS_PER_STEP * q

    @pl.when(pl.program_id(1) == 0)
    def _():
        for j in range(CONV_SLABS):
            xpad_ref[j, PAD_ROWS - CONV_HIST:PAD_ROWS, :] = cs_ref[0, :, j * LANES:(j + 1) * LANES]
        hout_ref[0] = h0_ref[0]

    x = x_ref[...]
    h = _rms(x, g_ref[...]).astype(BF16)
    z = jnp.dot(h, wp_ref[:, :D_INNER], preferred_element_type=F32)
    dt_raw = jnp.dot(h, wp_ref[:, D_INNER + CONV_DIM:], preferred_element_type=F32)
    for j in range(0, CONV_SLABS, 2):
        lo = D_INNER + j * LANES
        pair = jnp.dot(h, wp_ref[:, lo:lo + 2 * LANES], preferred_element_type=F32)
        xpad_ref[j, PAD_ROWS:PAD_ROWS + rows, :] = pair[:, :LANES]
        xpad_ref[j + 1, PAD_ROWS:PAD_ROWS + rows, :] = pair[:, LANES:]
    state = hout_ref.at[0]
    for c in range(CHUNKS_PER_STEP):
        sl = slice(c * q, (c + 1) * q)
        _ssd_chunk(q, xpad_ref, PAD_ROWS + c * q, z[sl], dt_raw[sl], cw_ref, cb_ref, dtb_ref, alog_ref, drep_ref,
                   ng_ref, state, state, gz_ref.at[pl.ds(c * q, q)], spread_ref)
    o_ref[...] = x + jnp.dot(gz_ref[...], wo_ref[...], preferred_element_type=F32)
    for j in range(CONV_SLABS):
        tail = xpad_ref[j, PAD_ROWS + rows - CONV_HIST:PAD_ROWS + rows, :]
        nc_ref[0, :, j * LANES:(j + 1) * LANES] = tail
        xpad_ref[j, PAD_ROWS - CONV_HIST:PAD_ROWS, :] = tail


def _ssd_prompt(x, g, w_proj, conv_state, ssm_state, prm, w_out, batch, seq):
    rows = CHUNKS_PER_STEP * SSD_CHUNK
    steps = seq // rows
    d = x.shape[1]
    hp = SSD_HEADS * SSD_HEADDIM
    row = lambda b, c: (b * steps + c, 0)
    per_b = lambda b, c: (b, 0, 0)
    head_of_row = jnp.arange(3 * LANES)[:, None] % LANES
    spread = (head_of_row == jnp.arange(SSD_HEADS * SSD_HEADDIM)[None, :] // SSD_HEADDIM).astype(BF16)
    consts = list(prm) + [w_out, spread]
    return pl.pallas_call(
        _ssd_prompt_body,
        out_shape=[jax.ShapeDtypeStruct(x.shape, F32),
                   jax.ShapeDtypeStruct((batch, CONV_HIST, CONV_DIM), F32),
                   jax.ShapeDtypeStruct((batch, hp, D_STATE), F32)],
        grid=(batch, steps),
        in_specs=[pl.BlockSpec((rows, d), row), _const_spec((1, d)), _const_spec(w_proj.shape),
                  pl.BlockSpec((1, CONV_HIST, CONV_DIM), per_b), pl.BlockSpec((1, hp, D_STATE), per_b)]
        + [_const_spec(c.shape) for c in consts],
        out_specs=[pl.BlockSpec((rows, d), row), pl.BlockSpec((1, CONV_HIST, CONV_DIM), per_b),
                   pl.BlockSpec((1, hp, D_STATE), per_b)],
        scratch_shapes=[pltpu.VMEM((CONV_SLABS, PAD_ROWS + rows, LANES), F32), pltpu.VMEM((rows, D_INNER), BF16)],
        compiler_params=_params("parallel", "arbitrary"),
        name="ssd_prompt",
    )(x, g.reshape(1, d), w_proj, conv_state, ssm_state, *consts)


SEQS_PER_STEP = 2


def _ssd_sample_body(z_ref, xbc_ref, dt_ref, cs_ref, h0_ref, cw_ref, cb_ref, dtb_ref, alog_ref, drep_ref, ng_ref,
                     gz_ref, nc_ref, hout_ref, xpad_ref, *, q):
    span = PAD_ROWS + q
    for e in range(SEQS_PER_STEP):
        row0 = e * span + PAD_ROWS
        sl = slice(e * q, (e + 1) * q)
        for j in range(CONV_SLABS):
            lanes = slice(j * LANES, (j + 1) * LANES)
            xpad_ref[j, row0 - CONV_HIST:row0, :] = cs_ref[e, :, lanes]
            xpad_ref[j, row0:row0 + q, :] = xbc_ref[sl, lanes]
        _ssd_chunk(q, xpad_ref, row0, z_ref[sl, :], dt_ref[sl, :], cw_ref, cb_ref, dtb_ref, alog_ref, drep_ref,
                   ng_ref, h0_ref.at[e], hout_ref.at[e], gz_ref.at[pl.ds(e * q, q)])
        for j in range(CONV_SLABS):
            nc_ref[e, :, j * LANES:(j + 1) * LANES] = xpad_ref[j, row0 + q - CONV_HIST:row0 + q, :]


def _ssd_sample(z, xbc, dt_raw, conv_state, ssm_state, prm, batch, seq):
    n = SEQS_PER_STEP
    hp = SSD_HEADS * SSD_HEADDIM
    row = lambda b: (b, 0)
    per_b = lambda b: (b, 0, 0)
    return pl.pallas_call(
        functools.partial(_ssd_sample_body, q=seq),
        out_shape=[jax.ShapeDtypeStruct((batch * seq, D_INNER), F32),
                   jax.ShapeDtypeStruct((batch, CONV_HIST, CONV_DIM), F32),
                   jax.ShapeDtypeStruct((batch, hp, D_STATE), F32)],
        grid=(batch // n,),
        in_specs=[pl.BlockSpec((n * seq, D_INNER), row), pl.BlockSpec((n * seq, CONV_DIM), row),
                  pl.BlockSpec((n * seq, LANES), row),
                  pl.BlockSpec((n, CONV_HIST, CONV_DIM), per_b), pl.BlockSpec((n, hp, D_STATE), per_b)]
        + [_const_spec(c.shape) for c in prm],
        out_specs=[pl.BlockSpec((n * seq, D_INNER), row), pl.BlockSpec((n, CONV_HIST, CONV_DIM), per_b),
                   pl.BlockSpec((n, hp, D_STATE), per_b)],
        scratch_shapes=[pltpu.VMEM((CONV_SLABS, n * (PAD_ROWS + seq), LANES), F32)],
        compiler_params=_params("parallel"),
        name="ssd_sample",
    )(z, xbc, dt_raw, conv_state, ssm_state, *prm)


ATTN_SLAB = ATTN_QBLOCK * 16
HEAD_PAIRS = ATTN_HPG // 2


def _attn_prompt_body(slope_ref, *refs):
    qkv_refs = refs[:9]
    o_ref = refs[9]
    m_sc, l_sc, acc_sc = refs[10:13]
    kv_ext = refs[13:]
    pair = pl.program_id(1)
    s_idx = pl.program_id(2)
    qb = ATTN_QBLOCK

    for g, (_, dil) in enumerate(ATTN_PATTERNS):
        sub = ATTN_SLAB // dil
        for ext, src in zip(kv_ext[2 * g:2 * g + 2], qkv_refs[3 * g + 1:3 * g + 3]):
            @pl.when(s_idx == 0)
            def _(ext=ext, dil=dil):
                ext[:, 0:qb, :] = jnp.zeros((dil, qb, LANES), BF16)

            for r in range(dil):
                picked = src[...] if dil == 1 else src[pl.ds(r, sub, stride=dil), :]
                ext[r, qb:qb + sub, :] = picked.astype(BF16)

    row = lax.broadcasted_iota(jnp.int32, (qb, 2 * qb), 0)
    col = lax.broadcasted_iota(jnp.int32, (qb, 2 * qb), 1)
    dist = qb + row - col
    in_window = (dist >= 0) & (dist <= ATTN_NKEYS)
    valid_first = in_window & (col >= jnp.where(s_idx > 0, 0, qb))
    dist_f = dist.astype(F32)
    lane = lax.broadcasted_iota(jnp.int32, (qb, LANES), 1)
    low_half = lane < ATTN_HEAD_DIM

    for g, (_, dil) in enumerate(ATTN_PATTERNS):
        q_ref = qkv_refs[3 * g]
        k_ext, v_ext = kv_ext[2 * g:2 * g + 2]
        n_blk = ATTN_SLAB // (qb * dil)
        slopes = [slope_ref[g * ATTN_HPG + 2 * pair + e] * float(dil) for e in range(2)]
        bias_in = [jnp.where(in_window, -slopes[e] * dist_f, NEG_INF) for e in range(2)]
        bias_first = [jnp.where(valid_first, -slopes[e] * dist_f, NEG_INF) for e in range(2)]

        for r, blk in [(r, blk) for r in range(dil) for blk in range(n_blk)]:
            start = r + (dil * qb) * blk
            rows = pl.ds(start, qb) if dil == 1 else pl.ds(start, qb, stride=dil)
            k2 = k_ext[r, blk * qb:(blk + 2) * qb, :]
            v2 = v_ext[r, blk * qb:(blk + 2) * qb, :]
            q2 = q_ref[rows, :] * (ATTN_HEAD_DIM ** -0.5)
            bias = bias_in if blk > 0 else bias_first
            ms, ls, pvs = [], [], []
            for e in range(2):
                qm = jnp.where(low_half if e == 0 else ~low_half, q2, 0.0).astype(BF16)
                s = lax.dot_general(qm, k2, NT_DIMS, preferred_element_type=F32) + bias[e]
                m = jnp.max(s, axis=-1, keepdims=True)
                p = jnp.exp(s - m)
                ms.append(m)
                ls.append(jnp.sum(p, axis=-1, keepdims=True))
                pvs.append(jnp.dot(p.astype(BF16), v2, preferred_element_type=F32))
            m_new = jnp.where(low_half, ms[0], ms[1])
            l_new = jnp.where(low_half, ls[0], ls[1])
            pv_new = jnp.where(low_half, pvs[0], pvs[1])
            if g > 0:
                m_old = m_sc[rows, :]
                m_tot = jnp.maximum(m_old, m_new)
                a_old = jnp.exp(m_old - m_tot)
                a_new = jnp.exp(m_new - m_tot)
                l_new = a_old * l_sc[rows, :] + a_new * l_new
                pv_new = a_old * acc_sc[rows, :] + a_new * pv_new
                m_new = m_tot
            if g == N_ATTN_GROUPS - 1:
                o_ref[rows, :] = pv_new / l_new
            else:
                m_sc[rows, :] = m_new
                l_sc[rows, :] = l_new
                acc_sc[rows, :] = pv_new

    for g, (_, dil) in enumerate(ATTN_PATTERNS):
        sub = ATTN_SLAB // dil
        for ext in kv_ext[2 * g:2 * g + 2]:
            ext[:, 0:qb, :] = ext[:, sub:sub + qb, :]


def _attn_prompt(qkv, slope_tab, batch, seq):
    n_slab = seq // ATTN_SLAB
    col_blocks = ATTN_WIDTH // LANES

    def spec(g, which):
        base = (3 * g + which) * col_blocks
        return pl.BlockSpec((ATTN_SLAB, LANES), lambda b, p, s: (b * n_slab + s, base + p))

    in_specs = [pl.BlockSpec(memory_space=pltpu.SMEM)]
    in_specs += [spec(g, which) for g in range(N_ATTN_GROUPS) for which in range(3)]
    return pl.pallas_call(
        _attn_prompt_body,
        out_shape=jax.ShapeDtypeStruct((batch * seq, ATTN_WIDTH), F32),
        grid=(batch, HEAD_PAIRS, n_slab),
        in_specs=in_specs,
        out_specs=pl.BlockSpec((ATTN_SLAB, LANES), lambda b, p, s: (b * n_slab + s, p)),
        scratch_shapes=[pltpu.VMEM((ATTN_SLAB, LANES), F32)] * 3 + [
            pltpu.VMEM((dil, ATTN_QBLOCK + ATTN_SLAB // dil, LANES), BF16)
            for _, dil in ATTN_PATTERNS for _ in range(2)],
        compiler_params=_params("parallel", "parallel", "arbitrary"),
        name="attn_prompt",
    )(slope_tab, *([qkv] * 9))


def _kv_tail_body(k_ref, v_ref, o_ref):
    o_ref[0, :ATTN_WIDTH, :] = k_ref[...].T
    o_ref[0, ATTN_WIDTH:, :] = v_ref[...].T


def _kv_tail_t(qkv, g, batch, seq, tm=KV_TAIL_TOKENS):
    win = min(ATTN_PATTERNS[g][0], seq)
    tm = min(tm, win)
    first = (seq - win) // tm
    per_b = seq // tm

    def spec(which):
        return pl.BlockSpec((tm, ATTN_WIDTH), lambda b, j: (b * per_b + first + j, 3 * g + which))

    return pl.pallas_call(
        _kv_tail_body,
        out_shape=jax.ShapeDtypeStruct((batch, 2 * ATTN_WIDTH, win), F32),
        grid=(batch, win // tm),
        in_specs=[spec(1), spec(2)],
        out_specs=pl.BlockSpec((1, 2 * ATTN_WIDTH, tm), lambda b, j: (b, 0, j)),
        compiler_params=_params("parallel", "parallel"),
        name=f"kv_tail_g{g}",
    )(qkv, qkv)


SAMPLE_SEQS_PER_STEP = 2


def _attn_sample_body(slope_ref, qkv_ref, c0_ref, c1_ref, c2_ref, o_ref, *, n_new):
    caches = (c0_ref, c1_ref, c2_ref)
    rows = ATTN_HPG * n_new
    row_head = lax.broadcasted_iota(jnp.int32, (rows, ATTN_WIDTH), 0) // n_new
    lane_head = lax.broadcasted_iota(jnp.int32, (rows, ATTN_WIDTH), 1) // ATTN_HEAD_DIM
    diag = row_head == lane_head
    head_col = lax.broadcasted_iota(jnp.int32, (rows, 1), 0) // n_new
    qi_n = lax.broadcasted_iota(jnp.int32, (rows, n_new), 0) % n_new
    gap_n = qi_n - lax.broadcasted_iota(jnp.int32, (rows, n_new), 1)
    geometry = []
    for g, (c_ref, (win, dil)) in enumerate(zip(caches, ATTN_PATTERNS)):
        past_len = c_ref.shape[2]
        slope = jnp.zeros((rows, 1), F32)
        for h in range(ATTN_HPG):
            slope = jnp.where(head_col == h, slope_ref[g * ATTN_HPG + h], slope)
        gap_p = (past_len + lax.broadcasted_iota(jnp.int32, (rows, past_len), 0) % n_new
                 - lax.broadcasted_iota(jnp.int32, (rows, past_len), 1))
        ok_p = (gap_p <= win) & ((gap_p & (dil - 1)) == 0)
        ok_n = (gap_n >= 0) & ((gap_n & (dil - 1)) == 0)
        geometry.append((ok_p, slope * gap_p.astype(F32), ok_n, slope * gap_n.astype(F32)))

    for e in range(SAMPLE_SEQS_PER_STEP):
        qkv = qkv_ref[e * n_new:(e + 1) * n_new, :]
        outs, lses = [], []
        for g, c_ref in enumerate(caches):
            base = g * 3 * ATTN_WIDTH
            ok_p, bias_p, ok_n, bias_n = geometry[g]
            q = qkv[:, base:base + ATTN_WIDTH] * (ATTN_HEAD_DIM ** -0.5)
            q_bd = jnp.where(diag, jnp.concatenate([q] * ATTN_HPG, axis=0), 0.0).astype(BF16)
            k_new = qkv[:, base + ATTN_WIDTH:base + 2 * ATTN_WIDTH].astype(BF16)
            v_new = qkv[:, base + 2 * ATTN_WIDTH:base + 3 * ATTN_WIDTH].astype(BF16)
            k_t = c_ref[e, :ATTN_WIDTH, :].astype(BF16)
            v_t = c_ref[e, ATTN_WIDTH:, :].astype(BF16)
            s_p = jnp.dot(q_bd, k_t, preferred_element_type=F32) - bias_p
            s_p = jnp.where(ok_p, s_p, NEG_INF)
            s_n = lax.dot_general(q_bd, k_new, NT_DIMS, preferred_element_type=F32) - bias_n
            s_n = jnp.where(ok_n, s_n, NEG_INF)
            mx = jnp.maximum(jnp.max(s_p, axis=-1, keepdims=True), jnp.max(s_n, axis=-1, keepdims=True))
            e_p = jnp.exp(s_p - mx)
            e_n = jnp.exp(s_n - mx)
            l = jnp.sum(e_p, axis=-1, keepdims=True) + jnp.sum(e_n, axis=-1, keepdims=True)
            o = (lax.dot_general(e_p.astype(BF16), v_t, NT_DIMS, preferred_element_type=F32)
                 + jnp.dot(e_n.astype(BF16), v_new, preferred_element_type=F32)) / l
            outs.append(o)
            lses.append(mx + jnp.log(l))
        top = jnp.maximum(jnp.maximum(lses[0], lses[1]), lses[2])
        es = [jnp.exp(l - top) for l in lses]
        merged = (es[0] * outs[0] + es[1] * outs[1] + es[2] * outs[2]) / (es[0] + es[1] + es[2])
        merged = jnp.where(diag, merged, 0.0).reshape(ATTN_HPG, n_new, ATTN_WIDTH)
        o_ref[e * n_new:(e + 1) * n_new, :] = jnp.sum(merged, axis=0)


def _attn_sample(qkv, caches_t, slope_tab, batch, n_new):
    n = SAMPLE_SEQS_PER_STEP
    specs = [pl.BlockSpec((n,) + c.shape[1:], lambda b: (b, 0, 0)) for c in caches_t]
    return pl.pallas_call(
        functools.partial(_attn_sample_body, n_new=n_new),
        out_shape=jax.ShapeDtypeStruct((batch * n_new, ATTN_WIDTH), F32),
        grid=(batch // n,),
        in_specs=[pl.BlockSpec(memory_space=pltpu.SMEM),
                  pl.BlockSpec((n * n_new, QKV_WIDTH), lambda b: (b, 0))] + specs,
        out_specs=pl.BlockSpec((n * n_new, ATTN_WIDTH), lambda b: (b, 0)),
        compiler_params=_params("parallel"),
        name="attn_sample",
    )(slope_tab, qkv, *caches_t)


KV_ROWS_SEQS = 16


def _kv_rows_body(qkv_ref, *o_refs, n_new):
    for g, o_ref in enumerate(o_refs):
        lo = (3 * g + 1) * ATTN_WIDTH
        for b in range(KV_ROWS_SEQS):
            rows = qkv_ref[b * n_new:(b + 1) * n_new, lo:lo + 2 * ATTN_WIDTH]
            o_ref[0, b] = rows.reshape(n_new, 2, ATTN_HPG, ATTN_HEAD_DIM)


def _kv_rows(qkv, batch, n_new):
    shape = (1, batch, n_new, 2, ATTN_HPG, ATTN_HEAD_DIM)
    block = (1, KV_ROWS_SEQS) + shape[2:]
    return pl.pallas_call(
        functools.partial(_kv_rows_body, n_new=n_new),
        out_shape=[jax.ShapeDtypeStruct(shape, F32)] * N_ATTN_GROUPS,
        grid=(batch // KV_ROWS_SEQS,),
        in_specs=[pl.BlockSpec((KV_ROWS_SEQS * n_new, QKV_WIDTH), lambda i: (i, 0))],
        out_specs=[pl.BlockSpec(block, lambda i: (0, i, 0, 0, 0, 0))] * N_ATTN_GROUPS,
        compiler_params=_params("parallel"),
        name="kv_rows",
    )(qkv)


def _alibi_slopes():
    n_heads = N_ATTN_GROUPS * ATTN_HPG
    return [2.0 ** (-8.0 * (h + 1) / n_heads) for h in range(n_heads)]


def kernel(x_prompt, x_sample, state_conv, state_ssm, cache_kv_g0, cache_kv_g1, cache_kv_g2, norm_w, w_ffn_in,
           w_ffn_out, ssm_w_in, ssm_conv_w, ssm_conv_b, ssm_dt_bias, ssm_a_log, ssm_d, ssm_norm_w, ssm_w_out,
           attn_w_qkv, attn_w_o, norm_f):
    bp, lp, d = x_prompt.shape
    bs, ls, _ = x_sample.shape
    xs_all = [x_prompt.reshape(bp * lp, d), x_sample.reshape(bs * ls, d)]
    dims = [(bp, lp), (bs, ls)]
    hist = CONV_W - 1
    hp = SSD_HEADS * SSD_HEADDIM

    w_in = w_ffn_in.astype(BF16)
    w_out = w_ffn_out.astype(BF16)
    pad = LANES - SSD_HEADS
    w_proj = jnp.pad(ssm_w_in[0], ((0, 0), (0, pad))).astype(BF16)
    ssd_prm = (ssm_conv_w[0], ssm_conv_b[0].reshape(1, CONV_DIM),
               jnp.pad(ssm_dt_bias[0], (0, pad)).reshape(1, LANES),
               jnp.pad(ssm_a_log[0], (0, pad)).reshape(1, LANES),
               jnp.repeat(ssm_d[0], SSD_HEADDIM).reshape(1, D_INNER),
               ssm_norm_w[0].reshape(1, D_INNER))
    w_ssm_out = ssm_w_out[0].astype(BF16)
    w_qkv = attn_w_qkv[0].astype(BF16)
    w_o = attn_w_o[0].astype(BF16)
    slope_tab = jnp.asarray(_alibi_slopes(), F32)

    conv_states = [jnp.zeros((bp, hist, CONV_DIM), F32), state_conv[0]]
    ssm_states = [jnp.zeros((bp, hp, D_STATE), F32), state_ssm[0].reshape(bs, hp, D_STATE)]
    caches_t = [jnp.transpose(c[0], (0, 2, 3, 4, 1)).reshape(bs, 2 * ATTN_WIDTH, c.shape[2])
                for c in (cache_kv_g0, cache_kv_g1, cache_kv_g2)]

    conv_out, ssm_out = [], []
    for n, (x, (b, l)) in enumerate(zip(xs_all, dims)):
        x = _ffn(x, norm_w[0, 0], w_in, w_out, (0, 0))
        mixer = None
        if l % (CHUNKS_PER_STEP * SSD_CHUNK) == 0:
            x, new_conv, new_ssm = _ssd_prompt(x, norm_w[0, 1], w_proj, conv_states[n], ssm_states[n], ssd_prm,
                                               w_ssm_out, b, l)
        else:
            z, xbc, dt_raw = _norm_mm(x, norm_w[0, 1], w_proj, (D_INNER, CONV_DIM, LANES))
            gz, new_conv, new_ssm = _ssd_sample(z, xbc, dt_raw, conv_states[n], ssm_states[n], ssd_prm, b, l)
            mixer = (gz, w_ssm_out)
        conv_out.append(new_conv[None])
        ssm_out.append(new_ssm.reshape(1, b, SSD_HEADS, SSD_HEADDIM, D_STATE))
        xs_all[n] = _ffn(x, norm_w[0, 2], w_in, w_out, (0, 1), mixer=mixer)

    kv_out = []
    for n, (x, (b, l)) in enumerate(zip(xs_all, dims)):
        x = _ffn(x, norm_w[1, 0], w_in, w_out, (1, 0))
        (qkv,) = _norm_mm(x, norm_w[1, 1], w_qkv, (QKV_WIDTH,), tm=QKV_ROWS)
        if n == 0:
            o = _attn_prompt(qkv, slope_tab, b, l)
            kv_t = [_kv_tail_t(qkv, g, b, l) for g in range(N_ATTN_GROUPS)]
            kv_out.append([jnp.transpose(t.reshape(b, 2, ATTN_HPG, ATTN_HEAD_DIM, t.shape[2]), (0, 4, 1, 2, 3))[None]
                           for t in kv_t])
        else:
            o = _attn_sample(qkv, caches_t, slope_tab, b, l)
            kv_out.append(_kv_rows(qkv, b, l))
        xs_all[n] = _ffn(x, norm_w[1, 2], w_in, w_out, (1, 1), mixer=(o, w_o), g_final=norm_f)

    return (xs_all[0].reshape(bp, lp, d), xs_all[1].reshape(bs, ls, d),
            conv_out[0], conv_out[1], ssm_out[0], ssm_out[1],
            kv_out[0][0], kv_out[1][0], kv_out[0][1], kv_out[1][1], kv_out[0][2], kv_out[1][2])
```
